```python
import jax, jax.numpy as jnp
from jax import lax
import numpy as np

D_MODEL = 1024
BATCH = 8
SEQ = 2048
DEPTH = 1

EPS = 1e-6
A_HEADS = 16
A_QK_DIM = 64
A_V_DIM = 64
A_Q_RANK = 256
A_KV_RANK = 256
IDX_HEADS = 8
IDX_DIM = 64
TOPK_MAX = 256
Q_BLOCK = 128
A_WIDTH = A_HEADS * A_V_DIM
B_HEADS = 8
B_K_DIM = 128
B_V_DIM = 128
B_QK = B_HEADS * B_K_DIM
B_VW = B_HEADS * B_V_DIM
CONV_WIDTH = 4
CHUNK = 64
D_FF = ((8 * D_MODEL // 3 + 255) // 256) * 256
IN_SIZES = (A_Q_RANK, A_KV_RANK, IDX_DIM, IDX_HEADS,
            B_QK, B_QK, B_VW, B_HEADS, B_HEADS, B_VW,
            D_MODEL, D_MODEL)
IN_WIDTH = (A_Q_RANK + A_KV_RANK + IDX_DIM + IDX_HEADS
            + 2 * B_QK + 2 * B_VW + 2 * B_HEADS + 2 * D_MODEL)

kernel_name = "hybrid_dsa_gdn_gated_merge"


def rmsnorm(x, g):
    xf = x.astype(jnp.float32)
    y = xf * lax.rsqrt(jnp.mean(xf * xf, axis=-1, keepdims=True) + EPS)
    return (y * g.astype(jnp.float32)).astype(x.dtype)


def layernorm(x, g, b):
    xf = x.astype(jnp.float32)
    mu = jnp.mean(xf, axis=-1, keepdims=True)
    xc = xf - mu
    y = xc * lax.rsqrt(jnp.mean(xc * xc, axis=-1, keepdims=True) + EPS)
    return (y * g.astype(jnp.float32) + b.astype(jnp.float32)).astype(x.dtype)


def l2norm(x):
    xf = x.astype(jnp.float32)
    return xf * lax.rsqrt(jnp.sum(xf * xf, axis=-1, keepdims=True) + EPS)


def dsa_attention(c_q, c_kv, k_idx_raw, w_idx_raw, cq_g, ckv_g, w_uq, w_uk, w_uv,
                  w_iq, kln_g, kln_b):
    B, L, _ = c_q.shape
    topk = min(TOPK_MAX, L // 4)
    nb = L // Q_BLOCK
    cq = rmsnorm(c_q, cq_g)
    ckv = rmsnorm(c_kv, ckv_g)
    q = (cq @ w_uq).reshape(B, L, A_HEADS, A_QK_DIM)
    q_lat = jnp.einsum('blhd,hdr->blhr', q, w_uk)
    q_idx = (cq @ w_iq).reshape(B, L, IDX_HEADS, IDX_DIM)
    k_idx = layernorm(k_idx_raw, kln_g, kln_b).astype(jnp.float32)
    w_idx = w_idx_raw * (IDX_HEADS ** -0.5 * IDX_DIM ** -0.5)
    key_pos = jnp.arange(L)

    def to_blocks(t):
        return t.reshape(B, nb, Q_BLOCK, *t.shape[2:]).swapaxes(0, 1)

    def block(args):
        i, ql, qi, wi = args
        q_pos = i * Q_BLOCK + jnp.arange(Q_BLOCK)
        causal = key_pos[None, :] <= q_pos[:, None]
        logits = jnp.einsum('bqhd,bsd->bqhs', qi.astype(jnp.float32), k_idx)
        score = jnp.einsum('bqh,bqhs->bqs', wi.astype(jnp.float32), jax.nn.relu(logits))
        score = jnp.where(causal[None], score, -jnp.inf)
        _, sel = lax.top_k(score, topk)
        valid = sel <= q_pos[None, :, None]
        kv_sel = jax.vmap(lambda kv, ix: kv[ix])(ckv, sel)
        s = jnp.einsum('bqhr,bqkr->bqhk', ql, kv_sel).astype(jnp.float32) * (A_QK_DIM ** -0.5)
        s = jnp.where(valid[:, :, None, :], s, -jnp.inf)
        p = jax.nn.softmax(s, axis=-1).astype(ckv.dtype)
        return jnp.einsum('bqhk,bqkr->bqhr', p, kv_sel)

    o_lat = lax.map(block, (jnp.arange(nb), to_blocks(q_lat), to_blocks(q_idx), to_blocks(w_idx)))
    o_lat = o_lat.swapaxes(0, 1).reshape(B, L, A_HEADS, A_KV_RANK)
    o = jnp.einsum('blhr,hrd->blhd', o_lat, w_uv)
    return o.reshape(B, L, A_WIDTH)


def short_conv(x, w):
    C = x.shape[-1]
    return lax.conv_general_dilated(
        x, w[:, None, :].astype(x.dtype), window_strides=(1,),
        padding=[(CONV_WIDTH - 1, 0)], dimension_numbers=('NWC', 'WIO', 'NWC'),
        feature_group_count=C)


def gated_delta_rule(q, k, v, g, beta):
    B, L, H, Dk = q.shape
    Dv = v.shape[-1]
    N = L // CHUNK
    def ch4(t):
        return t.reshape(B, N, CHUNK, H, t.shape[-1]).transpose(1, 0, 3, 2, 4)
    def ch3(t):
        return t.reshape(B, N, CHUNK, H).transpose(1, 0, 3, 2)
    qc = ch4(q * (Dk ** -0.5))
    kc, vc = ch4(k), ch4(v)
    bc = ch3(beta)
    G = jnp.cumsum(ch3(g), axis=-1)
    tri = jnp.tril(jnp.ones((CHUNK, CHUNK), bool))
    strict = jnp.tril(jnp.ones((CHUNK, CHUNK), bool), -1)
    decay = jnp.exp(jnp.where(tri, G[..., :, None] - G[..., None, :], -jnp.inf))
    kk = jnp.einsum('nbhcd,nbhsd->nbhcs', kc, kc)
    A = jnp.where(strict, bc[..., None] * kk * decay, 0.0) + jnp.eye(CHUNK, dtype=jnp.float32)
    rhs = jnp.concatenate([vc * bc[..., None], kc * (bc * jnp.exp(G))[..., None]], axis=-1)
    sol = lax.linalg.triangular_solve(A, rhs, left_side=True, lower=True, unit_diagonal=True)
    u_base, w_dec = sol[..., :Dv], sol[..., Dv:]
    qk = jnp.einsum('nbhcd,nbhsd->nbhcs', qc, kc) * decay
    q_dec = qc * jnp.exp(G)[..., None]
    k_tail = kc * jnp.exp(G[..., -1:] - G)[..., None]
    a_tail = jnp.exp(G[..., -1])

    def step(S, xs):
        u_b, w_d, qk_i, qd, kt, at = xs
        u = u_b - jnp.einsum('bhcd,bhdv->bhcv', w_d, S)
        o = jnp.einsum('bhcd,bhdv->bhcv', qd, S) + jnp.einsum('bhcs,bhsv->bhcv', qk_i, u)
        S = S * at[..., None, None] + jnp.einsum('bhcd,bhcv->bhdv', kt, u)
        return S, o

    S0 = jnp.zeros((B, H, Dk, Dv), jnp.float32)
    _, o = lax.scan(step, S0, (u_base, w_dec, qk, q_dec, k_tail, a_tail))
    return o.transpose(1, 0, 3, 2, 4).reshape(B, L, H, Dv)


def gated_deltanet(q, k, v, b, a, z, conv_w, a_log, dt_bias, onorm_g):
    B, L, _ = q.shape
    dt = q.dtype
    qkv = jax.nn.silu(short_conv(jnp.concatenate([q, k, v], axis=-1), conv_w))
    qs, ks, vs = jnp.split(qkv, [B_QK, 2 * B_QK], axis=-1)
    qs = l2norm(qs.reshape(B, L, B_HEADS, B_K_DIM))
    ks = l2norm(ks.reshape(B, L, B_HEADS, B_K_DIM))
    vs = vs.reshape(B, L, B_HEADS, B_V_DIM).astype(jnp.float32)
    beta = jax.nn.sigmoid(b.astype(jnp.float32))
    g = -jnp.exp(a_log.astype(jnp.float32)) * jax.nn.softplus(a.astype(jnp.float32) + dt_bias.astype(jnp.float32))
    o = gated_delta_rule(qs, ks, vs, g, beta)
    o = rmsnorm(o, onorm_g) * jax.nn.silu(z.reshape(B, L, B_HEADS, B_V_DIM).astype(jnp.float32))
    return o.reshape(B, L, B_VW).astype(dt)


def _w(k, shape, fan_in):
    return jax.random.normal(k, shape, jnp.float32) * (fan_in ** -0.5)


def _gain(k, shape):
    return 1.0 + 0.02 * jax.random.normal(k, shape, jnp.float32)


def setup_inputs(seed: int = 0) -> dict:
    key = jax.random.key(seed)
    ks = jax.random.split(key, 24)
    Dp = DEPTH
    dt0 = jnp.exp(jax.random.uniform(ks[14], (Dp, B_HEADS), jnp.float32, np.log(1e-3), np.log(1e-1)))
    return {
        "x": jax.random.normal(ks[0], (BATCH, SEQ, D_MODEL), jnp.float32),
        "mix_norm_g": _gain(ks[1], (Dp, D_MODEL)),
        "w_in": _w(ks[2], (Dp, D_MODEL, IN_WIDTH), D_MODEL),
        "cq_norm_g": _gain(ks[3], (Dp, A_Q_RANK)),
        "ckv_norm_g": _gain(ks[4], (Dp, A_KV_RANK)),
        "w_uq": _w(ks[5], (Dp, A_Q_RANK, A_HEADS * A_QK_DIM), A_Q_RANK),
        "w_uk": _w(ks[6], (Dp, A_HEADS, A_QK_DIM, A_KV_RANK), A_KV_RANK),
        "w_uv": _w(ks[7], (Dp, A_HEADS, A_KV_RANK, A_V_DIM), A_KV_RANK),
        "w_iq": _w(ks[8], (Dp, A_Q_RANK, IDX_HEADS * IDX_DIM), A_Q_RANK),
        "kidx_ln_g": _gain(ks[9], (Dp, IDX_DIM)),
        "kidx_ln_b": 0.02 * jax.random.normal(ks[10], (Dp, IDX_DIM), jnp.float32),
        "w_branch_a": _w(ks[11], (Dp, A_WIDTH, D_MODEL), A_WIDTH),
        "conv_w": _w(ks[12], (Dp, CONV_WIDTH, 2 * B_QK + B_VW), CONV_WIDTH),
        "a_log": jnp.log(jax.random.uniform(ks[13], (Dp, B_HEADS), jnp.float32, 1.0, 16.0)),
        "dt_bias": dt0 + jnp.log(-jnp.expm1(-dt0)),
        "onorm_g": _gain(ks[15], (Dp, B_V_DIM)),
        "w_branch_b": _w(ks[16], (Dp, B_VW, D_MODEL), B_VW),
        "w_out": _w(ks[17], (Dp, D_MODEL, D_MODEL), D_MODEL),
        "ffn_norm_g": _gain(ks[18], (Dp, D_MODEL)),
        "w_gate": _w(ks[19], (Dp, D_MODEL, D_FF), D_MODEL),
        "w_up": _w(ks[20], (Dp, D_MODEL, D_FF), D_MODEL),
        "w_down": _w(ks[21], (Dp, D_FF, D_MODEL), D_FF),
        "final_norm_g": _gain(ks[22], (D_MODEL,)),
    }


def reference(x, mix_norm_g, w_in, cq_norm_g, ckv_norm_g, w_uq, w_uk, w_uv, w_iq,
              kidx_ln_g, kidx_ln_b, w_branch_a, conv_w, a_log, dt_bias, onorm_g,
              w_branch_b, w_out, ffn_norm_g, w_gate, w_up, w_down, final_norm_g):
    splits = [int(s) for s in np.cumsum(IN_SIZES)[:-1]]
    for l in range(DEPTH):
        h = rmsnorm(x, mix_norm_g[l])
        proj = h @ w_in[l]
        (c_q, c_kv, k_idx, w_idx, q_b, k_b, v_b, beta_b, a_b, z_b,
         gate_a, gate_b) = jnp.split(proj, splits, axis=-1)
        o_a = dsa_attention(c_q, c_kv, k_idx, w_idx, cq_norm_g[l], ckv_norm_g[l],
                            w_uq[l], w_uk[l], w_uv[l], w_iq[l], kidx_ln_g[l], kidx_ln_b[l])
        o_b = gated_deltanet(q_b, k_b, v_b, beta_b, a_b, z_b, conv_w[l], a_log[l],
                             dt_bias[l], onorm_g[l])
        merged = (jax.nn.sigmoid(gate_a) * (o_a @ w_branch_a[l])
                  + jax.nn.sigmoid(gate_b) * (o_b @ w_branch_b[l]))
        x = x + merged @ w_out[l]
        h = rmsnorm(x, ffn_norm_g[l])
        x = x + (jax.nn.silu(h @ w_gate[l]) * (h @ w_up[l])) @ w_down[l]
    return rmsnorm(x, final_norm_g)
```

```python
import functools

import jax
import jax.numpy as jnp
from jax import lax
from jax.experimental import pallas as pl
from jax.experimental.pallas import tpu as pltpu

F32 = jnp.float32
BF16 = jnp.bfloat16

EPS = 1e-6
A_HEADS = 16
A_QK_DIM = 64
A_V_DIM = 64
A_Q_RANK = 256
A_KV_RANK = 256
IDX_HEADS = 8
IDX_DIM = 64
TOPK_MAX = 256
Q_BLOCK = 128
B_HEADS = 8
B_K_DIM = 128
B_V_DIM = 128
CONV_WIDTH = 4
CHUNK = 64

LANES = 128
SMALL_W = LANES
OFF_WIDX = IDX_DIM
OFF_BETA = IDX_DIM + IDX_HEADS
OFF_A = IDX_DIM + IDX_HEADS + B_HEADS
VMEM_LIMIT = 56 * 1024 * 1024
NEG_BIG = -1e30
BISECT_STEPS = 16
HEAD_GROUP = 4


def _resident(shape):
    nd = len(shape)
    return pl.BlockSpec(shape, lambda *_: (0,) * nd, pipeline_mode=pl.Buffered(1))


def _rms(x, g):
    return x * lax.rsqrt(jnp.mean(x * x, axis=-1, keepdims=True) + EPS) * g


def _in_proj_kernel(x_ref, g_ref, w_ref, cqg_ref, ckvg_ref, lng_ref, lnb_ref,
                    cq_ref, ckv_ref, kidx_ref, small_ref, q_ref, k_ref, v_ref, z_ref,
                    ga_ref, gb_ref, *, d_model):
    h = _rms(x_ref[...], g_ref[...]).astype(BF16)
    wide = d_model

    def proj(c0, width):
        return jnp.dot(h, w_ref[:, c0:c0 + width], preferred_element_type=F32)

    c0 = 0
    for ref in (q_ref, k_ref, v_ref, z_ref, ga_ref, gb_ref):
        ref[...] = proj(c0, wide)
        c0 += wide
    cq_ref[...] = _rms(proj(c0, A_Q_RANK), cqg_ref[...]).astype(BF16)
    c0 += A_Q_RANK
    ckv_ref[...] = _rms(proj(c0, A_KV_RANK), ckvg_ref[...]).astype(BF16)
    c0 += A_KV_RANK
    small = proj(c0, SMALL_W)
    small_ref[...] = small
    kraw = small[:, :IDX_DIM]
    mu = jnp.mean(kraw, axis=-1, keepdims=True)
    kc = kraw - mu
    kn = kc * lax.rsqrt(jnp.mean(kc * kc, axis=-1, keepdims=True) + EPS)
    kidx_ref[...] = (kn * lng_ref[...] + lnb_ref[...]).astype(BF16)


def _in_proj(x2, g, w_all, cq_g, ckv_g, ln_g, ln_b, *, tm):
    T, D = x2.shape
    n_all = w_all.shape[1]
    row = lambda w: pl.BlockSpec((tm, w), lambda i: (i, 0))
    out_shapes = (
        jax.ShapeDtypeStruct((T, A_Q_RANK), BF16),
        jax.ShapeDtypeStruct((T, A_KV_RANK), BF16),
        jax.ShapeDtypeStruct((T, IDX_DIM), BF16),
        jax.ShapeDtypeStruct((T, SMALL_W), F32),
    ) + tuple(jax.ShapeDtypeStruct((T, D), F32) for _ in range(6))
    out_specs = (row(A_Q_RANK), row(A_KV_RANK), row(IDX_DIM), row(SMALL_W)) + tuple(
        row(D) for _ in range(6))
    return pl.pallas_call(
        functools.partial(_in_proj_kernel, d_model=D),
        grid=(T // tm,),
        in_specs=[row(D), _resident((1, D)), _resident((D, n_all)),
                  _resident((1, A_Q_RANK)), _resident((1, A_KV_RANK)),
                  _resident((1, IDX_DIM)), _resident((1, IDX_DIM))],
        out_specs=out_specs,
        out_shape=out_shapes,
        compiler_params=pltpu.CompilerParams(
            dimension_semantics=("arbitrary",), vmem_limit_bytes=VMEM_LIMIT),
        name="in_proj",
    )(x2, g, w_all, cq_g, ckv_g, ln_g, ln_b)


def _topk_bias(score, row_pos, topk, bias_ref, tri_ones):
    nq, lk = score.shape
    col = lax.broadcasted_iota(jnp.int32, (nq, lk), 1)
    valid = col <= row_pos
    k_eff = jnp.minimum(row_pos + 1, topk).astype(F32)
    s = jnp.where(valid, score, -jnp.inf)
    lo = jnp.min(jnp.where(valid, score, jnp.inf), axis=1, keepdims=True)
    hi = jnp.max(s, axis=1, keepdims=True)

    def bisect(_, carry):
        lo, hi = carry
        mid = 0.5 * (lo + hi)
        cnt = jnp.sum(jnp.where(s >= mid, 1.0, 0.0), axis=1, keepdims=True)
        ge = cnt >= k_eff
        return jnp.where(ge, mid, lo), jnp.where(ge, hi, mid)

    lo, hi = lax.fori_loop(0, BISECT_STEPS, bisect, (lo, hi))

    def above(v):
        gt = s > v
        n_gt = jnp.sum(jnp.where(gt, 1.0, 0.0), axis=1, keepdims=True)
        nxt = jnp.min(jnp.where(gt, s, jnp.inf), axis=1, keepdims=True)
        return n_gt, nxt

    v0 = jnp.min(jnp.where(s >= lo, s, jnp.inf), axis=1, keepdims=True)
    n0, x0 = above(v0)

    def cond(c):
        _, n_gt, _ = c
        return jnp.max(jnp.where(n_gt >= k_eff, 1.0, 0.0)) > 0.0

    def body(c):
        v, n_gt, nxt = c
        v = jnp.where(n_gt >= k_eff, nxt, v)
        n_gt, nxt = above(v)
        return v, n_gt, nxt

    tau, n_gt, _ = lax.while_loop(cond, body, (v0, n0, x0))
    need = k_eff - n_gt

    carry = jnp.zeros((nq, LANES), F32)
    for c in range(lk // LANES):
        sc = s[:, c * LANES:(c + 1) * LANES]
        eq = sc == tau
        r = jnp.dot(jnp.where(eq, 1.0, 0.0).astype(BF16), tri_ones,
                    preferred_element_type=F32)
        prefix = r[:, :LANES] + carry
        carry = carry + r[:, LANES:]
        sel = (sc > tau) | (eq & (prefix <= need))
        bias_ref[:, c * LANES:(c + 1) * LANES] = jnp.where(sel, 0.0, NEG_BIG)


def _dsa_kernel(cq_ref, small_ref, ckv_ref, kidx_ref, wuq_ref, wuk_ref, wuv_ref, wiq_ref,
                o_ref, qall_ref, qidx_ref, bias_ref, *, seq, topk, lk_step):
    i = pl.program_id(1)
    nq = Q_BLOCK
    cq = cq_ref[0]
    q = jnp.dot(cq, wuq_ref[...], preferred_element_type=F32).astype(BF16)
    for h in range(A_HEADS):
        ql = jnp.dot(q[:, h * A_QK_DIM:(h + 1) * A_QK_DIM], wuk_ref[h],
                     preferred_element_type=F32) * (A_QK_DIM ** -0.5)
        qall_ref[h * nq:(h + 1) * nq, :] = ql.astype(BF16)
    qi = jnp.dot(cq, wiq_ref[...], preferred_element_type=F32).astype(BF16)
    for h in range(IDX_HEADS):
        qidx_ref[h * nq:(h + 1) * nq, :] = qi[:, h * IDX_DIM:(h + 1) * IDX_DIM]
    w_idx = small_ref[0][:, OFF_WIDX:OFF_WIDX + IDX_HEADS] * (IDX_HEADS ** -0.5 * IDX_DIM ** -0.5)
    row_pos = i * nq + lax.broadcasted_iota(jnp.int32, (nq, 1), 0)

    r_i = lax.broadcasted_iota(jnp.int32, (LANES, 2 * LANES), 0)
    c_i = lax.broadcasted_iota(jnp.int32, (LANES, 2 * LANES), 1)
    tri_ones = jnp.where((c_i >= LANES) | (r_i <= c_i), 1.0, 0.0).astype(BF16)

    nt = (((1,), (1,)), ((), ()))

    def block(lk):
        kidx = kidx_ref[0, :lk, :]
        logits = lax.dot_general(qidx_ref[...], kidx, nt, preferred_element_type=F32)
        score = jnp.zeros((nq, lk), F32)
        for h in range(IDX_HEADS):
            score = score + w_idx[:, h:h + 1] * jnp.maximum(logits[h * nq:(h + 1) * nq], 0.0)
        _topk_bias(score, row_pos, topk, bias_ref, tri_ones)
        bias = bias_ref[:, :lk]
        ckv = ckv_ref[0, :lk, :]
        for g in range(A_HEADS // HEAD_GROUP):
            rows = HEAD_GROUP * nq
            sg = lax.dot_general(qall_ref[g * rows:(g + 1) * rows, :], ckv, nt,
                                 preferred_element_type=F32)
            sg = sg.reshape(HEAD_GROUP, nq, lk) + bias[None]
            m = jnp.max(sg, axis=-1, keepdims=True)
            p = jnp.exp(sg - m)
            l = jnp.sum(p, axis=-1, keepdims=True)
            og = jnp.dot(p.astype(BF16).reshape(rows, lk), ckv, preferred_element_type=F32)
            og = (og.reshape(HEAD_GROUP, nq, A_KV_RANK) / l).astype(BF16)
            for hh in range(HEAD_GROUP):
                h = g * HEAD_GROUP + hh
                o_ref[0, :, h * A_V_DIM:(h + 1) * A_V_DIM] = jnp.dot(
                    og[hh], wuv_ref[h], preferred_element_type=F32).astype(o_ref.dtype)

    blocks_per_step = lk_step // nq
    n_var = seq // lk_step
    for v in range(n_var):
        pl.when(i // blocks_per_step == v)(functools.partial(block, (v + 1) * lk_step))


def _dsa(cq, small, ckv, kidx, w_uq, w_uk, w_uv, w_iq):
    B, L, _ = cq.shape
    topk = min(TOPK_MAX, L // 4)
    lk_step = min(512, L)
    nb = L // Q_BLOCK
    a_width = A_HEADS * A_V_DIM
    return pl.pallas_call(
        functools.partial(_dsa_kernel, seq=L, topk=topk, lk_step=lk_step),
        grid=(B, nb),
        in_specs=[
            pl.BlockSpec((1, Q_BLOCK, A_Q_RANK), lambda b, i: (b, i, 0)),
            pl.BlockSpec((1, Q_BLOCK, SMALL_W), lambda b, i: (b, i, 0)),
            pl.BlockSpec((1, L, A_KV_RANK), lambda b, i: (b, 0, 0)),
            pl.BlockSpec((1, L, IDX_DIM), lambda b, i: (b, 0, 0)),
            _resident(w_uq.shape), _resident(w_uk.shape), _resident(w_uv.shape),
            _resident(w_iq.shape),
        ],
        out_specs=pl.BlockSpec((1, Q_BLOCK, a_width), lambda b, i: (b, i, 0)),
        out_shape=jax.ShapeDtypeStruct((B, L, a_width), BF16),
        scratch_shapes=[
            pltpu.VMEM((A_HEADS * Q_BLOCK, A_KV_RANK), BF16),
            pltpu.VMEM((IDX_HEADS * Q_BLOCK, IDX_DIM), BF16),
            pltpu.VMEM((Q_BLOCK, L), F32),
        ],
        compiler_params=pltpu.CompilerParams(
            dimension_semantics=("arbitrary", "arbitrary"), vmem_limit_bytes=VMEM_LIMIT),
        name="dsa",
    )(cq, small, ckv, kidx, w_uq, w_uk, w_uv, w_iq)


def _causal_conv_silu(x, w):
    n = x.shape[0]
    t = lax.broadcasted_iota(jnp.int32, x.shape, 0)
    y = x * w[CONV_WIDTH - 1:CONV_WIDTH, :]
    for d in range(1, CONV_WIDTH):
        shifted = jnp.where(t >= d, pltpu.roll(x, d, 0), 0.0)
        y = y + shifted * w[CONV_WIDTH - 1 - d:CONV_WIDTH - d, :]
    return y * jax.nn.sigmoid(y)


def _l2n(x):
    return x * lax.rsqrt(jnp.sum(x * x, axis=-1, keepdims=True) + EPS)


def _gdn_kernel(alog_ref, dtb_ref, q_ref, k_ref, v_ref, z_ref, small_ref,
                cwq_ref, cwk_ref, cwv_ref, og_ref, o_ref,
                qn_ref, kn_ref, vn_ref, ub_ref, wd_ref, qk_ref, qd_ref, ktT_ref, at_ref,
                s_ref, *, seq):
    h = pl.program_id(1)
    C = CHUNK
    n_chunks = seq // C
    hi = lax.Precision.HIGHEST

    qn_ref[...] = _l2n(_causal_conv_silu(q_ref[0], cwq_ref[...]))
    kn_ref[...] = _l2n(_causal_conv_silu(k_ref[0], cwk_ref[...]))
    vn_ref[...] = _causal_conv_silu(v_ref[0], cwv_ref[...])

    lane_r = lax.broadcasted_iota(jnp.int32, (LANES, LANES), 0)
    sel_beta = jnp.where(lane_r == OFF_BETA + h, 1.0, 0.0)
    sel_a = jnp.where(lane_r == OFF_A + h, 1.0, 0.0)
    neg_a = -jnp.exp(jnp.full((1, LANES), alog_ref[h], F32))
    dt_b = jnp.full((1, LANES), dtb_ref[h], F32)
    r64 = lax.broadcasted_iota(jnp.int32, (C, C), 0)
    c64 = lax.broadcasted_iota(jnp.int32, (C, C), 1)
    tri = r64 >= c64
    strict = r64 > c64
    tri_f = jnp.where(tri, 1.0, 0.0)
    eye = jnp.where(r64 == c64, 1.0, 0.0)
    nt = (((1,), (1,)), ((), ()))

    def prep(n, _):
        rows = pl.ds(pl.multiple_of(n * C, C), C)
        sm = small_ref[0, rows, :]
        beta = jax.nn.sigmoid(jnp.dot(sm, sel_beta, precision=hi, preferred_element_type=F32))
        a_raw = jnp.dot(sm, sel_a, precision=hi, preferred_element_type=F32)
        g = neg_a * jax.nn.softplus(a_raw + dt_b)
        G = jnp.dot(tri_f, g, precision=hi, preferred_element_type=F32)
        Gr = G.T[:C, :C]
        Gc = G[:, :C]
        decay = jnp.exp(jnp.where(tri, Gc - Gr, -jnp.inf))
        eG = jnp.exp(G)
        G_last = G[C - 1:C, :]
        q = qn_ref[rows, :] * (B_K_DIM ** -0.5)
        k = kn_ref[rows, :]
        v = vn_ref[rows, :]
        kb = k.astype(BF16)
        kk = lax.dot_general(kb, kb, nt, preferred_element_type=F32)
        N = jnp.where(strict, beta[:, :C] * kk * decay, 0.0)
        X = eye - N
        P = jnp.dot(N, N, precision=hi, preferred_element_type=F32)
        steps = C.bit_length() - 2
        for it in range(steps):
            X = X + jnp.dot(X, P, precision=hi, preferred_element_type=F32)
            if it + 1 < steps:
                P = jnp.dot(P, P, precision=hi, preferred_element_type=F32)
        rhs = jnp.concatenate([v * beta, k * (beta * eG)], axis=-1).astype(BF16)
        sol = jnp.dot(X.astype(BF16), rhs, preferred_element_type=F32)
        ub_ref[n] = sol[:, :B_V_DIM]
        wd_ref[n] = sol[:, B_V_DIM:].astype(BF16)
        qk = lax.dot_general(q.astype(BF16), kb, nt, preferred_element_type=F32) * decay
        qk_ref[n] = qk.astype(BF16)
        qd_ref[n] = (q * eG).astype(BF16)
        ktT_ref[n] = (k * jnp.exp(G_last - G)).T.astype(BF16)
        at_ref[n] = jnp.exp(G_last)
        return 0

    lax.fori_loop(0, n_chunks, prep, 0)

    s_ref[...] = jnp.zeros_like(s_ref)
    og = og_ref[...]

    def scan(n, _):
        rows = pl.ds(pl.multiple_of(n * C, C), C)
        S = s_ref[...]
        Sb = S.astype(BF16)
        u = ub_ref[n] - jnp.dot(wd_ref[n], Sb, preferred_element_type=F32)
        o = (jnp.dot(qd_ref[n], Sb, preferred_element_type=F32)
             + jnp.dot(qk_ref[n], u.astype(BF16), preferred_element_type=F32))
        s_ref[...] = S * at_ref[n] + jnp.dot(ktT_ref[n], u.astype(BF16),
                                             preferred_element_type=F32)
        z = z_ref[0, rows, :]
        o_ref[0, rows, :] = (_rms(o, og) * (z * jax.nn.sigmoid(z))).astype(o_ref.dtype)
        return 0

    lax.fori_loop(0, n_chunks, scan, 0)


def _gdn(qb, kb, vb, zb, small, conv_w, a_log, dt_bias, onorm_g):
    B, L, W = qb.shape
    n_chunks = L // CHUNK
    heads = W // B_K_DIM
    col = pl.BlockSpec((1, L, B_K_DIM), lambda b, h: (b, 0, h))
    cw = lambda part: pl.BlockSpec((CONV_WIDTH, B_K_DIM), lambda b, h: (0, part * heads + h))
    smem = pl.BlockSpec(memory_space=pltpu.SMEM)
    return pl.pallas_call(
        functools.partial(_gdn_kernel, seq=L),
        grid=(B, heads),
        in_specs=[smem, smem, col, col, col, col,
                  pl.BlockSpec((1, L, SMALL_W), lambda b, h: (b, 0, 0)),
                  cw(0), cw(1), cw(2), _resident((1, B_V_DIM))],
        out_specs=col,
        out_shape=jax.ShapeDtypeStruct((B, L, W), BF16),
        scratch_shapes=[
            pltpu.VMEM((L, B_K_DIM), F32), pltpu.VMEM((L, B_K_DIM), F32),
            pltpu.VMEM((L, B_V_DIM), F32),
            pltpu.VMEM((n_chunks, CHUNK, B_V_DIM), F32),
            pltpu.VMEM((n_chunks, CHUNK, B_K_DIM), BF16),
            pltpu.VMEM((n_chunks, CHUNK, CHUNK), BF16),
            pltpu.VMEM((n_chunks, CHUNK, B_K_DIM), BF16),
            pltpu.VMEM((n_chunks, B_K_DIM, CHUNK), BF16),
            pltpu.VMEM((n_chunks, 1, LANES), F32),
            pltpu.VMEM((B_K_DIM, B_V_DIM), F32),
        ],
        compiler_params=pltpu.CompilerParams(
            dimension_semantics=("arbitrary", "arbitrary"), vmem_limit_bytes=VMEM_LIMIT),
        name="gdn",
    )(a_log, dt_bias, qb, kb, vb, zb, small, conv_w, conv_w, conv_w, onorm_g)


def _merge_kernel(x_ref, oa_ref, ob_ref, ga_ref, gb_ref, wa_ref, wb_ref, wo_ref,
                  fg_ref, wg_ref, wu_ref, wd_ref, og_ref, o_ref, *, ff_chunk, final_norm):
    ya = jnp.dot(oa_ref[...], wa_ref[...], preferred_element_type=F32)
    yb = jnp.dot(ob_ref[...], wb_ref[...], preferred_element_type=F32)
    merged = jax.nn.sigmoid(ga_ref[...]) * ya + jax.nn.sigmoid(gb_ref[...]) * yb
    x1 = x_ref[...] + jnp.dot(merged.astype(BF16), wo_ref[...], preferred_element_type=F32)
    h = _rms(x1, fg_ref[...]).astype(BF16)
    d_ff = wg_ref.shape[1]
    acc = x1
    for c0 in range(0, d_ff, ff_chunk):
        gate = jnp.dot(h, wg_ref[:, c0:c0 + ff_chunk], preferred_element_type=F32)
        up = jnp.dot(h, wu_ref[:, c0:c0 + ff_chunk], preferred_element_type=F32)
        act = (gate * jax.nn.sigmoid(gate) * up).astype(BF16)
        acc = acc + jnp.dot(act, wd_ref[c0:c0 + ff_chunk, :], preferred_element_type=F32)
    o_ref[...] = _rms(acc, og_ref[...]) if final_norm else acc


def _merge(x2, oa, ob, ga, gb, wa, wb, wo, fg, wg, wu, wd, og, *, tm, final_norm):
    T, D = x2.shape
    d_ff = wg.shape[1]
    ff_chunk = d_ff // 2 if (d_ff // 2) % LANES == 0 else d_ff
    row = pl.BlockSpec((tm, D), lambda i: (i, 0))
    return pl.pallas_call(
        functools.partial(_merge_kernel, ff_chunk=ff_chunk, final_norm=final_norm),
        grid=(T // tm,),
        in_specs=[row, row, row, row, row,
                  _resident(wa.shape), _resident(wb.shape), _resident(wo.shape),
                  _resident(fg.shape), _resident(wg.shape), _resident(wu.shape),
                  _resident(wd.shape), _resident(og.shape)],
        out_specs=row,
        out_shape=jax.ShapeDtypeStruct((T, D), F32),
        compiler_params=pltpu.CompilerParams(
            dimension_semantics=("arbitrary",), vmem_limit_bytes=VMEM_LIMIT),
        name="merge_ffn",
    )(x2, oa, ob, ga, gb, wa, wb, wo, fg, wg, wu, wd, og)


def _reorder_w_in(w):
    sizes = (A_Q_RANK, A_KV_RANK, IDX_DIM, IDX_HEADS,
             B_HEADS * B_K_DIM, B_HEADS * B_K_DIM, B_HEADS * B_V_DIM, B_HEADS, B_HEADS,
             B_HEADS * B_V_DIM, w.shape[0], w.shape[0])
    parts, c0 = [], 0
    for s in sizes:
        parts.append(w[:, c0:c0 + s])
        c0 += s
    (c_q, c_kv, k_idx, w_idx, q_b, k_b, v_b, beta_b, a_b, z_b, gate_a, gate_b) = parts
    pad = jnp.zeros((w.shape[0], SMALL_W - IDX_DIM - IDX_HEADS - 2 * B_HEADS), w.dtype)
    return jnp.concatenate(
        [q_b, k_b, v_b, z_b, gate_a, gate_b, c_q, c_kv, k_idx, w_idx, beta_b, a_b, pad],
        axis=1).astype(BF16)


def kernel(x, mix_norm_g, w_in, cq_norm_g, ckv_norm_g, w_uq, w_uk, w_uv, w_iq, kidx_ln_g, kidx_ln_b, w_branch_a, conv_w, a_log, dt_bias, onorm_g, w_branch_b, w_out, ffn_norm_g, w_gate, w_up, w_down, final_norm_g):
    B, L, D = x.shape
    depth = w_in.shape[0]
    T = B * L
    tm = min(256, T)
    x2 = x.reshape(T, D)
    vec = lambda a: a.reshape(1, -1).astype(F32)
    for l in range(depth):
        (cq, ckv, kidx, small, qb, kb, vb, zb, ga, gb) = _in_proj(
            x2, vec(mix_norm_g[l]), _reorder_w_in(w_in[l]), vec(cq_norm_g[l]),
            vec(ckv_norm_g[l]), vec(kidx_ln_g[l]), vec(kidx_ln_b[l]), tm=tm)
        seq = lambda a: a.reshape(B, L, a.shape[-1])
        o_a = _dsa(seq(cq), seq(small), seq(ckv), seq(kidx), w_uq[l].astype(BF16),
                   w_uk[l].astype(BF16), w_uv[l].astype(BF16), w_iq[l].astype(BF16))
        o_b = _gdn(seq(qb), seq(kb), seq(vb), seq(zb), seq(small), conv_w[l],
                   a_log[l], dt_bias[l], vec(onorm_g[l]))
        x2 = _merge(x2, o_a.reshape(T, -1), o_b.reshape(T, -1), ga, gb,
                    w_branch_a[l].astype(BF16), w_branch_b[l].astype(BF16),
                    w_out[l].astype(BF16), vec(ffn_norm_g[l]), w_gate[l].astype(BF16),
                    w_up[l].astype(BF16), w_down[l].astype(BF16), vec(final_norm_g),
                    tm=tm, final_norm=(l == depth - 1))
    return x2.reshape(B, L, D)
```

```python
import functools

import jax
import jax.numpy as jnp
from jax import lax
from jax.experimental import pallas as pl
from jax.experimental.pallas import tpu as pltpu

F32 = jnp.float32
BF16 = jnp.bfloat16

EPS = 1e-6
A_HEADS = 16
A_QK_DIM = 64
A_V_DIM = 64
A_Q_RANK = 256
A_KV_RANK = 256
IDX_HEADS = 8
IDX_DIM = 64
TOPK_MAX = 256
Q_BLOCK = 128
B_HEADS = 8
B_K_DIM = 128
B_V_DIM = 128
CONV_WIDTH = 4
CHUNK = 64

GDN_W = B_HEADS * B_K_DIM
assert B_K_DIM == B_V_DIM
LANES = 128
SUBLANES = 8
SMALL_W = LANES
OFF_WIDX = IDX_DIM
OFF_BETA = IDX_DIM + IDX_HEADS
OFF_A = IDX_DIM + IDX_HEADS + B_HEADS
VMEM_LIMIT = 56 * 1024 * 1024
NEG_BIG = -1e30
BISECT_STEPS = 16
HEAD_GROUP = 4
GDN_HEADS_PER_STEP = 2
PREP_GROUP = 4


def _resident(shape):
    nd = len(shape)
    return pl.BlockSpec(shape, lambda *_: (0,) * nd, pipeline_mode=pl.Buffered(1))


def _rms(x, g):
    return x * lax.rsqrt(jnp.mean(x * x, axis=-1, keepdims=True) + EPS) * g


def _l2n(x):
    return x * lax.rsqrt(jnp.sum(x * x, axis=-1, keepdims=True) + EPS)


def _chunk_cumsum(g, chunk):
    pos = lax.broadcasted_iota(jnp.int32, g.shape, 0) % chunk
    d = 1
    while d < chunk:
        g = g + jnp.where(pos >= d, pltpu.roll(g, d, 0), 0.0)
        d *= 2
    return g


def _conv_silu_tile(x, tail_ref, w):
    full = jnp.concatenate([tail_ref[...], x], axis=0)
    y = x * w[CONV_WIDTH - 1:CONV_WIDTH, :]
    for d in range(1, CONV_WIDTH):
        y = y + pltpu.roll(full, d, 0)[SUBLANES:] * w[CONV_WIDTH - 1 - d:CONV_WIDTH - d, :]
    tail_ref[...] = x[x.shape[0] - SUBLANES:]
    return y * jax.nn.sigmoid(y)


def _in_proj_kernel(x_ref, g_ref, w_ref, cqg_ref, ckvg_ref, lng_ref, lnb_ref, cw_ref,
                    alog_ref, dtb_ref,
                    cq_ref, ckv_ref, kidx_ref, small_ref, gates_ref, q_ref, k_ref, v_ref,
                    z_ref, ga_ref, gb_ref, tail_ref, *, d_model, tiles_per_seq):
    @pl.when(pl.program_id(0) % tiles_per_seq == 0)
    def _():
        tail_ref[...] = jnp.zeros_like(tail_ref)

    h = _rms(x_ref[...], g_ref[...]).astype(BF16)

    def proj(c0, width):
        return jnp.dot(h, w_ref[:, c0:c0 + width], preferred_element_type=F32)

    pair = 2 * B_K_DIM
    for part, ref in enumerate((q_ref, k_ref, v_ref)):
        for c in range(0, GDN_W, pair):
            cc = part * GDN_W + c
            y = _conv_silu_tile(proj(cc, pair), tail_ref.at[:, cc:cc + pair],
                                cw_ref[:, cc:cc + pair])
            if part < 2:
                scale = B_K_DIM ** -0.5 if part == 0 else 1.0
                y = jnp.concatenate([_l2n(y[:, :B_K_DIM]), _l2n(y[:, B_K_DIM:])], axis=1) * scale
            ref[:, c:c + pair] = y.astype(ref.dtype)
    c0 = 3 * GDN_W
    for ref, width in ((z_ref, GDN_W), (ga_ref, d_model), (gb_ref, d_model)):
        ref[...] = proj(c0, width)
        c0 += width
    cq_ref[...] = _rms(proj(c0, A_Q_RANK), cqg_ref[...]).astype(BF16)
    c0 += A_Q_RANK
    ckv_ref[...] = _rms(proj(c0, A_KV_RANK), ckvg_ref[...]).astype(BF16)
    c0 += A_KV_RANK
    small = proj(c0, SMALL_W)
    small_ref[...] = small
    kraw = small[:, :IDX_DIM]
    mu = jnp.mean(kraw, axis=-1, keepdims=True)
    kc = kraw - mu
    kn = kc * lax.rsqrt(jnp.mean(kc * kc, axis=-1, keepdims=True) + EPS)
    kidx_ref[...] = (kn * lng_ref[...] + lnb_ref[...]).astype(BF16)
    g = -jnp.exp(alog_ref[...]) * jax.nn.softplus(small + dtb_ref[...])
    lane = lax.broadcasted_iota(jnp.int32, small.shape, 1)
    gates_ref[...] = jnp.where(lane >= OFF_A, _chunk_cumsum(g, CHUNK), jax.nn.sigmoid(small))


def _in_proj(x2, g, w_all, cq_g, ckv_g, ln_g, ln_b, conv_w, alog_v, dtb_v, *, tm, seq):
    T, D = x2.shape
    n_all = w_all.shape[1]
    assert tm % CHUNK == 0 and seq % tm == 0
    row = lambda w: pl.BlockSpec((tm, w), lambda i: (i, 0))
    wide = ((GDN_W, BF16), (GDN_W, BF16), (GDN_W, BF16), (GDN_W, F32),
            (D, F32), (D, F32))
    out_shapes = (
        jax.ShapeDtypeStruct((T, A_Q_RANK), BF16),
        jax.ShapeDtypeStruct((T, A_KV_RANK), BF16),
        jax.ShapeDtypeStruct((T, IDX_DIM), BF16),
        jax.ShapeDtypeStruct((T, SMALL_W), F32),
        jax.ShapeDtypeStruct((T, SMALL_W), F32),
    ) + tuple(jax.ShapeDtypeStruct((T, w), dt) for w, dt in wide)
    out_specs = (row(A_Q_RANK), row(A_KV_RANK), row(IDX_DIM), row(SMALL_W), row(SMALL_W)
                 ) + tuple(row(w) for w, _ in wide)
    return pl.pallas_call(
        functools.partial(_in_proj_kernel, d_model=D, tiles_per_seq=seq // tm),
        grid=(T // tm,),
        in_specs=[row(D), _resident((1, D)), _resident((D, n_all)),
                  _resident((1, A_Q_RANK)), _resident((1, A_KV_RANK)),
                  _resident((1, IDX_DIM)), _resident((1, IDX_DIM)),
                  _resident(conv_w.shape), _resident((1, SMALL_W)), _resident((1, SMALL_W))],
        out_specs=out_specs,
        out_shape=out_shapes,
        scratch_shapes=[pltpu.VMEM((SUBLANES, conv_w.shape[1]), F32)],
        compiler_params=pltpu.CompilerParams(
            dimension_semantics=("arbitrary",), vmem_limit_bytes=VMEM_LIMIT),
        name="in_proj",
    )(x2, g, w_all, cq_g, ckv_g, ln_g, ln_b, conv_w, alog_v, dtb_v)


def _topk_bias(score, row_pos, topk, bias_ref, tri_ones):
    nq, lk = score.shape
    col = lax.broadcasted_iota(jnp.int32, (nq, lk), 1)
    valid = col <= row_pos
    k_eff = jnp.minimum(row_pos + 1, topk).astype(F32)
    s = jnp.where(valid, score, -jnp.inf)
    lo = jnp.min(jnp.where(valid, score, jnp.inf), axis=1, keepdims=True)
    hi = jnp.max(s, axis=1, keepdims=True)

    def bisect(_, carry):
        lo, hi = carry
        mid = 0.5 * (lo + hi)
        cnt = jnp.sum(jnp.where(s >= mid, 1.0, 0.0), axis=1, keepdims=True)
        ge = cnt >= k_eff
        return jnp.where(ge, mid, lo), jnp.where(ge, hi, mid)

    lo, hi = lax.fori_loop(0, BISECT_STEPS, bisect, (lo, hi))

    def above(v):
        gt = s > v
        n_gt = jnp.sum(jnp.where(gt, 1.0, 0.0), axis=1, keepdims=True)
        nxt = jnp.min(jnp.where(gt, s, jnp.inf), axis=1, keepdims=True)
        return n_gt, nxt

    v0 = jnp.min(jnp.where(s >= lo, s, jnp.inf), axis=1, keepdims=True)
    n0, x0 = above(v0)

    def cond(c):
        _, n_gt, _ = c
        return jnp.max(jnp.where(n_gt >= k_eff, 1.0, 0.0)) > 0.0

    def body(c):
        v, n_gt, nxt = c
        v = jnp.where(n_gt >= k_eff, nxt, v)
        n_gt, nxt = above(v)
        return v, n_gt, nxt

    tau, n_gt, _ = lax.while_loop(cond, body, (v0, n0, x0))
    need = k_eff - n_gt

    carry = jnp.zeros((nq, LANES), F32)
    for c in range(lk // LANES):
        sc = s[:, c * LANES:(c + 1) * LANES]
        eq = sc == tau
        r = jnp.dot(jnp.where(eq, 1.0, 0.0).astype(BF16), tri_ones,
                    preferred_element_type=F32)
        prefix = r[:, :LANES] + carry
        carry = carry + r[:, LANES:]
        sel = (sc > tau) | (eq & (prefix <= need))
        bias_ref[:, c * LANES:(c + 1) * LANES] = jnp.where(sel, 0.0, NEG_BIG)


def _dsa_kernel(cq_ref, small_ref, ckv_ref, kidx_ref, wuq_ref, wuk_ref, wuv_ref, wiq_ref,
                o_ref, qall_ref, qidx_ref, bias_ref, *, seq, topk, lk_step):
    i = pl.program_id(1)
    nq = Q_BLOCK
    cq = cq_ref[0]
    q = jnp.dot(cq, wuq_ref[...], preferred_element_type=F32).astype(BF16)
    for h in range(A_HEADS):
        ql = jnp.dot(q[:, h * A_QK_DIM:(h + 1) * A_QK_DIM], wuk_ref[h],
                     preferred_element_type=F32) * (A_QK_DIM ** -0.5)
        qall_ref[h * nq:(h + 1) * nq, :] = ql.astype(BF16)
    qi = jnp.dot(cq, wiq_ref[...], preferred_element_type=F32).astype(BF16)
    for h in range(IDX_HEADS):
        qidx_ref[h * nq:(h + 1) * nq, :] = qi[:, h * IDX_DIM:(h + 1) * IDX_DIM]
    w_idx = small_ref[0][:, OFF_WIDX:OFF_WIDX + IDX_HEADS] * (IDX_HEADS ** -0.5 * IDX_DIM ** -0.5)
    row_pos = i * nq + lax.broadcasted_iota(jnp.int32, (nq, 1), 0)

    r_i = lax.broadcasted_iota(jnp.int32, (LANES, 2 * LANES), 0)
    c_i = lax.broadcasted_iota(jnp.int32, (LANES, 2 * LANES), 1)
    tri_ones = jnp.where((c_i >= LANES) | (r_i <= c_i), 1.0, 0.0).astype(BF16)

    nt = (((1,), (1,)), ((), ()))

    def block(lk):
        kidx = kidx_ref[0, :lk, :]
        logits = lax.dot_general(qidx_ref[...], kidx, nt, preferred_element_type=F32)
        score = jnp.zeros((nq, lk), F32)
        for h in range(IDX_HEADS):
            score = score + w_idx[:, h:h + 1] * jnp.maximum(logits[h * nq:(h + 1) * nq], 0.0)
        _topk_bias(score, row_pos, topk, bias_ref, tri_ones)
        bias = bias_ref[:, :lk]
        ckv = ckv_ref[0, :lk, :]
        for g in range(A_HEADS // HEAD_GROUP):
            rows = HEAD_GROUP * nq
            sg = lax.dot_general(qall_ref[g * rows:(g + 1) * rows, :], ckv, nt,
                                 preferred_element_type=F32)
            sg = sg.reshape(HEAD_GROUP, nq, lk) + bias[None]
            m = jnp.max(sg, axis=-1, keepdims=True)
            p = jnp.exp(sg - m)
            l = jnp.sum(p, axis=-1, keepdims=True)
            og = jnp.dot(p.astype(BF16).reshape(rows, lk), ckv, preferred_element_type=F32)
            og = (og.reshape(HEAD_GROUP, nq, A_KV_RANK) / l).astype(BF16)
            for hh in range(HEAD_GROUP):
                h = g * HEAD_GROUP + hh
                o_ref[0, :, h * A_V_DIM:(h + 1) * A_V_DIM] = jnp.dot(
                    og[hh], wuv_ref[h], preferred_element_type=F32).astype(o_ref.dtype)

    blocks_per_step = lk_step // nq
    n_var = seq // lk_step
    for v in range(n_var):
        pl.when(i // blocks_per_step == v)(functools.partial(block, (v + 1) * lk_step))


def _dsa(cq, small, ckv, kidx, w_uq, w_uk, w_uv, w_iq):
    B, L, _ = cq.shape
    topk = min(TOPK_MAX, L // 4)
    lk_step = min(512, L)
    nb = L // Q_BLOCK
    a_width = A_HEADS * A_V_DIM
    return pl.pallas_call(
        functools.partial(_dsa_kernel, seq=L, topk=topk, lk_step=lk_step),
        grid=(B, nb),
        in_specs=[
            pl.BlockSpec((1, Q_BLOCK, A_Q_RANK), lambda b, i: (b, i, 0)),
            pl.BlockSpec((1, Q_BLOCK, SMALL_W), lambda b, i: (b, i, 0)),
            pl.BlockSpec((1, L, A_KV_RANK), lambda b, i: (b, 0, 0)),
            pl.BlockSpec((1, L, IDX_DIM), lambda b, i: (b, 0, 0)),
            _resident(w_uq.shape), _resident(w_uk.shape), _resident(w_uv.shape),
            _resident(w_iq.shape),
        ],
        out_specs=pl.BlockSpec((1, Q_BLOCK, a_width), lambda b, i: (b, i, 0)),
        out_shape=jax.ShapeDtypeStruct((B, L, a_width), BF16),
        scratch_shapes=[
            pltpu.VMEM((A_HEADS * Q_BLOCK, A_KV_RANK), BF16),
            pltpu.VMEM((IDX_HEADS * Q_BLOCK, IDX_DIM), BF16),
            pltpu.VMEM((Q_BLOCK, L), F32),
        ],
        compiler_params=pltpu.CompilerParams(
            dimension_semantics=("arbitrary", "arbitrary"), vmem_limit_bytes=VMEM_LIMIT),
        name="dsa",
    )(cq, small, ckv, kidx, w_uq, w_uk, w_uv, w_iq)


def _lane_bcast(x, c):
    lane = lax.broadcasted_iota(jnp.int32, x.shape, 1)
    col = jnp.sum(jnp.where(lane == c, x, 0.0), axis=1, keepdims=True)
    return jnp.broadcast_to(col, x.shape)


def _gdn_kernel(q_ref, k_ref, v_ref, z_ref, gates_ref, og_ref, o_ref,
                mneg_ref, r_ref, qeff_ref, o0_ref, at_ref, oraw_ref, *, seq, hp):
    hg = pl.program_id(1)
    C = CHUNK
    n_chunks = seq // C
    dk = B_K_DIM

    r64 = lax.broadcasted_iota(jnp.int32, (C, C), 0)
    c64 = lax.broadcasted_iota(jnp.int32, (C, C), 1)
    tri = r64 >= c64
    strict = r64 > c64
    eye = jnp.where(r64 == c64, 1.0, 0.0)
    nt = (((1,), (1,)), ((), ()))
    dot = functools.partial(jnp.dot, preferred_element_type=F32)

    def prep(t, _):
        ids = [(hh, t * PREP_GROUP + j) for j in range(PREP_GROUP) for hh in range(hp)]
        each = lambda f, *cols: [f(*a) for a in zip(*cols)]
        rows = [pl.ds(pl.multiple_of(n * C, C), C) for _, n in ids]
        load = lambda ref: [ref[0, r, hh * dk:(hh + 1) * dk] for (hh, _), r in zip(ids, rows)]
        qb, kb, vb = (load(r) for r in (q_ref, k_ref, v_ref))
        q, k, v = (each(lambda a: a.astype(F32), x) for x in (qb, kb, vb))
        gt = [gates_ref[0, r, :] for r in rows]
        beta = [_lane_bcast(a, OFF_BETA + hg * hp + hh) for a, (hh, _) in zip(gt, ids)]
        G = [_lane_bcast(a, OFF_A + hg * hp + hh) for a, (hh, _) in zip(gt, ids)]
        decay = each(lambda g: jnp.exp(jnp.where(tri, g[:, :C] - g.T[:C, :C], -jnp.inf)), G)
        eG = each(jnp.exp, G)
        kk = each(lambda a: lax.dot_general(a, a, nt, preferred_element_type=F32), kb)
        N = each(lambda b, a, d: jnp.where(strict, b[:, :C] * a * d, 0.0), beta, kk, decay)
        X = each(lambda a: eye - a, N)
        Nb = each(lambda a: a.astype(BF16), N)
        Pb = each(lambda a: dot(a, a).astype(BF16), Nb)
        steps = C.bit_length() - 2
        for it in range(steps):
            X = each(lambda x, p: x + dot(x.astype(BF16), p), X, Pb)
            if it + 1 < steps:
                Pb = each(lambda p: dot(p, p).astype(BF16), Pb)
        rhs = each(lambda v_, k_, b, e: jnp.concatenate(
            [v_ * b, k_ * (b * e)], axis=-1).astype(BF16), v, k, beta, eG)
        sol = each(lambda x, r: dot(x.astype(BF16), r).astype(BF16), X, rhs)
        ub = [s[:, :B_V_DIM] for s in sol]
        wd = [s[:, B_V_DIM:] for s in sol]
        qk = each(lambda q_, k_, d: (lax.dot_general(
            q_, k_, nt, preferred_element_type=F32) * d).astype(BF16), qb, kb, decay)
        ktT = each(lambda k_, g: (k_ * jnp.exp(g[C - 1:C, :] - g)).T.astype(BF16), k, G)
        mneg = each(lambda a, b: dot(a, b).astype(BF16), ktT, wd)
        rr = each(dot, ktT, ub)
        qeff = each(lambda q_, e, a, b: (q_ * e - dot(a, b)).astype(BF16), q, eG, qk, wd)
        o0 = each(dot, qk, ub)
        for i, (hh, n) in enumerate(ids):
            mneg_ref[hh, n] = mneg[i]
            r_ref[hh, n] = rr[i]
            qeff_ref[hh, n] = qeff[i]
            o0_ref[hh, n] = o0[i]
            at_ref[hh, n] = jnp.exp(G[i][C - 1:C, :])
        return 0

    lax.fori_loop(0, n_chunks // PREP_GROUP, prep, 0)

    def scan(n, S):
        rows = pl.ds(pl.multiple_of(n * C, C), C)
        Sb = [s.astype(BF16) for s in S]
        ms = [dot(mneg_ref[hh, n], Sb[hh]) for hh in range(hp)]
        os_ = [dot(qeff_ref[hh, n], Sb[hh]) for hh in range(hp)]
        new = []
        for hh in range(hp):
            new.append(S[hh] * at_ref[hh, n] - ms[hh] + r_ref[hh, n])
            oraw_ref[hh, rows, :] = os_[hh] + o0_ref[hh, n]
        return tuple(new)

    lax.fori_loop(0, n_chunks, scan, tuple(jnp.zeros((dk, B_V_DIM), F32) for _ in range(hp)))

    for hh in range(hp):
        ls = slice(hh * dk, (hh + 1) * dk)
        z = z_ref[0, :, ls]
        o_ref[0, :, ls] = (_rms(oraw_ref[hh], og_ref[...])
                           * (z * jax.nn.sigmoid(z))).astype(o_ref.dtype)


def _gdn(qn, kn, vn, zb, gates, onorm_g):
    B, L, W = qn.shape
    hp = GDN_HEADS_PER_STEP
    n_chunks = L // CHUNK
    heads = W // B_K_DIM
    assert heads % hp == 0 and n_chunks % PREP_GROUP == 0
    col = pl.BlockSpec((1, L, hp * B_K_DIM), lambda b, h: (b, 0, h))
    return pl.pallas_call(
        functools.partial(_gdn_kernel, seq=L, hp=hp),
        grid=(B, heads // hp),
        in_specs=[col, col, col, col,
                  pl.BlockSpec((1, L, SMALL_W), lambda b, h: (b, 0, 0)),
                  _resident((1, B_V_DIM))],
        out_specs=col,
        out_shape=jax.ShapeDtypeStruct((B, L, W), BF16),
        scratch_shapes=[
            pltpu.VMEM((hp, n_chunks, B_K_DIM, B_K_DIM), BF16),
            pltpu.VMEM((hp, n_chunks, B_K_DIM, B_V_DIM), F32),
            pltpu.VMEM((hp, n_chunks, CHUNK, B_K_DIM), BF16),
            pltpu.VMEM((hp, n_chunks, CHUNK, B_V_DIM), F32),
            pltpu.VMEM((hp, n_chunks, 1, LANES), F32),
            pltpu.VMEM((hp, L, B_V_DIM), F32),
        ],
        compiler_params=pltpu.CompilerParams(
            dimension_semantics=("arbitrary", "arbitrary"), vmem_limit_bytes=VMEM_LIMIT),
        name="gdn",
    )(qn, kn, vn, zb, gates, onorm_g)


def _merge_kernel(x_ref, oa_ref, ob_ref, ga_ref, gb_ref, wa_ref, wb_ref, wo_ref,
                  fg_ref, wg_ref, wu_ref, wd_ref, og_ref, o_ref, *, ff_chunk, final_norm):
    ya = jnp.dot(oa_ref[...], wa_ref[...], preferred_element_type=F32)
    yb = jnp.dot(ob_ref[...], wb_ref[...], preferred_element_type=F32)
    merged = jax.nn.sigmoid(ga_ref[...]) * ya + jax.nn.sigmoid(gb_ref[...]) * yb
    x1 = x_ref[...] + jnp.dot(merged.astype(BF16), wo_ref[...], preferred_element_type=F32)
    h = _rms(x1, fg_ref[...]).astype(BF16)
    d_ff = wg_ref.shape[1]
    acc = x1
    for c0 in range(0, d_ff, ff_chunk):
        gate = jnp.dot(h, wg_ref[:, c0:c0 + ff_chunk], preferred_element_type=F32)
        up = jnp.dot(h, wu_ref[:, c0:c0 + ff_chunk], preferred_element_type=F32)
        act = (gate * jax.nn.sigmoid(gate) * up).astype(BF16)
        acc = acc + jnp.dot(act, wd_ref[c0:c0 + ff_chunk, :], preferred_element_type=F32)
    o_ref[...] = _rms(acc, og_ref[...]) if final_norm else acc


def _merge(x2, oa, ob, ga, gb, wa, wb, wo, fg, wg, wu, wd, og, *, tm, final_norm):
    T, D = x2.shape
    d_ff = wg.shape[1]
    ff_chunk = d_ff // 2 if (d_ff // 2) % LANES == 0 else d_ff
    row = pl.BlockSpec((tm, D), lambda i: (i, 0))
    return pl.pallas_call(
        functools.partial(_merge_kernel, ff_chunk=ff_chunk, final_norm=final_norm),
        grid=(T // tm,),
        in_specs=[row, row, row, row, row,
                  _resident(wa.shape), _resident(wb.shape), _resident(wo.shape),
                  _resident(fg.shape), _resident(wg.shape), _resident(wu.shape),
                  _resident(wd.shape), _resident(og.shape)],
        out_specs=row,
        out_shape=jax.ShapeDtypeStruct((T, D), F32),
        compiler_params=pltpu.CompilerParams(
            dimension_semantics=("arbitrary",), vmem_limit_bytes=VMEM_LIMIT),
        name="merge_ffn",
    )(x2, oa, ob, ga, gb, wa, wb, wo, fg, wg, wu, wd, og)


def _reorder_w_in(w):
    sizes = (A_Q_RANK, A_KV_RANK, IDX_DIM, IDX_HEADS,
             B_HEADS * B_K_DIM, B_HEADS * B_K_DIM, B_HEADS * B_V_DIM, B_HEADS, B_HEADS,
             B_HEADS * B_V_DIM, w.shape[0], w.shape[0])
    parts, c0 = [], 0
    for s in sizes:
        parts.append(w[:, c0:c0 + s])
        c0 += s
    (c_q, c_kv, k_idx, w_idx, q_b, k_b, v_b, beta_b, a_b, z_b, gate_a, gate_b) = parts
    pad = jnp.zeros((w.shape[0], SMALL_W - IDX_DIM - IDX_HEADS - 2 * B_HEADS), w.dtype)
    return jnp.concatenate(
        [q_b, k_b, v_b, z_b, gate_a, gate_b, c_q, c_kv, k_idx, w_idx, beta_b, a_b, pad],
        axis=1).astype(BF16)


def kernel(x, mix_norm_g, w_in, cq_norm_g, ckv_norm_g, w_uq, w_uk, w_uv, w_iq, kidx_ln_g, kidx_ln_b, w_branch_a, conv_w, a_log, dt_bias, onorm_g, w_branch_b, w_out, ffn_norm_g, w_gate, w_up, w_down, final_norm_g):
    B, L, D = x.shape
    depth = w_in.shape[0]
    T = B * L
    tm = min(256, T)
    x2 = x.reshape(T, D)
    vec = lambda a: a.reshape(1, -1).astype(F32)
    at_a = lambda a: jnp.zeros((1, SMALL_W), F32).at[0, OFF_A:OFF_A + B_HEADS].set(a.astype(F32))
    for l in range(depth):
        (cq, ckv, kidx, small, gates, qn, kn, vn, zb, ga, gb) = _in_proj(
            x2, vec(mix_norm_g[l]), _reorder_w_in(w_in[l]), vec(cq_norm_g[l]),
            vec(ckv_norm_g[l]), vec(kidx_ln_g[l]), vec(kidx_ln_b[l]), conv_w[l].astype(F32),
            at_a(a_log[l]), at_a(dt_bias[l]), tm=tm, seq=L)
        seq = lambda a: a.reshape(B, L, a.shape[-1])
        o_a = _dsa(seq(cq), seq(small), seq(ckv), seq(kidx), w_uq[l].astype(BF16),
                   w_uk[l].astype(BF16), w_uv[l].astype(BF16), w_iq[l].astype(BF16))
        o_b = _gdn(seq(qn), seq(kn), seq(vn), seq(zb), seq(gates), vec(onorm_g[l]))
        x2 = _merge(x2, o_a.reshape(T, -1), o_b.reshape(T, -1), ga, gb,
                    w_branch_a[l].astype(BF16), w_branch_b[l].astype(BF16),
                    w_out[l].astype(BF16), vec(ffn_norm_g[l]), w_gate[l].astype(BF16),
                    w_up[l].astype(BF16), w_down[l].astype(BF16), vec(final_norm_g),
                    tm=tm, final_norm=(l == depth - 1))
    return x2.reshape(B, L, D)
```

```python
import functools

import jax
import jax.numpy as jnp
from jax import lax
from jax.experimental import pallas as pl
from jax.experimental.pallas import tpu as pltpu

F32 = jnp.float32
BF16 = jnp.bfloat16

EPS = 1e-6
A_HEADS = 16
A_QK_DIM = 64
A_V_DIM = 64
A_Q_RANK = 256
A_KV_RANK = 256
IDX_HEADS = 8
IDX_DIM = 64
TOPK_MAX = 256
Q_BLOCK = 128
B_HEADS = 8
B_K_DIM = 128
B_V_DIM = 128
CONV_WIDTH = 4
CHUNK = 64

GDN_W = B_HEADS * B_K_DIM
assert B_K_DIM == B_V_DIM
LANES = 128
SUBLANES = 8
SMALL_W = LANES
OFF_WIDX = IDX_DIM
OFF_BETA = IDX_DIM + IDX_HEADS
OFF_A = IDX_DIM + IDX_HEADS + B_HEADS
VMEM_LIMIT = 56 * 1024 * 1024
NEG_BIG = -1e30
LOG2_E = 1.4426950408889634
BISECT_STEPS = 16
BISECT_UNROLL = 4
HEAD_GROUP = 4
KEY_CHUNK = 512
SEARCH_GROUPS = 4
GDN_HEADS_PER_STEP = 2
PREP_GROUP = 8


def _resident(shape):
    nd = len(shape)
    return pl.BlockSpec(shape, lambda *_: (0,) * nd, pipeline_mode=pl.Buffered(1))


def _rms(x, g):
    return x * lax.rsqrt(jnp.mean(x * x, axis=-1, keepdims=True) + EPS) * g


def _l2n(x):
    return x * lax.rsqrt(jnp.sum(x * x, axis=-1, keepdims=True) + EPS)


def _chunk_cumsum(g, chunk):
    pos = lax.broadcasted_iota(jnp.int32, g.shape, 0) % chunk
    d = 1
    while d < chunk:
        g = g + jnp.where(pos >= d, pltpu.roll(g, d, 0), 0.0)
        d *= 2
    return g


def _conv_silu_tile(x, tail_ref, w):
    full = jnp.concatenate([tail_ref[...], x], axis=0)
    y = x * w[CONV_WIDTH - 1:CONV_WIDTH, :]
    for d in range(1, CONV_WIDTH):
        y = y + pltpu.roll(full, d, 0)[SUBLANES:] * w[CONV_WIDTH - 1 - d:CONV_WIDTH - d, :]
    tail_ref[...] = x[x.shape[0] - SUBLANES:]
    return y * jax.nn.sigmoid(y)


def _in_proj_kernel(x_ref, g_ref, w_ref, cqg_ref, ckvg_ref, lng_ref, lnb_ref, cw_ref,
                    alog_ref, dtb_ref,
                    cq_ref, ckv_ref, kidx_ref, small_ref, gates_ref, q_ref, k_ref, v_ref,
                    z_ref, ga_ref, gb_ref, tail_ref, *, d_model, tiles_per_seq):
    @pl.when(pl.program_id(0) % tiles_per_seq == 0)
    def _():
        tail_ref[...] = jnp.zeros_like(tail_ref)

    h = _rms(x_ref[...], g_ref[...]).astype(BF16)

    def proj(c0, width):
        return jnp.dot(h, w_ref[:, c0:c0 + width], preferred_element_type=F32)

    pair = 2 * B_K_DIM
    for part, ref in enumerate((q_ref, k_ref, v_ref)):
        for c in range(0, GDN_W, pair):
            cc = part * GDN_W + c
            y = _conv_silu_tile(proj(cc, pair), tail_ref.at[:, cc:cc + pair],
                                cw_ref[:, cc:cc + pair])
            if part < 2:
                scale = B_K_DIM ** -0.5 if part == 0 else 1.0
                y = jnp.concatenate([_l2n(y[:, :B_K_DIM]), _l2n(y[:, B_K_DIM:])], axis=1) * scale
            ref[:, c:c + pair] = y.astype(ref.dtype)
    c0 = 3 * GDN_W
    for ref, width in ((z_ref, GDN_W), (ga_ref, d_model), (gb_ref, d_model)):
        ref[...] = proj(c0, width)
        c0 += width
    cq_ref[...] = _rms(proj(c0, A_Q_RANK), cqg_ref[...]).astype(BF16)
    c0 += A_Q_RANK
    ckv_ref[...] = _rms(proj(c0, A_KV_RANK), ckvg_ref[...]).astype(BF16)
    c0 += A_KV_RANK
    small = proj(c0, SMALL_W)
    small_ref[...] = small
    kraw = small[:, :IDX_DIM]
    mu = jnp.mean(kraw, axis=-1, keepdims=True)
    kc = kraw - mu
    kn = kc * lax.rsqrt(jnp.mean(kc * kc, axis=-1, keepdims=True) + EPS)
    kidx_ref[...] = (kn * lng_ref[...] + lnb_ref[...]).astype(BF16)
    g = -jnp.exp(alog_ref[...]) * jax.nn.softplus(small + dtb_ref[...])
    lane = lax.broadcasted_iota(jnp.int32, small.shape, 1)
    gates_ref[...] = jnp.where(lane >= OFF_A, _chunk_cumsum(g, CHUNK), jax.nn.sigmoid(small))


def _in_proj(x2, g, w_all, cq_g, ckv_g, ln_g, ln_b, conv_w, alog_v, dtb_v, *, tm, seq):
    T, D = x2.shape
    n_all = w_all.shape[1]
    assert tm % CHUNK == 0 and seq % tm == 0
    row = lambda w: pl.BlockSpec((tm, w), lambda i: (i, 0))
    wide = ((GDN_W, BF16), (GDN_W, BF16), (GDN_W, BF16), (GDN_W, F32),
            (D, F32), (D, F32))
    out_shapes = (
        jax.ShapeDtypeStruct((T, A_Q_RANK), BF16),
        jax.ShapeDtypeStruct((T, A_KV_RANK), BF16),
        jax.ShapeDtypeStruct((T, IDX_DIM), BF16),
        jax.ShapeDtypeStruct((T, SMALL_W), F32),
        jax.ShapeDtypeStruct((T, SMALL_W), F32),
    ) + tuple(jax.ShapeDtypeStruct((T, w), dt) for w, dt in wide)
    out_specs = (row(A_Q_RANK), row(A_KV_RANK), row(IDX_DIM), row(SMALL_W), row(SMALL_W)
                 ) + tuple(row(w) for w, _ in wide)
    return pl.pallas_call(
        functools.partial(_in_proj_kernel, d_model=D, tiles_per_seq=seq // tm),
        grid=(T // tm,),
        in_specs=[row(D), _resident((1, D)), _resident((D, n_all)),
                  _resident((1, A_Q_RANK)), _resident((1, A_KV_RANK)),
                  _resident((1, IDX_DIM)), _resident((1, IDX_DIM)),
                  _resident(conv_w.shape), _resident((1, SMALL_W)), _resident((1, SMALL_W))],
        out_specs=out_specs,
        out_shape=out_shapes,
        scratch_shapes=[pltpu.VMEM((SUBLANES, conv_w.shape[1]), F32)],
        compiler_params=pltpu.CompilerParams(
            dimension_semantics=("arbitrary",), vmem_limit_bytes=VMEM_LIMIT),
        name="in_proj",
    )(x2, g, w_all, cq_g, ckv_g, ln_g, ln_b, conv_w, alog_v, dtb_v)


def _topk_bias(score_ref, bias_ref, row0, n_chunks, topk, tri_ones):
    _, nq, kc = score_ref.shape
    rg = nq // SEARCH_GROUPS
    groups = range(SEARCH_GROUPS)
    rsum = lambda x: jnp.sum(x, axis=1, keepdims=True)
    rmin = lambda x: jnp.min(x, axis=1, keepdims=True)

    def ld(g):
        return jnp.concatenate(
            [score_ref[c, g * rg:(g + 1) * rg, :] for c in range(n_chunks)], axis=1)

    pos = lambda g: row0 + g * rg + lax.broadcasted_iota(jnp.int32, (rg, 1), 0)
    k_eff = [jnp.minimum(pos(g) + 1, topk).astype(F32) for g in groups]
    lo = tuple(rmin(jnp.where(ld(g) == -jnp.inf, jnp.inf, ld(g))) for g in groups)
    hi = tuple(jnp.max(ld(g), axis=1, keepdims=True) for g in groups)

    def bisect(_, carry):
        lo, hi = carry
        mid = [0.5 * (a + b) for a, b in zip(lo, hi)]
        cnt = [rsum(jnp.where(ld(g) >= mid[g], 1.0, 0.0)) for g in groups]
        ge = [cnt[g] >= k_eff[g] for g in groups]
        return (tuple(jnp.where(ge[g], mid[g], lo[g]) for g in groups),
                tuple(jnp.where(ge[g], hi[g], mid[g]) for g in groups))

    lo, hi = lax.fori_loop(0, BISECT_STEPS, bisect, (lo, hi), unroll=BISECT_UNROLL)

    def above(g, v):
        s = ld(g)
        gt = s > v
        return rsum(jnp.where(gt, 1.0, 0.0)), rmin(jnp.where(gt, s, jnp.inf))

    v0 = tuple(rmin(jnp.where(ld(g) >= lo[g], ld(g), jnp.inf)) for g in groups)
    first = [above(g, v0[g]) for g in groups]

    def cond(c):
        _, n_gt, _ = c
        flags = [jnp.max(jnp.where(n_gt[g] >= k_eff[g], 1.0, 0.0)) for g in groups]
        return functools.reduce(jnp.maximum, flags) > 0.0

    def body(c):
        v, n_gt, nxt = c
        v = tuple(jnp.where(n_gt[g] >= k_eff[g], nxt[g], v[g]) for g in groups)
        nxt_state = [above(g, v[g]) for g in groups]
        return v, tuple(a for a, _ in nxt_state), tuple(b for _, b in nxt_state)

    tau, n_gt, _ = lax.while_loop(
        cond, body, (v0, tuple(a for a, _ in first), tuple(b for _, b in first)))
    k_all = jnp.concatenate(k_eff, axis=0)
    need = k_all - jnp.concatenate(n_gt, axis=0)
    tau = jnp.concatenate(tau, axis=0)

    n_ge = jnp.zeros((nq, LANES), F32)
    for c in range(n_chunks):
        for j in range(kc // LANES):
            ge = score_ref[c, :, j * LANES:(j + 1) * LANES] >= tau
            bias_ref[c, :, j * LANES:(j + 1) * LANES] = jnp.where(ge, 0.0, NEG_BIG)
            n_ge = n_ge + jnp.where(ge, 1.0, 0.0)
    surplus = jnp.max(jnp.where(rsum(n_ge) > k_all, 1.0, 0.0)) > 0.0

    @pl.when(surplus)
    def _():
        carry = jnp.zeros((nq, LANES), F32)
        for c in range(n_chunks):
            for j in range(kc // LANES):
                sc = score_ref[c, :, j * LANES:(j + 1) * LANES]
                eq = sc == tau
                r = jnp.dot(jnp.where(eq, 1.0, 0.0).astype(BF16), tri_ones,
                            preferred_element_type=F32)
                prefix = r[:, :LANES] + carry
                carry = carry + r[:, LANES:]
                sel = (sc > tau) | (eq & (prefix <= need))
                bias_ref[c, :, j * LANES:(j + 1) * LANES] = jnp.where(sel, 0.0, NEG_BIG)


def _dsa_kernel(cq_ref, small_ref, ckv_ref, kidx_ref, wuq_ref, wuk_ref, wuv_ref, wiq_ref,
                o_ref, qall_ref, qidx_ref, score_ref, bias_ref, s_ref, m_ref, l_ref, acc_ref,
                *, topk):
    i = pl.program_id(1)
    nq = Q_BLOCK
    max_chunks, _, kc = score_ref.shape
    n_kc = (i * nq) // kc + 1
    ng = A_HEADS // HEAD_GROUP
    rows = HEAD_GROUP * nq
    tiles = kc // LANES
    nt = (((1,), (1,)), ((), ()))

    cq = cq_ref[0]
    q = jnp.dot(cq, wuq_ref[...], preferred_element_type=F32).astype(BF16)
    for h in range(A_HEADS):
        ql = jnp.dot(q[:, h * A_QK_DIM:(h + 1) * A_QK_DIM], wuk_ref[h],
                     preferred_element_type=F32) * (A_QK_DIM ** -0.5 * LOG2_E)
        qall_ref[h * nq:(h + 1) * nq, :] = ql.astype(BF16)
    qi = jnp.dot(cq, wiq_ref[...], preferred_element_type=F32).astype(BF16)
    for h in range(IDX_HEADS):
        qidx_ref[h * nq:(h + 1) * nq, :] = qi[:, h * IDX_DIM:(h + 1) * IDX_DIM]
    w_idx = small_ref[0][:, OFF_WIDX:OFF_WIDX + IDX_HEADS] * (IDX_HEADS ** -0.5 * IDX_DIM ** -0.5)
    row_pos = i * nq + lax.broadcasted_iota(jnp.int32, (nq, 1), 0)

    def key_rows(c):
        return pl.ds(pl.multiple_of(c * kc, kc), kc)

    def score_chunk(c, _):
        logits = lax.dot_general(qidx_ref[...], kidx_ref[0, key_rows(c), :], nt,
                                 preferred_element_type=F32)
        sc = jnp.zeros((nq, kc), F32)
        for h in range(IDX_HEADS):
            sc = sc + w_idx[:, h:h + 1] * jnp.maximum(logits[h * nq:(h + 1) * nq], 0.0)
        col = c * kc + lax.broadcasted_iota(jnp.int32, (nq, kc), 1)
        score_ref[c] = jnp.where(col <= row_pos, sc, -jnp.inf)
        return 0

    lax.fori_loop(0, n_kc, score_chunk, 0)

    r_i = lax.broadcasted_iota(jnp.int32, (LANES, 2 * LANES), 0)
    c_i = lax.broadcasted_iota(jnp.int32, (LANES, 2 * LANES), 1)
    tri_ones = jnp.where((c_i >= LANES) | (r_i <= c_i), 1.0, 0.0).astype(BF16)
    keep_all = (i + 1) * nq <= topk

    @pl.when(keep_all)
    def _():
        bias_ref[0] = jnp.where(score_ref[0] > -jnp.inf, 0.0, NEG_BIG)

    for v in range(max_chunks):
        pl.when(jnp.logical_and(n_kc == v + 1, jnp.logical_not(keep_all)))(functools.partial(
            _topk_bias, score_ref, bias_ref, i * nq, v + 1, topk, tri_ones))

    m_ref[...] = jnp.full(m_ref.shape, -jnp.inf, F32)
    l_ref[...] = jnp.zeros_like(l_ref)
    acc_ref[...] = jnp.zeros_like(acc_ref)

    def pass1(c, _):
        ckv = ckv_ref[0, key_rows(c), :]
        bias = bias_ref[c]
        ss = [lax.dot_general(qall_ref[g * rows:(g + 1) * rows, :], ckv, nt,
                              preferred_element_type=F32) for g in range(ng)]
        for g in range(ng):
            s = (ss[g].reshape(HEAD_GROUP, nq, kc) + bias[None]).reshape(rows, kc)
            s_ref[c, g] = s
            mt = s[:, :LANES]
            for j in range(1, tiles):
                mt = jnp.maximum(mt, s[:, j * LANES:(j + 1) * LANES])
            m_ref[g] = jnp.maximum(m_ref[g], mt)
        return 0

    lax.fori_loop(0, n_kc, pass1, 0)
    for g in range(ng):
        m_ref[g] = jnp.broadcast_to(jnp.max(m_ref[g], axis=1, keepdims=True), (rows, LANES))

    def pass2(c, _):
        ckv = ckv_ref[0, key_rows(c), :]
        for g in range(ng):
            s = s_ref[c, g]
            m = m_ref[g]
            p = [jnp.exp2(s[:, j * LANES:(j + 1) * LANES] - m) for j in range(tiles)]
            l_ref[g] += functools.reduce(lambda a, b: a + b, p)
            pb = jnp.concatenate(p, axis=1).astype(BF16)
            acc_ref[g] += jnp.dot(pb, ckv, preferred_element_type=F32)
        return 0

    lax.fori_loop(0, n_kc, pass2, 0)
    for g in range(ng):
        l = jnp.sum(l_ref[g], axis=1, keepdims=True)
        og = (acc_ref[g] / l).astype(BF16)
        for hh in range(HEAD_GROUP):
            h = g * HEAD_GROUP + hh
            o_ref[0, :, h * A_V_DIM:(h + 1) * A_V_DIM] = jnp.dot(
                og[hh * nq:(hh + 1) * nq], wuv_ref[h],
                preferred_element_type=F32).astype(o_ref.dtype)


def _dsa(cq, small, ckv, kidx, w_uq, w_uk, w_uv, w_iq):
    B, L, _ = cq.shape
    topk = min(TOPK_MAX, L // 4)
    kc = min(KEY_CHUNK, L)
    assert L % kc == 0 and kc % Q_BLOCK == 0 and topk <= kc
    n_chunks = L // kc
    nb = L // Q_BLOCK
    ng = A_HEADS // HEAD_GROUP
    rows = HEAD_GROUP * Q_BLOCK
    a_width = A_HEADS * A_V_DIM
    return pl.pallas_call(
        functools.partial(_dsa_kernel, topk=topk),
        grid=(B, nb),
        in_specs=[
            pl.BlockSpec((1, Q_BLOCK, A_Q_RANK), lambda b, i: (b, i, 0)),
            pl.BlockSpec((1, Q_BLOCK, SMALL_W), lambda b, i: (b, i, 0)),
            pl.BlockSpec((1, L, A_KV_RANK), lambda b, i: (b, 0, 0)),
            pl.BlockSpec((1, L, IDX_DIM), lambda b, i: (b, 0, 0)),
            _resident(w_uq.shape), _resident(w_uk.shape), _resident(w_uv.shape),
            _resident(w_iq.shape),
        ],
        out_specs=pl.BlockSpec((1, Q_BLOCK, a_width), lambda b, i: (b, i, 0)),
        out_shape=jax.ShapeDtypeStruct((B, L, a_width), BF16),
        scratch_shapes=[
            pltpu.VMEM((A_HEADS * Q_BLOCK, A_KV_RANK), BF16),
            pltpu.VMEM((IDX_HEADS * Q_BLOCK, IDX_DIM), BF16),
            pltpu.VMEM((n_chunks, Q_BLOCK, kc), F32),
            pltpu.VMEM((n_chunks, Q_BLOCK, kc), F32),
            pltpu.VMEM((n_chunks, ng, rows, kc), F32),
            pltpu.VMEM((ng, rows, LANES), F32),
            pltpu.VMEM((ng, rows, LANES), F32),
            pltpu.VMEM((ng, rows, A_KV_RANK), F32),
        ],
        compiler_params=pltpu.CompilerParams(
            dimension_semantics=("arbitrary", "arbitrary"), vmem_limit_bytes=VMEM_LIMIT),
        name="dsa",
    )(cq, small, ckv, kidx, w_uq, w_uk, w_uv, w_iq)


def _lane_bcast(x, c):
    lane = lax.broadcasted_iota(jnp.int32, x.shape, 1)
    col = jnp.sum(jnp.where(lane == c, x, 0.0), axis=1, keepdims=True)
    return jnp.broadcast_to(col, x.shape)


def _gdn_kernel(q_ref, k_ref, v_ref, z_ref, gates_ref, og_ref, o_ref,
                mneg_ref, r_ref, qeff_ref, o0_ref, at_ref, oraw_ref, *, seq, hp):
    hg = pl.program_id(1)
    C = CHUNK
    n_chunks = seq // C
    dk = B_K_DIM

    r64 = lax.broadcasted_iota(jnp.int32, (C, C), 0)
    c64 = lax.broadcasted_iota(jnp.int32, (C, C), 1)
    tri = r64 >= c64
    strict = r64 > c64
    eye = jnp.where(r64 == c64, 1.0, 0.0)
    nt = (((1,), (1,)), ((), ()))
    dot = functools.partial(jnp.dot, preferred_element_type=F32)

    def prep(t, _):
        ids = [(hh, t * PREP_GROUP + j) for j in range(PREP_GROUP) for hh in range(hp)]
        each = lambda f, *cols: [f(*a) for a in zip(*cols)]
        rows = [pl.ds(pl.multiple_of(n * C, C), C) for _, n in ids]
        load = lambda ref: [ref[0, r, hh * dk:(hh + 1) * dk] for (hh, _), r in zip(ids, rows)]
        qb, kb, vb = (load(r) for r in (q_ref, k_ref, v_ref))
        q, k, v = (each(lambda a: a.astype(F32), x) for x in (qb, kb, vb))
        gt = [gates_ref[0, r, :] for r in rows]
        beta = [_lane_bcast(a, OFF_BETA + hg * hp + hh) for a, (hh, _) in zip(gt, ids)]
        G = [_lane_bcast(a, OFF_A + hg * hp + hh) for a, (hh, _) in zip(gt, ids)]
        decay = each(lambda g: jnp.exp(jnp.where(tri, g[:, :C] - g.T[:C, :C], -jnp.inf)), G)
        eG = each(jnp.exp, G)
        kk = each(lambda a: lax.dot_general(a, a, nt, preferred_element_type=F32), kb)
        N = each(lambda b, a, d: jnp.where(strict, b[:, :C] * a * d, 0.0), beta, kk, decay)
        X = each(lambda a: eye - a, N)
        Nb = each(lambda a: a.astype(BF16), N)
        Pb = each(lambda a: dot(a, a).astype(BF16), Nb)
        steps = C.bit_length() - 2
        for it in range(steps):
            X = each(lambda x, p: x + dot(x.astype(BF16), p), X, Pb)
            if it + 1 < steps:
                Pb = each(lambda p: dot(p, p).astype(BF16), Pb)
        rhs = each(lambda v_, k_, b, e: jnp.concatenate(
            [v_ * b, k_ * (b * e)], axis=-1).astype(BF16), v, k, beta, eG)
        sol = each(lambda x, r: dot(x.astype(BF16), r).astype(BF16), X, rhs)
        ub = [s[:, :B_V_DIM] for s in sol]
        wd = [s[:, B_V_DIM:] for s in sol]
        qk = each(lambda q_, k_, d: (lax.dot_general(
            q_, k_, nt, preferred_element_type=F32) * d).astype(BF16), qb, kb, decay)
        ktT = each(lambda k_, g: (k_ * jnp.exp(g[C - 1:C, :] - g)).T.astype(BF16), k, G)
        mneg = each(lambda a, b: dot(a, b).astype(BF16), ktT, wd)
        rr = each(dot, ktT, ub)
        qeff = each(lambda q_, e, a, b: (q_ * e - dot(a, b)).astype(BF16), q, eG, qk, wd)
        o0 = each(dot, qk, ub)
        for i, (hh, n) in enumerate(ids):
            mneg_ref[hh, n] = mneg[i]
            r_ref[hh, n] = rr[i]
            qeff_ref[hh, n] = qeff[i]
            o0_ref[hh, n] = o0[i]
            at_ref[hh, n] = jnp.exp(G[i][C - 1:C, :])
        return 0

    lax.fori_loop(0, n_chunks // PREP_GROUP, prep, 0)

    def scan(n, S):
        rows = pl.ds(pl.multiple_of(n * C, C), C)
        Sb = [s.astype(BF16) for s in S]
        ms = [dot(mneg_ref[hh, n], Sb[hh]) for hh in range(hp)]
        os_ = [dot(qeff_ref[hh, n], Sb[hh]) for hh in range(hp)]
        new = []
        for hh in range(hp):
            new.append(S[hh] * at_ref[hh, n] - ms[hh] + r_ref[hh, n])
            oraw_ref[hh, rows, :] = os_[hh] + o0_ref[hh, n]
        return tuple(new)

    lax.fori_loop(0, n_chunks, scan, tuple(jnp.zeros((dk, B_V_DIM), F32) for _ in range(hp)))

    for hh in range(hp):
        ls = slice(hh * dk, (hh + 1) * dk)
        z = z_ref[0, :, ls]
        o_ref[0, :, ls] = (_rms(oraw_ref[hh], og_ref[...])
                           * (z * jax.nn.sigmoid(z))).astype(o_ref.dtype)


def _gdn(qn, kn, vn, zb, gates, onorm_g):
    B, L, W = qn.shape
    hp = GDN_HEADS_PER_STEP
    n_chunks = L // CHUNK
    heads = W // B_K_DIM
    assert heads % hp == 0 and n_chunks % PREP_GROUP == 0
    col = pl.BlockSpec((1, L, hp * B_K_DIM), lambda b, h: (b, 0, h))
    return pl.pallas_call(
        functools.partial(_gdn_kernel, seq=L, hp=hp),
        grid=(B, heads // hp),
        in_specs=[col, col, col, col,
                  pl.BlockSpec((1, L, SMALL_W), lambda b, h: (b, 0, 0)),
                  _resident((1, B_V_DIM))],
        out_specs=col,
        out_shape=jax.ShapeDtypeStruct((B, L, W), BF16),
        scratch_shapes=[
            pltpu.VMEM((hp, n_chunks, B_K_DIM, B_K_DIM), BF16),
            pltpu.VMEM((hp, n_chunks, B_K_DIM, B_V_DIM), F32),
            pltpu.VMEM((hp, n_chunks, CHUNK, B_K_DIM), BF16),
            pltpu.VMEM((hp, n_chunks, CHUNK, B_V_DIM), F32),
            pltpu.VMEM((hp, n_chunks, 1, LANES), F32),
            pltpu.VMEM((hp, L, B_V_DIM), F32),
        ],
        compiler_params=pltpu.CompilerParams(
            dimension_semantics=("arbitrary", "arbitrary"), vmem_limit_bytes=VMEM_LIMIT),
        name="gdn",
    )(qn, kn, vn, zb, gates, onorm_g)


def _merge_kernel(x_ref, oa_ref, ob_ref, ga_ref, gb_ref, wa_ref, wb_ref, wo_ref,
                  fg_ref, wg_ref, wu_ref, wd_ref, og_ref, o_ref, *, ff_chunk, final_norm):
    ya = jnp.dot(oa_ref[...], wa_ref[...], preferred_element_type=F32)
    yb = jnp.dot(ob_ref[...], wb_ref[...], preferred_element_type=F32)
    merged = jax.nn.sigmoid(ga_ref[...]) * ya + jax.nn.sigmoid(gb_ref[...]) * yb
    x1 = x_ref[...] + jnp.dot(merged.astype(BF16), wo_ref[...], preferred_element_type=F32)
    h = _rms(x1, fg_ref[...]).astype(BF16)
    d_ff = wg_ref.shape[1]
    acc = x1
    for c0 in range(0, d_ff, ff_chunk):
        gate = jnp.dot(h, wg_ref[:, c0:c0 + ff_chunk], preferred_element_type=F32)
        up = jnp.dot(h, wu_ref[:, c0:c0 + ff_chunk], preferred_element_type=F32)
        act = (gate * jax.nn.sigmoid(gate) * up).astype(BF16)
        acc = acc + jnp.dot(act, wd_ref[c0:c0 + ff_chunk, :], preferred_element_type=F32)
    o_ref[...] = _rms(acc, og_ref[...]) if final_norm else acc


def _merge(x2, oa, ob, ga, gb, wa, wb, wo, fg, wg, wu, wd, og, *, tm, final_norm):
    T, D = x2.shape
    d_ff = wg.shape[1]
    ff_chunk = d_ff // 2 if (d_ff // 2) % LANES == 0 else d_ff
    row = pl.BlockSpec((tm, D), lambda i: (i, 0))
    return pl.pallas_call(
        functools.partial(_merge_kernel, ff_chunk=ff_chunk, final_norm=final_norm),
        grid=(T // tm,),
        in_specs=[row, row, row, row, row,
                  _resident(wa.shape), _resident(wb.shape), _resident(wo.shape),
                  _resident(fg.shape), _resident(wg.shape), _resident(wu.shape),
                  _resident(wd.shape), _resident(og.shape)],
        out_specs=row,
        out_shape=jax.ShapeDtypeStruct((T, D), F32),
        compiler_params=pltpu.CompilerParams(
            dimension_semantics=("arbitrary",), vmem_limit_bytes=VMEM_LIMIT),
        name="merge_ffn",
    )(x2, oa, ob, ga, gb, wa, wb, wo, fg, wg, wu, wd, og)


def _reorder_w_in(w):
    sizes = (A_Q_RANK, A_KV_RANK, IDX_DIM, IDX_HEADS,
             B_HEADS * B_K_DIM, B_HEADS * B_K_DIM, B_HEADS * B_V_DIM, B_HEADS, B_HEADS,
             B_HEADS * B_V_DIM, w.shape[0], w.shape[0])
    parts, c0 = [], 0
    for s in sizes:
        parts.append(w[:, c0:c0 + s])
        c0 += s
    (c_q, c_kv, k_idx, w_idx, q_b, k_b, v_b, beta_b, a_b, z_b, gate_a, gate_b) = parts
    pad = jnp.zeros((w.shape[0], SMALL_W - IDX_DIM - IDX_HEADS - 2 * B_HEADS), w.dtype)
    return jnp.concatenate(
        [q_b, k_b, v_b, z_b, gate_a, gate_b, c_q, c_kv, k_idx, w_idx, beta_b, a_b, pad],
        axis=1).astype(BF16)


def kernel(x, mix_norm_g, w_in, cq_norm_g, ckv_norm_g, w_uq, w_uk, w_uv, w_iq, kidx_ln_g, kidx_ln_b, w_branch_a, conv_w, a_log, dt_bias, onorm_g, w_branch_b, w_out, ffn_norm_g, w_gate, w_up, w_down, final_norm_g):
    B, L, D = x.shape
    depth = w_in.shape[0]
    T = B * L
    tm = min(256, T)
    x2 = x.reshape(T, D)
    vec = lambda a: a.reshape(1, -1).astype(F32)
    at_a = lambda a: jnp.zeros((1, SMALL_W), F32).at[0, OFF_A:OFF_A + B_HEADS].set(a.astype(F32))
    for l in range(depth):
        (cq, ckv, kidx, small, gates, qn, kn, vn, zb, ga, gb) = _in_proj(
            x2, vec(mix_norm_g[l]), _reorder_w_in(w_in[l]), vec(cq_norm_g[l]),
            vec(ckv_norm_g[l]), vec(kidx_ln_g[l]), vec(kidx_ln_b[l]), conv_w[l].astype(F32),
            at_a(a_log[l]), at_a(dt_bias[l]), tm=tm, seq=L)
        seq = lambda a: a.reshape(B, L, a.shape[-1])
        o_a = _dsa(seq(cq), seq(small), seq(ckv), seq(kidx), w_uq[l].astype(BF16),
                   w_uk[l].astype(BF16), w_uv[l].astype(BF16), w_iq[l].astype(BF16))
        o_b = _gdn(seq(qn), seq(kn), seq(vn), seq(zb), seq(gates), vec(onorm_g[l]))
        x2 = _merge(x2, o_a.reshape(T, -1), o_b.reshape(T, -1), ga, gb,
                    w_branch_a[l].astype(BF16), w_branch_b[l].astype(BF16),
                    w_out[l].astype(BF16), vec(ffn_norm_g[l]), w_gate[l].astype(BF16),
                    w_up[l].astype(BF16), w_down[l].astype(BF16), vec(final_norm_g),
                    tm=tm, final_norm=(l == depth - 1))
    return x2.reshape(B, L, D)
```

```python
import functools

import jax
import jax.numpy as jnp
from jax import lax
from jax.experimental import pallas as pl
from jax.experimental.pallas import tpu as pltpu

F32 = jnp.float32
BF16 = jnp.bfloat16

EPS = 1e-6
A_HEADS = 16
A_QK_DIM = 64
A_V_DIM = 64
A_Q_RANK = 256
A_KV_RANK = 256
IDX_HEADS = 8
IDX_DIM = 64
TOPK_MAX = 256
Q_BLOCK = 128
B_HEADS = 8
B_K_DIM = 128
B_V_DIM = 128
CONV_WIDTH = 4
CHUNK = 64

GDN_W = B_HEADS * B_K_DIM
assert B_K_DIM == B_V_DIM
LANES = 128
SUBLANES = 8
SMALL_W = LANES
OFF_WIDX = IDX_DIM
OFF_BETA = IDX_DIM + IDX_HEADS
OFF_A = IDX_DIM + IDX_HEADS + B_HEADS
VMEM_LIMIT = 56 * 1024 * 1024
NEG_BIG = -1e30
LOG2_E = 1.4426950408889634
BISECT_STEPS = 16
BISECT_UNROLL = 8
HEAD_GROUP = 4
KEY_CHUNK = 512
SEARCH_GROUPS = 4
GDN_HEADS_PER_STEP = 4
PREP_GROUP = 4


def _resident(shape):
    nd = len(shape)
    return pl.BlockSpec(shape, lambda *_: (0,) * nd, pipeline_mode=pl.Buffered(1))


def _rms(x, g):
    return x * lax.rsqrt(jnp.mean(x * x, axis=-1, keepdims=True) + EPS) * g


def _l2n(x, scale=1.0):
    return x * (lax.rsqrt(jnp.sum(x * x, axis=-1, keepdims=True) + EPS) * scale)


def _chunk_cumsum(g, chunk):
    pos = lax.broadcasted_iota(jnp.int32, g.shape, 0) % chunk
    d = 1
    while d < chunk:
        g = g + jnp.where(pos >= d, pltpu.roll(g, d, 0), 0.0)
        d *= 2
    return g


def _sigmoid(x):
    return 0.5 * jnp.tanh(0.5 * x) + 0.5


def _silu(x):
    u = 0.5 * x
    return u + u * jnp.tanh(u)


def _conv_silu_tile(x, tail_ref, w_half):
    full = jnp.concatenate([tail_ref[...], x], axis=0)
    u = x * w_half[CONV_WIDTH - 1:CONV_WIDTH, :]
    for d in range(1, CONV_WIDTH):
        u = u + pltpu.roll(full, d, 0)[SUBLANES:] * w_half[CONV_WIDTH - 1 - d:CONV_WIDTH - d, :]
    tail_ref[...] = x[x.shape[0] - SUBLANES:]
    return u + u * jnp.tanh(u)


def _in_proj_kernel(x_ref, g_ref, w_ref, cqg_ref, ckvg_ref, lng_ref, lnb_ref, cw_ref,
                    alog_ref, dtb_ref,
                    cq_ref, ckv_ref, kidx_ref, small_ref, gates_ref, q_ref, k_ref, v_ref,
                    z_ref, ga_ref, gb_ref, tail_ref, *, d_model, tiles_per_seq):
    @pl.when(pl.program_id(0) % tiles_per_seq == 0)
    def _():
        tail_ref[...] = jnp.zeros_like(tail_ref)

    h = _rms(x_ref[...], g_ref[...]).astype(BF16)

    def proj(c0, width):
        return jnp.dot(h, w_ref[:, c0:c0 + width], preferred_element_type=F32)

    pair = 2 * B_K_DIM
    plain, c0 = [], 3 * GDN_W
    for ref, width in ((z_ref, GDN_W), (ga_ref, d_model), (gb_ref, d_model)):
        plain += [(ref, c, c0 + c) for c in range(0, width, pair)]
        c0 += width
    steps = [(part, c) for part in range(3) for c in range(0, GDN_W, pair)]
    for idx, (part, c) in enumerate(steps):
        ref = (q_ref, k_ref, v_ref)[part]
        cc = part * GDN_W + c
        y = _conv_silu_tile(proj(cc, pair), tail_ref.at[:, cc:cc + pair],
                            cw_ref[:, cc:cc + pair] * 0.5)
        if part < 2:
            scale = B_K_DIM ** -0.5 if part == 0 else 1.0
            y = jnp.concatenate([_l2n(y[:, :B_K_DIM], scale), _l2n(y[:, B_K_DIM:], scale)], axis=1)
        ref[:, c:c + pair] = y.astype(ref.dtype)
        lo, hi = idx * len(plain) // len(steps), (idx + 1) * len(plain) // len(steps)
        for pref, pc, wc in plain[lo:hi]:
            pref[:, pc:pc + pair] = proj(wc, pair)
    cq_ref[...] = _rms(proj(c0, A_Q_RANK), cqg_ref[...]).astype(BF16)
    c0 += A_Q_RANK
    ckv_ref[...] = _rms(proj(c0, A_KV_RANK), ckvg_ref[...]).astype(BF16)
    c0 += A_KV_RANK
    small = proj(c0, SMALL_W)
    small_ref[...] = small
    kraw = small[:, :IDX_DIM]
    mu = jnp.mean(kraw, axis=-1, keepdims=True)
    kc = kraw - mu
    kn = kc * lax.rsqrt(jnp.mean(kc * kc, axis=-1, keepdims=True) + EPS)
    kidx_ref[...] = (kn * lng_ref[...] + lnb_ref[...]).astype(BF16)
    g = -jnp.exp(alog_ref[...]) * jax.nn.softplus(small + dtb_ref[...])
    lane = lax.broadcasted_iota(jnp.int32, small.shape, 1)
    gates_ref[...] = jnp.where(lane >= OFF_A, _chunk_cumsum(g, CHUNK), _sigmoid(small))


def _in_proj(x2, g, w_all, cq_g, ckv_g, ln_g, ln_b, conv_w, alog_v, dtb_v, *, tm, seq):
    T, D = x2.shape
    n_all = w_all.shape[1]
    assert tm % CHUNK == 0 and seq % tm == 0
    row = lambda w: pl.BlockSpec((tm, w), lambda i: (i, 0))
    wide = ((GDN_W, BF16), (GDN_W, BF16), (GDN_W, BF16), (GDN_W, F32),
            (D, F32), (D, F32))
    out_shapes = (
        jax.ShapeDtypeStruct((T, A_Q_RANK), BF16),
        jax.ShapeDtypeStruct((T, A_KV_RANK), BF16),
        jax.ShapeDtypeStruct((T, IDX_DIM), BF16),
        jax.ShapeDtypeStruct((T, SMALL_W), F32),
        jax.ShapeDtypeStruct((T, SMALL_W), F32),
    ) + tuple(jax.ShapeDtypeStruct((T, w), dt) for w, dt in wide)
    out_specs = (row(A_Q_RANK), row(A_KV_RANK), row(IDX_DIM), row(SMALL_W), row(SMALL_W)
                 ) + tuple(row(w) for w, _ in wide)
    return pl.pallas_call(
        functools.partial(_in_proj_kernel, d_model=D, tiles_per_seq=seq // tm),
        grid=(T // tm,),
        in_specs=[row(D), _resident((1, D)), _resident((D, n_all)),
                  _resident((1, A_Q_RANK)), _resident((1, A_KV_RANK)),
                  _resident((1, IDX_DIM)), _resident((1, IDX_DIM)),
                  _resident(conv_w.shape), _resident((1, SMALL_W)), _resident((1, SMALL_W))],
        out_specs=out_specs,
        out_shape=out_shapes,
        scratch_shapes=[pltpu.VMEM((SUBLANES, conv_w.shape[1]), F32)],
        compiler_params=pltpu.CompilerParams(
            dimension_semantics=("arbitrary",), vmem_limit_bytes=VMEM_LIMIT),
        name="in_proj",
    )(x2, g, w_all, cq_g, ckv_g, ln_g, ln_b, conv_w, alog_v, dtb_v)


def _topk_bias(score_ref, bias_ref, row0, n_chunks, topk, tri_ones):
    _, nq, kc = score_ref.shape
    rg = nq // SEARCH_GROUPS
    groups = range(SEARCH_GROUPS)
    rsum = lambda x: jnp.sum(x, axis=1, keepdims=True)
    rmin = lambda x: jnp.min(x, axis=1, keepdims=True)

    def ld(g):
        return jnp.concatenate(
            [score_ref[c, g * rg:(g + 1) * rg, :] for c in range(n_chunks)], axis=1)

    pos = lambda g: row0 + g * rg + lax.broadcasted_iota(jnp.int32, (rg, 1), 0)
    k_eff = [jnp.minimum(pos(g) + 1, topk).astype(F32) for g in groups]
    lo = tuple(rmin(jnp.where(ld(g) == -jnp.inf, jnp.inf, ld(g))) for g in groups)
    hi = tuple(jnp.max(ld(g), axis=1, keepdims=True) for g in groups)

    def bisect(_, carry):
        lo, hi = carry
        mid = [0.5 * (a + b) for a, b in zip(lo, hi)]
        cnt = [rsum(jnp.where(ld(g) >= mid[g], 1.0, 0.0)) for g in groups]
        ge = [cnt[g] >= k_eff[g] for g in groups]
        return (tuple(jnp.where(ge[g], mid[g], lo[g]) for g in groups),
                tuple(jnp.where(ge[g], hi[g], mid[g]) for g in groups))

    lo, hi = lax.fori_loop(0, BISECT_STEPS, bisect, (lo, hi), unroll=BISECT_UNROLL)

    def above(g, v):
        s = ld(g)
        gt = s > v
        return rsum(jnp.where(gt, 1.0, 0.0)), rmin(jnp.where(gt, s, jnp.inf))

    v0 = tuple(rmin(jnp.where(ld(g) >= lo[g], ld(g), jnp.inf)) for g in groups)
    first = [above(g, v0[g]) for g in groups]

    def cond(c):
        _, n_gt, _ = c
        flags = [jnp.max(jnp.where(n_gt[g] >= k_eff[g], 1.0, 0.0)) for g in groups]
        return functools.reduce(jnp.maximum, flags) > 0.0

    def body(c):
        v, n_gt, nxt = c
        v = tuple(jnp.where(n_gt[g] >= k_eff[g], nxt[g], v[g]) for g in groups)
        nxt_state = [above(g, v[g]) for g in groups]
        return v, tuple(a for a, _ in nxt_state), tuple(b for _, b in nxt_state)

    tau, n_gt, _ = lax.while_loop(
        cond, body, (v0, tuple(a for a, _ in first), tuple(b for _, b in first)))
    k_all = jnp.concatenate(k_eff, axis=0)
    need = k_all - jnp.concatenate(n_gt, axis=0)
    tau = jnp.concatenate(tau, axis=0)

    n_ge = jnp.zeros((nq, LANES), F32)
    for c in range(n_chunks):
        for j in range(kc // LANES):
            ge = score_ref[c, :, j * LANES:(j + 1) * LANES] >= tau
            bias_ref[c, :, j * LANES:(j + 1) * LANES] = jnp.where(ge, 0.0, NEG_BIG)
            n_ge = n_ge + jnp.where(ge, 1.0, 0.0)
    surplus = jnp.max(jnp.where(rsum(n_ge) > k_all, 1.0, 0.0)) > 0.0

    @pl.when(surplus)
    def _():
        carry = jnp.zeros((nq, LANES), F32)
        for c in range(n_chunks):
            for j in range(kc // LANES):
                sc = score_ref[c, :, j * LANES:(j + 1) * LANES]
                eq = sc == tau
                r = jnp.dot(jnp.where(eq, 1.0, 0.0).astype(BF16), tri_ones,
                            preferred_element_type=F32)
                prefix = r[:, :LANES] + carry
                carry = carry + r[:, LANES:]
                sel = (sc > tau) | (eq & (prefix <= need))
                bias_ref[c, :, j * LANES:(j + 1) * LANES] = jnp.where(sel, 0.0, NEG_BIG)


def _dsa_kernel(cq_ref, small_ref, ckv_ref, kidx_ref, wuq_ref, wuk_ref, wuv_ref, wiq_ref,
                o_ref, qall_ref, qidx_ref, score_ref, bias_ref, s_ref, m_ref, l_ref, acc_ref,
                *, topk):
    i = pl.program_id(1)
    nq = Q_BLOCK
    max_chunks, _, kc = score_ref.shape
    n_kc = (i * nq) // kc + 1
    ng = A_HEADS // HEAD_GROUP
    rows = HEAD_GROUP * nq
    tiles = kc // LANES
    nt = (((1,), (1,)), ((), ()))

    cq = cq_ref[0]
    q = jnp.dot(cq, wuq_ref[...], preferred_element_type=F32).astype(BF16)
    for h in range(A_HEADS):
        ql = jnp.dot(q[:, h * A_QK_DIM:(h + 1) * A_QK_DIM], wuk_ref[h],
                     preferred_element_type=F32) * (A_QK_DIM ** -0.5 * LOG2_E)
        qall_ref[h * nq:(h + 1) * nq, :] = ql.astype(BF16)
    qi = jnp.dot(cq, wiq_ref[...], preferred_element_type=F32).astype(BF16)
    for h in range(IDX_HEADS):
        qidx_ref[h * nq:(h + 1) * nq, :] = qi[:, h * IDX_DIM:(h + 1) * IDX_DIM]
    w_idx = small_ref[0][:, OFF_WIDX:OFF_WIDX + IDX_HEADS] * (IDX_HEADS ** -0.5 * IDX_DIM ** -0.5)
    row_pos = i * nq + lax.broadcasted_iota(jnp.int32, (nq, 1), 0)

    def key_rows(c):
        return pl.ds(pl.multiple_of(c * kc, kc), kc)

    def score_chunk(c, _):
        logits = lax.dot_general(qidx_ref[...], kidx_ref[0, key_rows(c), :], nt,
                                 preferred_element_type=F32)
        sc = jnp.zeros((nq, kc), F32)
        for h in range(IDX_HEADS):
            sc = sc + w_idx[:, h:h + 1] * jnp.maximum(logits[h * nq:(h + 1) * nq], 0.0)
        col = c * kc + lax.broadcasted_iota(jnp.int32, (nq, kc), 1)
        score_ref[c] = jnp.where(col <= row_pos, sc, -jnp.inf)
        return 0

    lax.fori_loop(0, n_kc, score_chunk, 0)

    r_i = lax.broadcasted_iota(jnp.int32, (LANES, 2 * LANES), 0)
    c_i = lax.broadcasted_iota(jnp.int32, (LANES, 2 * LANES), 1)
    tri_ones = jnp.where((c_i >= LANES) | (r_i <= c_i), 1.0, 0.0).astype(BF16)
    keep_all = (i + 1) * nq <= topk

    @pl.when(keep_all)
    def _():
        bias_ref[0] = jnp.where(score_ref[0] > -jnp.inf, 0.0, NEG_BIG)

    for v in range(max_chunks):
        pl.when(jnp.logical_and(n_kc == v + 1, jnp.logical_not(keep_all)))(functools.partial(
            _topk_bias, score_ref, bias_ref, i * nq, v + 1, topk, tri_ones))

    m_ref[...] = jnp.full(m_ref.shape, -jnp.inf, F32)
    l_ref[...] = jnp.zeros_like(l_ref)
    acc_ref[...] = jnp.zeros_like(acc_ref)

    def pass1(c, _):
        ckv = ckv_ref[0, key_rows(c), :]
        bias = bias_ref[c]
        ss = [lax.dot_general(qall_ref[g * rows:(g + 1) * rows, :], ckv, nt,
                              preferred_element_type=F32) for g in range(ng)]
        for g in range(ng):
            s = (ss[g].reshape(HEAD_GROUP, nq, kc) + bias[None]).reshape(rows, kc)
            s_ref[c, g] = s
            mt = s[:, :LANES]
            for j in range(1, tiles):
                mt = jnp.maximum(mt, s[:, j * LANES:(j + 1) * LANES])
            m_ref[g] = jnp.maximum(m_ref[g], mt)
        return 0

    lax.fori_loop(0, n_kc, pass1, 0)
    for g in range(ng):
        m_ref[g] = jnp.broadcast_to(jnp.max(m_ref[g], axis=1, keepdims=True), (rows, LANES))

    def pass2(c, _):
        ckv = ckv_ref[0, key_rows(c), :]
        for g in range(ng):
            s = s_ref[c, g]
            m = m_ref[g]
            p = [jnp.exp2(s[:, j * LANES:(j + 1) * LANES] - m) for j in range(tiles)]
            l_ref[g] += functools.reduce(lambda a, b: a + b, p)
            pb = jnp.concatenate(p, axis=1).astype(BF16)
            acc_ref[g] += jnp.dot(pb, ckv, preferred_element_type=F32)
        return 0

    lax.fori_loop(0, n_kc, pass2, 0)
    for g in range(ng):
        l = jnp.sum(l_ref[g], axis=1, keepdims=True)
        og = (acc_ref[g] * (1.0 / l)).astype(BF16)
        for hh in range(HEAD_GROUP):
            h = g * HEAD_GROUP + hh
            o_ref[0, :, h * A_V_DIM:(h + 1) * A_V_DIM] = jnp.dot(
                og[hh * nq:(hh + 1) * nq], wuv_ref[h],
                preferred_element_type=F32).astype(o_ref.dtype)


def _dsa(cq, small, ckv, kidx, w_uq, w_uk, w_uv, w_iq):
    B, L, _ = cq.shape
    topk = min(TOPK_MAX, L // 4)
    kc = min(KEY_CHUNK, L)
    assert L % kc == 0 and kc % Q_BLOCK == 0 and topk <= kc
    n_chunks = L // kc
    nb = L // Q_BLOCK
    ng = A_HEADS // HEAD_GROUP
    rows = HEAD_GROUP * Q_BLOCK
    a_width = A_HEADS * A_V_DIM
    return pl.pallas_call(
        functools.partial(_dsa_kernel, topk=topk),
        grid=(B, nb),
        in_specs=[
            pl.BlockSpec((1, Q_BLOCK, A_Q_RANK), lambda b, i: (b, i, 0)),
            pl.BlockSpec((1, Q_BLOCK, SMALL_W), lambda b, i: (b, i, 0)),
            pl.BlockSpec((1, L, A_KV_RANK), lambda b, i: (b, 0, 0)),
            pl.BlockSpec((1, L, IDX_DIM), lambda b, i: (b, 0, 0)),
            _resident(w_uq.shape), _resident(w_uk.shape), _resident(w_uv.shape),
            _resident(w_iq.shape),
        ],
        out_specs=pl.BlockSpec((1, Q_BLOCK, a_width), lambda b, i: (b, i, 0)),
        out_shape=jax.ShapeDtypeStruct((B, L, a_width), BF16),
        scratch_shapes=[
            pltpu.VMEM((A_HEADS * Q_BLOCK, A_KV_RANK), BF16),
            pltpu.VMEM((IDX_HEADS * Q_BLOCK, IDX_DIM), BF16),
            pltpu.VMEM((n_chunks, Q_BLOCK, kc), F32),
            pltpu.VMEM((n_chunks, Q_BLOCK, kc), F32),
            pltpu.VMEM((n_chunks, ng, rows, kc), F32),
            pltpu.VMEM((ng, rows, LANES), F32),
            pltpu.VMEM((ng, rows, LANES), F32),
            pltpu.VMEM((ng, rows, A_KV_RANK), F32),
        ],
        compiler_params=pltpu.CompilerParams(
            dimension_semantics=("arbitrary", "arbitrary"), vmem_limit_bytes=VMEM_LIMIT),
        name="dsa",
    )(cq, small, ckv, kidx, w_uq, w_uk, w_uv, w_iq)


def _lane_bcast(x, c):
    lane = lax.broadcasted_iota(jnp.int32, x.shape, 1)
    col = jnp.sum(jnp.where(lane == c, x, 0.0), axis=1, keepdims=True)
    return jnp.broadcast_to(col, x.shape)


def _gdn_kernel(q_ref, k_ref, v_ref, z_ref, gates_ref, og_ref, o_ref,
                mneg_ref, r_ref, qeff_ref, o0_ref, at_ref, oraw_ref, *, seq, hp):
    hg = pl.program_id(1)
    C = CHUNK
    n_chunks = seq // C
    dk = B_K_DIM

    r64 = lax.broadcasted_iota(jnp.int32, (C, C), 0)
    c64 = lax.broadcasted_iota(jnp.int32, (C, C), 1)
    tri = r64 >= c64
    strict = r64 > c64
    eye = jnp.where(r64 == c64, 1.0, 0.0)
    nt = (((1,), (1,)), ((), ()))
    dot = functools.partial(jnp.dot, preferred_element_type=F32)

    def prep(t, _):
        ids = [(hh, t * PREP_GROUP + j) for j in range(PREP_GROUP) for hh in range(hp)]
        each = lambda f, *cols: [f(*a) for a in zip(*cols)]
        rows = [pl.ds(pl.multiple_of(n * C, C), C) for _, n in ids]
        load = lambda ref: [ref[0, r, hh * dk:(hh + 1) * dk] for (hh, _), r in zip(ids, rows)]
        qb, kb, vb = (load(r) for r in (q_ref, k_ref, v_ref))
        q, k, v = (each(lambda a: a.astype(F32), x) for x in (qb, kb, vb))
        gt = [gates_ref[0, r, :] for r in rows]
        beta = [_lane_bcast(a, OFF_BETA + hg * hp + hh) for a, (hh, _) in zip(gt, ids)]
        G = [_lane_bcast(a, OFF_A + hg * hp + hh) for a, (hh, _) in zip(gt, ids)]
        decay = each(lambda g: jnp.exp(jnp.where(tri, g[:, :C] - g.T[:C, :C], -jnp.inf)), G)
        eG = each(jnp.exp, G)
        qkk = each(lambda q_, k_: lax.dot_general(
            jnp.concatenate([q_, k_], axis=0), k_, nt, preferred_element_type=F32), qb, kb)
        N = each(lambda b, a, d: jnp.where(strict, b[:, :C] * a[C:] * d, 0.0), beta, qkk, decay)
        X = each(lambda a: eye - a, N)
        Nb = each(lambda a: a.astype(BF16), N)
        Pb = each(lambda a: dot(a, a).astype(BF16), Nb)
        steps = C.bit_length() - 2
        for it in range(steps):
            if it + 1 < steps:
                xp = each(lambda x, p: dot(jnp.concatenate([x.astype(BF16), p], axis=0), p), X, Pb)
                X = each(lambda x, a: x + a[:C], X, xp)
                Pb = each(lambda a: a[C:].astype(BF16), xp)
            else:
                X = each(lambda x, p: x + dot(x.astype(BF16), p), X, Pb)
        rhs = each(lambda v_, k_, b, e: jnp.concatenate(
            [v_ * b, k_ * (b * e)], axis=-1).astype(BF16), v, k, beta, eG)
        sol = each(lambda x, r: dot(x.astype(BF16), r).astype(BF16), X, rhs)
        qk = each(lambda a, d: (a[:C] * d).astype(BF16), qkk, decay)
        ktT = each(lambda k_, g: (k_ * jnp.exp(g[C - 1:C, :] - g)).T.astype(BF16), k, G)
        kts = each(dot, ktT, sol)
        qks = each(dot, qk, sol)
        for i, (hh, n) in enumerate(ids):
            r_ref[hh, n] = kts[i][:, :B_V_DIM]
            mneg_ref[hh, n] = kts[i][:, B_V_DIM:].astype(BF16)
            o0_ref[hh, n] = qks[i][:, :B_V_DIM]
            qeff_ref[hh, n] = (q[i] * eG[i] - qks[i][:, B_V_DIM:]).astype(BF16)
            at_ref[hh, n] = jnp.exp(G[i][C - 1:C, :])
        return 0

    lax.fori_loop(0, n_chunks // PREP_GROUP, prep, 0)

    def scan(n, S):
        rows = pl.ds(pl.multiple_of(n * C, C), C)
        Sb = [s.astype(BF16) for s in S]
        ms = [dot(mneg_ref[hh, n], Sb[hh]) for hh in range(hp)]
        os_ = [dot(qeff_ref[hh, n], Sb[hh]) for hh in range(hp)]
        new = []
        for hh in range(hp):
            new.append(S[hh] * at_ref[hh, n] - ms[hh] + r_ref[hh, n])
            oraw_ref[hh, rows, :] = os_[hh] + o0_ref[hh, n]
        return tuple(new)

    lax.fori_loop(0, n_chunks, scan, tuple(jnp.zeros((dk, B_V_DIM), F32) for _ in range(hp)))

    for hh in range(hp):
        ls = slice(hh * dk, (hh + 1) * dk)
        z = z_ref[0, :, ls]
        o_ref[0, :, ls] = (_rms(oraw_ref[hh], og_ref[...])
                           * _silu(z)).astype(o_ref.dtype)


def _gdn(qn, kn, vn, zb, gates, onorm_g):
    B, L, W = qn.shape
    hp = GDN_HEADS_PER_STEP
    n_chunks = L // CHUNK
    heads = W // B_K_DIM
    assert heads % hp == 0 and n_chunks % PREP_GROUP == 0
    col = pl.BlockSpec((1, L, hp * B_K_DIM), lambda b, h: (b, 0, h))
    return pl.pallas_call(
        functools.partial(_gdn_kernel, seq=L, hp=hp),
        grid=(B, heads // hp),
        in_specs=[col, col, col, col,
                  pl.BlockSpec((1, L, SMALL_W), lambda b, h: (b, 0, 0)),
                  _resident((1, B_V_DIM))],
        out_specs=col,
        out_shape=jax.ShapeDtypeStruct((B, L, W), BF16),
        scratch_shapes=[
            pltpu.VMEM((hp, n_chunks, B_K_DIM, B_K_DIM), BF16),
            pltpu.VMEM((hp, n_chunks, B_K_DIM, B_V_DIM), F32),
            pltpu.VMEM((hp, n_chunks, CHUNK, B_K_DIM), BF16),
            pltpu.VMEM((hp, n_chunks, CHUNK, B_V_DIM), F32),
            pltpu.VMEM((hp, n_chunks, 1, LANES), F32),
            pltpu.VMEM((hp, L, B_V_DIM), F32),
        ],
        compiler_params=pltpu.CompilerParams(
            dimension_semantics=("arbitrary", "arbitrary"), vmem_limit_bytes=VMEM_LIMIT),
        name="gdn",
    )(qn, kn, vn, zb, gates, onorm_g)


def _merge_kernel(x_ref, oa_ref, ob_ref, ga_ref, gb_ref, wa_ref, wb_ref, wo_ref,
                  fg_ref, wg_ref, wu_ref, wd_ref, og_ref, o_ref, *, ff_chunk, final_norm):
    ya = jnp.dot(oa_ref[...], wa_ref[...], preferred_element_type=F32)
    yb = jnp.dot(ob_ref[...], wb_ref[...], preferred_element_type=F32)
    merged = _sigmoid(ga_ref[...]) * ya + _sigmoid(gb_ref[...]) * yb
    x1 = x_ref[...] + jnp.dot(merged.astype(BF16), wo_ref[...], preferred_element_type=F32)
    h = _rms(x1, fg_ref[...]).astype(BF16)
    d_ff = wg_ref.shape[1]
    acc = x1
    for c0 in range(0, d_ff, ff_chunk):
        gate = jnp.dot(h, wg_ref[:, c0:c0 + ff_chunk], preferred_element_type=F32)
        up = jnp.dot(h, wu_ref[:, c0:c0 + ff_chunk], preferred_element_type=F32)
        act = (_silu(gate) * up).astype(BF16)
        acc = acc + jnp.dot(act, wd_ref[c0:c0 + ff_chunk, :], preferred_element_type=F32)
    o_ref[...] = _rms(acc, og_ref[...]) if final_norm else acc


def _merge(x2, oa, ob, ga, gb, wa, wb, wo, fg, wg, wu, wd, og, *, tm, final_norm):
    T, D = x2.shape
    d_ff = wg.shape[1]
    ff_chunk = d_ff // 2 if (d_ff // 2) % LANES == 0 else d_ff
    row = pl.BlockSpec((tm, D), lambda i: (i, 0))
    return pl.pallas_call(
        functools.partial(_merge_kernel, ff_chunk=ff_chunk, final_norm=final_norm),
        grid=(T // tm,),
        in_specs=[row, row, row, row, row,
                  _resident(wa.shape), _resident(wb.shape), _resident(wo.shape),
                  _resident(fg.shape), _resident(wg.shape), _resident(wu.shape),
                  _resident(wd.shape), _resident(og.shape)],
        out_specs=row,
        out_shape=jax.ShapeDtypeStruct((T, D), F32),
        compiler_params=pltpu.CompilerParams(
            dimension_semantics=("arbitrary",), vmem_limit_bytes=VMEM_LIMIT),
        name="merge_ffn",
    )(x2, oa, ob, ga, gb, wa, wb, wo, fg, wg, wu, wd, og)


def _reorder_w_in(w):
    sizes = (A_Q_RANK, A_KV_RANK, IDX_DIM, IDX_HEADS,
             B_HEADS * B_K_DIM, B_HEADS * B_K_DIM, B_HEADS * B_V_DIM, B_HEADS, B_HEADS,
             B_HEADS * B_V_DIM, w.shape[0], w.shape[0])
    parts, c0 = [], 0
    for s in sizes:
        parts.append(w[:, c0:c0 + s])
        c0 += s
    (c_q, c_kv, k_idx, w_idx, q_b, k_b, v_b, beta_b, a_b, z_b, gate_a, gate_b) = parts
    pad = jnp.zeros((w.shape[0], SMALL_W - IDX_DIM - IDX_HEADS - 2 * B_HEADS), w.dtype)
    return jnp.concatenate(
        [q_b, k_b, v_b, z_b, gate_a, gate_b, c_q, c_kv, k_idx, w_idx, beta_b, a_b, pad],
        axis=1).astype(BF16)


def kernel(x, mix_norm_g, w_in, cq_norm_g, ckv_norm_g, w_uq, w_uk, w_uv, w_iq, kidx_ln_g, kidx_ln_b, w_branch_a, conv_w, a_log, dt_bias, onorm_g, w_branch_b, w_out, ffn_norm_g, w_gate, w_up, w_down, final_norm_g):
    B, L, D = x.shape
    depth = w_in.shape[0]
    T = B * L
    tm = min(256, T)
    x2 = x.reshape(T, D)
    vec = lambda a: a.reshape(1, -1).astype(F32)
    at_a = lambda a: jnp.zeros((1, SMALL_W), F32).at[0, OFF_A:OFF_A + B_HEADS].set(a.astype(F32))
    for l in range(depth):
        (cq, ckv, kidx, small, gates, qn, kn, vn, zb, ga, gb) = _in_proj(
            x2, vec(mix_norm_g[l]), _reorder_w_in(w_in[l]), vec(cq_norm_g[l]),
            vec(ckv_norm_g[l]), vec(kidx_ln_g[l]), vec(kidx_ln_b[l]), conv_w[l].astype(F32),
            at_a(a_log[l]), at_a(dt_bias[l]), tm=tm, seq=L)
        seq = lambda a: a.reshape(B, L, a.shape[-1])
        o_a = _dsa(seq(cq), seq(small), seq(ckv), seq(kidx), w_uq[l].astype(BF16),
                   w_uk[l].astype(BF16), w_uv[l].astype(BF16), w_iq[l].astype(BF16))
        o_b = _gdn(seq(qn), seq(kn), seq(vn), seq(zb), seq(gates), vec(onorm_g[l]))
        x2 = _merge(x2, o_a.reshape(T, -1), o_b.reshape(T, -1), ga, gb,
                    w_branch_a[l].astype(BF16), w_branch_b[l].astype(BF16),
                    w_out[l].astype(BF16), vec(ffn_norm_g[l]), w_gate[l].astype(BF16),
                    w_up[l].astype(BF16), w_down[l].astype(BF16), vec(final_norm_g),
                    tm=tm, final_norm=(l == depth - 1))
    return x2.reshape(B, L, D)
```

```python
import functools

import jax
import jax.numpy as jnp
from jax import lax
from jax.experimental import pallas as pl
from jax.experimental.pallas import tpu as pltpu

F32 = jnp.float32
BF16 = jnp.bfloat16

EPS = 1e-6
A_HEADS = 16
A_QK_DIM = 64
A_V_DIM = 64
A_Q_RANK = 256
A_KV_RANK = 256
IDX_HEADS = 8
IDX_DIM = 64
TOPK_MAX = 256
Q_BLOCK = 128
B_HEADS = 8
B_K_DIM = 128
B_V_DIM = 128
CONV_WIDTH = 4
CHUNK = 64

GDN_W = B_HEADS * B_K_DIM
assert B_K_DIM == B_V_DIM
LANES = 128
SUBLANES = 8
SMALL_W = LANES
OFF_WIDX = IDX_DIM
OFF_BETA = IDX_DIM + IDX_HEADS
OFF_A = IDX_DIM + IDX_HEADS + B_HEADS
VMEM_LIMIT = 56 * 1024 * 1024
NEG_BIG = -1e30
LOG2_E = 1.4426950408889634
BISECT_STEPS = 16
BISECT_UNROLL = 8
HEAD_GROUP = 4
KEY_CHUNK = 512
SEARCH_GROUPS = 4
GDN_HEADS_PER_STEP = 4
PREP_GROUP = 4


def _resident(shape):
    nd = len(shape)
    return pl.BlockSpec(shape, lambda *_: (0,) * nd, pipeline_mode=pl.Buffered(1))


def _rms(x, g):
    return x * lax.rsqrt(jnp.mean(x * x, axis=-1, keepdims=True) + EPS) * g


def _l2n(x, scale=1.0):
    return x * (lax.rsqrt(jnp.sum(x * x, axis=-1, keepdims=True) + EPS) * scale)


def _chunk_cumsum(g, chunk):
    pos = lax.broadcasted_iota(jnp.int32, g.shape, 0) % chunk
    d = 1
    while d < chunk:
        g = g + jnp.where(pos >= d, pltpu.roll(g, d, 0), 0.0)
        d *= 2
    return g


def _sigmoid(x):
    return 0.5 * jnp.tanh(0.5 * x) + 0.5


def _silu(x):
    u = 0.5 * x
    return u + u * jnp.tanh(u)


def _conv_silu_tile(x, tail_ref, w_half):
    full = jnp.concatenate([tail_ref[...], x], axis=0)
    u = x * w_half[CONV_WIDTH - 1:CONV_WIDTH, :]
    for d in range(1, CONV_WIDTH):
        u = u + pltpu.roll(full, d, 0)[SUBLANES:] * w_half[CONV_WIDTH - 1 - d:CONV_WIDTH - d, :]
    tail_ref[...] = x[x.shape[0] - SUBLANES:]
    return u + u * jnp.tanh(u)


def _in_proj_kernel(x_ref, g_ref, w_ref, cqg_ref, ckvg_ref, lng_ref, lnb_ref, cw_ref,
                    alog_ref, dtb_ref,
                    cq_ref, ckv_ref, kidx_ref, small_ref, gates_ref, q_ref, k_ref, v_ref,
                    z_ref, ga_ref, gb_ref, tail_ref, *, d_model, tiles_per_seq):
    @pl.when(pl.program_id(0) % tiles_per_seq == 0)
    def _():
        tail_ref[...] = jnp.zeros_like(tail_ref)

    h = _rms(x_ref[...], g_ref[...]).astype(BF16)

    def proj(c0, width):
        return jnp.dot(h, w_ref[:, c0:c0 + width], preferred_element_type=F32)

    pair = 2 * B_K_DIM
    plain, c0 = [], 3 * GDN_W
    for ref, width in ((z_ref, GDN_W), (ga_ref, d_model), (gb_ref, d_model)):
        plain += [(ref, c, c0 + c) for c in range(0, width, pair)]
        c0 += width
    steps = [(part, c) for part in range(3) for c in range(0, GDN_W, pair)]
    for idx, (part, c) in enumerate(steps):
        ref = (q_ref, k_ref, v_ref)[part]
        cc = part * GDN_W + c
        y = _conv_silu_tile(proj(cc, pair), tail_ref.at[:, cc:cc + pair],
                            cw_ref[:, cc:cc + pair] * 0.5)
        if part < 2:
            scale = B_K_DIM ** -0.5 if part == 0 else 1.0
            y = jnp.concatenate([_l2n(y[:, :B_K_DIM], scale), _l2n(y[:, B_K_DIM:], scale)], axis=1)
        ref[:, c:c + pair] = y.astype(ref.dtype)
        lo, hi = idx * len(plain) // len(steps), (idx + 1) * len(plain) // len(steps)
        for pref, pc, wc in plain[lo:hi]:
            pref[:, pc:pc + pair] = proj(wc, pair)
    cq_ref[...] = _rms(proj(c0, A_Q_RANK), cqg_ref[...]).astype(BF16)
    c0 += A_Q_RANK
    ckv_ref[...] = _rms(proj(c0, A_KV_RANK), ckvg_ref[...]).astype(BF16)
    c0 += A_KV_RANK
    small = proj(c0, SMALL_W)
    small_ref[...] = small
    kraw = small[:, :IDX_DIM]
    mu = jnp.mean(kraw, axis=-1, keepdims=True)
    kc = kraw - mu
    kn = kc * lax.rsqrt(jnp.mean(kc * kc, axis=-1, keepdims=True) + EPS)
    kidx_ref[...] = (kn * lng_ref[...] + lnb_ref[...]).astype(BF16)
    g = -jnp.exp(alog_ref[...]) * jax.nn.softplus(small + dtb_ref[...])
    lane = lax.broadcasted_iota(jnp.int32, small.shape, 1)
    gates_ref[...] = jnp.where(lane >= OFF_A, _chunk_cumsum(g, CHUNK), _sigmoid(small))


def _in_proj(x2, g, w_all, cq_g, ckv_g, ln_g, ln_b, conv_w, alog_v, dtb_v, *, tm, seq):
    T, D = x2.shape
    n_all = w_all.shape[1]
    assert tm % CHUNK == 0 and seq % tm == 0
    row = lambda w: pl.BlockSpec((tm, w), lambda i: (i, 0))
    wide = ((GDN_W, BF16), (GDN_W, BF16), (GDN_W, BF16), (GDN_W, F32),
            (D, F32), (D, F32))
    out_shapes = (
        jax.ShapeDtypeStruct((T, A_Q_RANK), BF16),
        jax.ShapeDtypeStruct((T, A_KV_RANK), BF16),
        jax.ShapeDtypeStruct((T, IDX_DIM), BF16),
        jax.ShapeDtypeStruct((T, SMALL_W), F32),
        jax.ShapeDtypeStruct((T, SMALL_W), F32),
    ) + tuple(jax.ShapeDtypeStruct((T, w), dt) for w, dt in wide)
    out_specs = (row(A_Q_RANK), row(A_KV_RANK), row(IDX_DIM), row(SMALL_W), row(SMALL_W)
                 ) + tuple(row(w) for w, _ in wide)
    return pl.pallas_call(
        functools.partial(_in_proj_kernel, d_model=D, tiles_per_seq=seq // tm),
        grid=(T // tm,),
        in_specs=[row(D), _resident((1, D)), _resident((D, n_all)),
                  _resident((1, A_Q_RANK)), _resident((1, A_KV_RANK)),
                  _resident((1, IDX_DIM)), _resident((1, IDX_DIM)),
                  _resident(conv_w.shape), _resident((1, SMALL_W)), _resident((1, SMALL_W))],
        out_specs=out_specs,
        out_shape=out_shapes,
        scratch_shapes=[pltpu.VMEM((SUBLANES, conv_w.shape[1]), F32)],
        compiler_params=pltpu.CompilerParams(
            dimension_semantics=("arbitrary",), vmem_limit_bytes=VMEM_LIMIT),
        name="in_proj",
    )(x2, g, w_all, cq_g, ckv_g, ln_g, ln_b, conv_w, alog_v, dtb_v)


def _topk_bias(score_ref, bias_ref, row0, n_chunks, topk, tri_ones):
    _, nq, kc = score_ref.shape
    rg = nq // SEARCH_GROUPS
    groups = range(SEARCH_GROUPS)
    rsum = lambda x: jnp.sum(x, axis=1, keepdims=True)
    rmin = lambda x: jnp.min(x, axis=1, keepdims=True)

    def ld(g):
        return jnp.concatenate(
            [score_ref[c, g * rg:(g + 1) * rg, :] for c in range(n_chunks)], axis=1)

    pos = lambda g: row0 + g * rg + lax.broadcasted_iota(jnp.int32, (rg, 1), 0)
    k_eff = [jnp.minimum(pos(g) + 1, topk).astype(F32) for g in groups]
    lo = tuple(rmin(jnp.where(ld(g) == -jnp.inf, jnp.inf, ld(g))) for g in groups)
    hi = tuple(jnp.max(ld(g), axis=1, keepdims=True) for g in groups)

    def bisect(_, carry):
        lo, hi = carry
        mid = [0.5 * (a + b) for a, b in zip(lo, hi)]
        cnt = [rsum(jnp.where(ld(g) >= mid[g], 1.0, 0.0)) for g in groups]
        ge = [cnt[g] >= k_eff[g] for g in groups]
        return (tuple(jnp.where(ge[g], mid[g], lo[g]) for g in groups),
                tuple(jnp.where(ge[g], hi[g], mid[g]) for g in groups))

    lo, hi = lax.fori_loop(0, BISECT_STEPS, bisect, (lo, hi), unroll=BISECT_UNROLL)

    def above(g, v):
        s = ld(g)
        gt = s > v
        return rsum(jnp.where(gt, 1.0, 0.0)), rmin(jnp.where(gt, s, jnp.inf))

    v0 = tuple(rmin(jnp.where(ld(g) >= lo[g], ld(g), jnp.inf)) for g in groups)
    first = [above(g, v0[g]) for g in groups]

    def cond(c):
        _, n_gt, _ = c
        flags = [jnp.max(jnp.where(n_gt[g] >= k_eff[g], 1.0, 0.0)) for g in groups]
        return functools.reduce(jnp.maximum, flags) > 0.0

    def body(c):
        v, n_gt, nxt = c
        v = tuple(jnp.where(n_gt[g] >= k_eff[g], nxt[g], v[g]) for g in groups)
        nxt_state = [above(g, v[g]) for g in groups]
        return v, tuple(a for a, _ in nxt_state), tuple(b for _, b in nxt_state)

    tau, n_gt, _ = lax.while_loop(
        cond, body, (v0, tuple(a for a, _ in first), tuple(b for _, b in first)))
    k_all = jnp.concatenate(k_eff, axis=0)
    need = k_all - jnp.concatenate(n_gt, axis=0)
    tau = jnp.concatenate(tau, axis=0)

    n_ge = jnp.zeros((nq, LANES), F32)
    for c in range(n_chunks):
        for j in range(kc // LANES):
            ge = score_ref[c, :, j * LANES:(j + 1) * LANES] >= tau
            bias_ref[c, :, j * LANES:(j + 1) * LANES] = jnp.where(ge, 0.0, NEG_BIG)
            n_ge = n_ge + jnp.where(ge, 1.0, 0.0)
    surplus = jnp.max(jnp.where(rsum(n_ge) > k_all, 1.0, 0.0)) > 0.0

    @pl.when(surplus)
    def _():
        carry = jnp.zeros((nq, LANES), F32)
        for c in range(n_chunks):
            for j in range(kc // LANES):
                sc = score_ref[c, :, j * LANES:(j + 1) * LANES]
                eq = sc == tau
                r = jnp.dot(jnp.where(eq, 1.0, 0.0).astype(BF16), tri_ones,
                            preferred_element_type=F32)
                prefix = r[:, :LANES] + carry
                carry = carry + r[:, LANES:]
                sel = (sc > tau) | (eq & (prefix <= need))
                bias_ref[c, :, j * LANES:(j + 1) * LANES] = jnp.where(sel, 0.0, NEG_BIG)


def _dsa_kernel(cq_ref, small_ref, ckv_ref, kidx_ref, wuq_ref, wuk_ref, wuv_ref, wiq_ref,
                o_ref, qall_ref, qidx_ref, score_ref, bias_ref, s_ref, m_ref, l_ref, acc_ref,
                *, topk):
    i = pl.program_id(1)
    nq = Q_BLOCK
    max_chunks, _, kc = score_ref.shape
    n_kc = (i * nq) // kc + 1
    ng = A_HEADS // HEAD_GROUP
    rows = HEAD_GROUP * nq
    tiles = kc // LANES
    nt = (((1,), (1,)), ((), ()))

    cq = cq_ref[0]
    q = jnp.dot(cq, wuq_ref[...], preferred_element_type=F32).astype(BF16)
    for h in range(A_HEADS):
        ql = jnp.dot(q[:, h * A_QK_DIM:(h + 1) * A_QK_DIM], wuk_ref[h],
                     preferred_element_type=F32) * (A_QK_DIM ** -0.5 * LOG2_E)
        qall_ref[h * nq:(h + 1) * nq, :] = ql.astype(BF16)
    qi = jnp.dot(cq, wiq_ref[...], preferred_element_type=F32).astype(BF16)
    for h in range(IDX_HEADS):
        qidx_ref[h * nq:(h + 1) * nq, :] = qi[:, h * IDX_DIM:(h + 1) * IDX_DIM]
    w_idx = small_ref[0][:, OFF_WIDX:OFF_WIDX + IDX_HEADS] * (IDX_HEADS ** -0.5 * IDX_DIM ** -0.5)
    row_pos = i * nq + lax.broadcasted_iota(jnp.int32, (nq, 1), 0)

    r_i = lax.broadcasted_iota(jnp.int32, (LANES, 2 * LANES), 0)
    c_i = lax.broadcasted_iota(jnp.int32, (LANES, 2 * LANES), 1)
    tri_ones = jnp.where((c_i >= LANES) | (r_i <= c_i), 1.0, 0.0).astype(BF16)

    def key_rows(c):
        return slice(c * kc, (c + 1) * kc)

    def score_chunk(c):
        logits = lax.dot_general(qidx_ref[...], kidx_ref[0, key_rows(c), :], nt,
                                 preferred_element_type=F32)
        sc = jnp.zeros((nq, kc), F32)
        for h in range(IDX_HEADS):
            sc = sc + w_idx[:, h:h + 1] * jnp.maximum(logits[h * nq:(h + 1) * nq], 0.0)
        col = c * kc + lax.broadcasted_iota(jnp.int32, (nq, kc), 1)
        score_ref[c] = jnp.where(col <= row_pos, sc, -jnp.inf)

    def pass1(c):
        ckv = ckv_ref[0, key_rows(c), :]
        bias = bias_ref[c]
        ss = [lax.dot_general(qall_ref[g * rows:(g + 1) * rows, :], ckv, nt,
                              preferred_element_type=F32) for g in range(ng)]
        for g in range(ng):
            s = (ss[g].reshape(HEAD_GROUP, nq, kc) + bias[None]).reshape(rows, kc)
            s_ref[c, g] = s
            mt = s[:, :LANES]
            for j in range(1, tiles):
                mt = jnp.maximum(mt, s[:, j * LANES:(j + 1) * LANES])
            m_ref[g] = mt if c == 0 else jnp.maximum(m_ref[g], mt)

    def pass2(c):
        ckv = ckv_ref[0, key_rows(c), :]
        for g in range(ng):
            s = s_ref[c, g]
            m = m_ref[g]
            p = [jnp.exp2(s[:, j * LANES:(j + 1) * LANES] - m) for j in range(tiles)]
            lsum = functools.reduce(lambda a, b: a + b, p)
            pv = jnp.dot(jnp.concatenate(p, axis=1).astype(BF16), ckv,
                         preferred_element_type=F32)
            l_ref[g] = lsum if c == 0 else l_ref[g] + lsum
            acc_ref[g] = pv if c == 0 else acc_ref[g] + pv

    def block(nk, search):
        for c in range(nk):
            score_chunk(c)
        if search:
            _topk_bias(score_ref, bias_ref, i * nq, nk, topk, tri_ones)
        else:
            bias_ref[0] = jnp.where(score_ref[0] > -jnp.inf, 0.0, NEG_BIG)
        for c in range(nk):
            pass1(c)
        for g in range(ng):
            m_ref[g] = jnp.broadcast_to(jnp.max(m_ref[g], axis=1, keepdims=True), (rows, LANES))
        for c in range(nk):
            pass2(c)
        for g in range(ng):
            l = jnp.sum(l_ref[g], axis=1, keepdims=True)
            og = (acc_ref[g] * (1.0 / l)).astype(BF16)
            for hh in range(HEAD_GROUP):
                h = g * HEAD_GROUP + hh
                o_ref[0, :, h * A_V_DIM:(h + 1) * A_V_DIM] = jnp.dot(
                    og[hh * nq:(hh + 1) * nq], wuv_ref[h],
                    preferred_element_type=F32).astype(o_ref.dtype)

    keep_all = (i + 1) * nq <= topk
    pl.when(keep_all)(functools.partial(block, 1, False))
    for v in range(max_chunks):
        pl.when(jnp.logical_and(n_kc == v + 1, jnp.logical_not(keep_all)))(
            functools.partial(block, v + 1, True))


def _dsa(cq, small, ckv, kidx, w_uq, w_uk, w_uv, w_iq):
    B, L, _ = cq.shape
    topk = min(TOPK_MAX, L // 4)
    kc = min(KEY_CHUNK, L)
    assert L % kc == 0 and kc % Q_BLOCK == 0 and topk <= kc
    n_chunks = L // kc
    nb = L // Q_BLOCK
    ng = A_HEADS // HEAD_GROUP
    rows = HEAD_GROUP * Q_BLOCK
    a_width = A_HEADS * A_V_DIM
    return pl.pallas_call(
        functools.partial(_dsa_kernel, topk=topk),
        grid=(B, nb),
        in_specs=[
            pl.BlockSpec((1, Q_BLOCK, A_Q_RANK), lambda b, i: (b, i, 0)),
            pl.BlockSpec((1, Q_BLOCK, SMALL_W), lambda b, i: (b, i, 0)),
            pl.BlockSpec((1, L, A_KV_RANK), lambda b, i: (b, 0, 0)),
            pl.BlockSpec((1, L, IDX_DIM), lambda b, i: (b, 0, 0)),
            _resident(w_uq.shape), _resident(w_uk.shape), _resident(w_uv.shape),
            _resident(w_iq.shape),
        ],
        out_specs=pl.BlockSpec((1, Q_BLOCK, a_width), lambda b, i: (b, i, 0)),
        out_shape=jax.ShapeDtypeStruct((B, L, a_width), BF16),
        scratch_shapes=[
            pltpu.VMEM((A_HEADS * Q_BLOCK, A_KV_RANK), BF16),
            pltpu.VMEM((IDX_HEADS * Q_BLOCK, IDX_DIM), BF16),
            pltpu.VMEM((n_chunks, Q_BLOCK, kc), F32),
            pltpu.VMEM((n_chunks, Q_BLOCK, kc), F32),
            pltpu.VMEM((n_chunks, ng, rows, kc), F32),
            pltpu.VMEM((ng, rows, LANES), F32),
            pltpu.VMEM((ng, rows, LANES), F32),
            pltpu.VMEM((ng, rows, A_KV_RANK), F32),
        ],
        compiler_params=pltpu.CompilerParams(
            dimension_semantics=("arbitrary", "arbitrary"), vmem_limit_bytes=VMEM_LIMIT),
        name="dsa",
    )(cq, small, ckv, kidx, w_uq, w_uk, w_uv, w_iq)


def _lane_bcast(x, c):
    lane = lax.broadcasted_iota(jnp.int32, x.shape, 1)
    col = jnp.sum(jnp.where(lane == c, x, 0.0), axis=1, keepdims=True)
    return jnp.broadcast_to(col, x.shape)


def _gdn_kernel(q_ref, k_ref, v_ref, z_ref, gates_ref, og_ref, o_ref,
                mneg_ref, r_ref, qeff_ref, o0_ref, at_ref, oraw_ref, *, seq, hp):
    hg = pl.program_id(1)
    C = CHUNK
    n_chunks = seq // C
    dk = B_K_DIM

    r64 = lax.broadcasted_iota(jnp.int32, (C, C), 0)
    c64 = lax.broadcasted_iota(jnp.int32, (C, C), 1)
    tri = r64 >= c64
    strict = r64 > c64
    eye = jnp.where(r64 == c64, 1.0, 0.0)
    nt = (((1,), (1,)), ((), ()))
    dot = functools.partial(jnp.dot, preferred_element_type=F32)

    def prep(t, _):
        ids = [(hh, t * PREP_GROUP + j) for j in range(PREP_GROUP) for hh in range(hp)]
        each = lambda f, *cols: [f(*a) for a in zip(*cols)]
        rows = [pl.ds(pl.multiple_of(n * C, C), C) for _, n in ids]
        load = lambda ref: [ref[0, r, hh * dk:(hh + 1) * dk] for (hh, _), r in zip(ids, rows)]
        qb, kb, vb = (load(r) for r in (q_ref, k_ref, v_ref))
        q, k, v = (each(lambda a: a.astype(F32), x) for x in (qb, kb, vb))
        gt = [gates_ref[0, r, :] for r in rows]
        beta = [_lane_bcast(a, OFF_BETA + hg * hp + hh) for a, (hh, _) in zip(gt, ids)]
        G = [_lane_bcast(a, OFF_A + hg * hp + hh) for a, (hh, _) in zip(gt, ids)]
        decay = each(lambda g: jnp.exp(jnp.where(tri, g[:, :C] - g.T[:C, :C], -jnp.inf)), G)
        eG = each(jnp.exp, G)
        qkk = each(lambda q_, k_: lax.dot_general(
            jnp.concatenate([q_, k_], axis=0), k_, nt, preferred_element_type=F32), qb, kb)
        N = each(lambda b, a, d: jnp.where(strict, b[:, :C] * a[C:] * d, 0.0), beta, qkk, decay)
        X = each(lambda a: eye - a, N)
        Nb = each(lambda a: a.astype(BF16), N)
        Pb = each(lambda a: dot(a, a).astype(BF16), Nb)
        steps = C.bit_length() - 2
        for it in range(steps):
            if it + 1 < steps:
                xp = each(lambda x, p: dot(jnp.concatenate([x.astype(BF16), p], axis=0), p), X, Pb)
                X = each(lambda x, a: x + a[:C], X, xp)
                Pb = each(lambda a: a[C:].astype(BF16), xp)
            else:
                X = each(lambda x, p: x + dot(x.astype(BF16), p), X, Pb)
        rhs = each(lambda v_, k_, b, e: jnp.concatenate(
            [v_ * b, k_ * (b * e)], axis=-1).astype(BF16), v, k, beta, eG)
        sol = each(lambda x, r: dot(x.astype(BF16), r).astype(BF16), X, rhs)
        qk = each(lambda a, d: (a[:C] * d).astype(BF16), qkk, decay)
        ktT = each(lambda k_, g: (k_ * jnp.exp(g[C - 1:C, :] - g)).T.astype(BF16), k, G)
        kts = each(dot, ktT, sol)
        qks = each(dot, qk, sol)
        for i, (hh, n) in enumerate(ids):
            r_ref[hh, n] = kts[i][:, :B_V_DIM]
            mneg_ref[hh, n] = kts[i][:, B_V_DIM:].astype(BF16)
            o0_ref[hh, n] = qks[i][:, :B_V_DIM]
            qeff_ref[hh, n] = (q[i] * eG[i] - qks[i][:, B_V_DIM:]).astype(BF16)
            at_ref[hh, n] = jnp.exp(G[i][C - 1:C, :])
        return 0

    lax.fori_loop(0, n_chunks // PREP_GROUP, prep, 0)

    def scan(n, S):
        rows = pl.ds(pl.multiple_of(n * C, C), C)
        Sb = [s.astype(BF16) for s in S]
        ms = [dot(mneg_ref[hh, n], Sb[hh]) for hh in range(hp)]
        os_ = [dot(qeff_ref[hh, n], Sb[hh]) for hh in range(hp)]
        new = []
        for hh in range(hp):
            new.append(S[hh] * at_ref[hh, n] - ms[hh] + r_ref[hh, n])
            oraw_ref[hh, rows, :] = os_[hh] + o0_ref[hh, n]
        return tuple(new)

    lax.fori_loop(0, n_chunks, scan, tuple(jnp.zeros((dk, B_V_DIM), F32) for _ in range(hp)))

    for hh in range(hp):
        ls = slice(hh * dk, (hh + 1) * dk)
        z = z_ref[0, :, ls]
        o_ref[0, :, ls] = (_rms(oraw_ref[hh], og_ref[...])
                           * _silu(z)).astype(o_ref.dtype)


def _gdn(qn, kn, vn, zb, gates, onorm_g):
    B, L, W = qn.shape
    hp = GDN_HEADS_PER_STEP
    n_chunks = L // CHUNK
    heads = W // B_K_DIM
    assert heads % hp == 0 and n_chunks % PREP_GROUP == 0
    col = pl.BlockSpec((1, L, hp * B_K_DIM), lambda b, h: (b, 0, h))
    return pl.pallas_call(
        functools.partial(_gdn_kernel, seq=L, hp=hp),
        grid=(B, heads // hp),
        in_specs=[col, col, col, col,
                  pl.BlockSpec((1, L, SMALL_W), lambda b, h: (b, 0, 0)),
                  _resident((1, B_V_DIM))],
        out_specs=col,
        out_shape=jax.ShapeDtypeStruct((B, L, W), BF16),
        scratch_shapes=[
            pltpu.VMEM((hp, n_chunks, B_K_DIM, B_K_DIM), BF16),
            pltpu.VMEM((hp, n_chunks, B_K_DIM, B_V_DIM), F32),
            pltpu.VMEM((hp, n_chunks, CHUNK, B_K_DIM), BF16),
            pltpu.VMEM((hp, n_chunks, CHUNK, B_V_DIM), F32),
            pltpu.VMEM((hp, n_chunks, 1, LANES), F32),
            pltpu.VMEM((hp, L, B_V_DIM), F32),
        ],
        compiler_params=pltpu.CompilerParams(
            dimension_semantics=("arbitrary", "arbitrary"), vmem_limit_bytes=VMEM_LIMIT),
        name="gdn",
    )(qn, kn, vn, zb, gates, onorm_g)


def _merge_kernel(x_ref, oa_ref, ob_ref, ga_ref, gb_ref, wa_ref, wb_ref, wo_ref,
                  fg_ref, wg_ref, wu_ref, wd_ref, og_ref, o_ref, *, ff_chunk, final_norm):
    ya = jnp.dot(oa_ref[...], wa_ref[...], preferred_element_type=F32)
    yb = jnp.dot(ob_ref[...], wb_ref[...], preferred_element_type=F32)
    merged = _sigmoid(ga_ref[...]) * ya + _sigmoid(gb_ref[...]) * yb
    x1 = x_ref[...] + jnp.dot(merged.astype(BF16), wo_ref[...], preferred_element_type=F32)
    h = _rms(x1, fg_ref[...]).astype(BF16)
    d_ff = wg_ref.shape[1]
    acc = x1
    for c0 in range(0, d_ff, ff_chunk):
        gate = jnp.dot(h, wg_ref[:, c0:c0 + ff_chunk], preferred_element_type=F32)
        up = jnp.dot(h, wu_ref[:, c0:c0 + ff_chunk], preferred_element_type=F32)
        act = (_silu(gate) * up).astype(BF16)
        acc = acc + jnp.dot(act, wd_ref[c0:c0 + ff_chunk, :], preferred_element_type=F32)
    o_ref[...] = _rms(acc, og_ref[...]) if final_norm else acc


def _merge(x2, oa, ob, ga, gb, wa, wb, wo, fg, wg, wu, wd, og, *, tm, final_norm):
    T, D = x2.shape
    d_ff = wg.shape[1]
    ff_chunk = d_ff // 2 if (d_ff // 2) % LANES == 0 else d_ff
    row = pl.BlockSpec((tm, D), lambda i: (i, 0))
    return pl.pallas_call(
        functools.partial(_merge_kernel, ff_chunk=ff_chunk, final_norm=final_norm),
        grid=(T // tm,),
        in_specs=[row, row, row, row, row,
                  _resident(wa.shape), _resident(wb.shape), _resident(wo.shape),
                  _resident(fg.shape), _resident(wg.shape), _resident(wu.shape),
                  _resident(wd.shape), _resident(og.shape)],
        out_specs=row,
        out_shape=jax.ShapeDtypeStruct((T, D), F32),
        compiler_params=pltpu.CompilerParams(
            dimension_semantics=("arbitrary",), vmem_limit_bytes=VMEM_LIMIT),
        name="merge_ffn",
    )(x2, oa, ob, ga, gb, wa, wb, wo, fg, wg, wu, wd, og)


def _reorder_w_in(w):
    sizes = (A_Q_RANK, A_KV_RANK, IDX_DIM, IDX_HEADS,
             B_HEADS * B_K_DIM, B_HEADS * B_K_DIM, B_HEADS * B_V_DIM, B_HEADS, B_HEADS,
             B_HEADS * B_V_DIM, w.shape[0], w.shape[0])
    parts, c0 = [], 0
    for s in sizes:
        parts.append(w[:, c0:c0 + s])
        c0 += s
    (c_q, c_kv, k_idx, w_idx, q_b, k_b, v_b, beta_b, a_b, z_b, gate_a, gate_b) = parts
    pad = jnp.zeros((w.shape[0], SMALL_W - IDX_DIM - IDX_HEADS - 2 * B_HEADS), w.dtype)
    return jnp.concatenate(
        [q_b, k_b, v_b, z_b, gate_a, gate_b, c_q, c_kv, k_idx, w_idx, beta_b, a_b, pad],
        axis=1).astype(BF16)


def kernel(x, mix_norm_g, w_in, cq_norm_g, ckv_norm_g, w_uq, w_uk, w_uv, w_iq, kidx_ln_g, kidx_ln_b, w_branch_a, conv_w, a_log, dt_bias, onorm_g, w_branch_b, w_out, ffn_norm_g, w_gate, w_up, w_down, final_norm_g):
    B, L, D = x.shape
    depth = w_in.shape[0]
    T = B * L
    tm = min(256, T)
    x2 = x.reshape(T, D)
    vec = lambda a: a.reshape(1, -1).astype(F32)
    at_a = lambda a: jnp.zeros((1, SMALL_W), F32).at[0, OFF_A:OFF_A + B_HEADS].set(a.astype(F32))
    for l in range(depth):
        (cq, ckv, kidx, small, gates, qn, kn, vn, zb, ga, gb) = _in_proj(
            x2, vec(mix_norm_g[l]), _reorder_w_in(w_in[l]), vec(cq_norm_g[l]),
            vec(ckv_norm_g[l]), vec(kidx_ln_g[l]), vec(kidx_ln_b[l]), conv_w[l].astype(F32),
            at_a(a_log[l]), at_a(dt_bias[l]), tm=tm, seq=L)
        seq = lambda a: a.reshape(B, L, a.shape[-1])
        o_a = _dsa(seq(cq), seq(small), seq(ckv), seq(kidx), w_uq[l].astype(BF16),
                   w_uk[l].astype(BF16), w_uv[l].astype(BF16), w_iq[l].astype(BF16))
        o_b = _gdn(seq(qn), seq(kn), seq(vn), seq(zb), seq(gates), vec(onorm_g[l]))
        x2 = _merge(x2, o_a.reshape(T, -1), o_b.reshape(T, -1), ga, gb,
                    w_branch_a[l].astype(BF16), w_branch_b[l].astype(BF16),
                    w_out[l].astype(BF16), vec(ffn_norm_g[l]), w_gate[l].astype(BF16),
                    w_up[l].astype(BF16), w_down[l].astype(BF16), vec(final_norm_g),
                    tm=tm, final_norm=(l == depth - 1))
    return x2.reshape(B, L, D)
```

```python
import functools

import jax
import jax.numpy as jnp
from jax import lax
from jax.experimental import pallas as pl
from jax.experimental.pallas import tpu as pltpu

F32 = jnp.float32
BF16 = jnp.bfloat16

EPS = 1e-6
A_HEADS = 16
A_QK_DIM = 64
A_V_DIM = 64
A_Q_RANK = 256
A_KV_RANK = 256
IDX_HEADS = 8
IDX_DIM = 64
TOPK_MAX = 256
Q_BLOCK = 128
B_HEADS = 8
B_K_DIM = 128
B_V_DIM = 128
CONV_WIDTH = 4
CHUNK = 64

GDN_W = B_HEADS * B_K_DIM
assert B_K_DIM == B_V_DIM
LANES = 128
SUBLANES = 8
SMALL_W = LANES
OFF_WIDX = IDX_DIM
OFF_BETA = IDX_DIM + IDX_HEADS
OFF_A = IDX_DIM + IDX_HEADS + B_HEADS
VMEM_LIMIT = 56 * 1024 * 1024
NEG_BIG = -1e30
LOG2_E = 1.4426950408889634
BISECT_STEPS = 16
BISECT_UNROLL = 8
TOKEN_TILE = 256
HEAD_GROUP = 4
KEY_CHUNK = 512
SEARCH_GROUPS = 4
GDN_HEADS_PER_STEP = 4
PACK = 2
PREP_GROUP = 4


def _resident(shape):
    nd = len(shape)
    return pl.BlockSpec(shape, lambda *_: (0,) * nd, pipeline_mode=pl.Buffered(1))


def _rms(x, g):
    return x * lax.rsqrt(jnp.mean(x * x, axis=-1, keepdims=True) + EPS) * g


def _l2n(x, scale=1.0):
    return x * (lax.rsqrt(jnp.sum(x * x, axis=-1, keepdims=True) + EPS) * scale)


def _sigmoid(x):
    return 0.5 * jnp.tanh(0.5 * x) + 0.5


def _silu(x):
    u = 0.5 * x
    return u + u * jnp.tanh(u)


def _chunk_cumsum(g, chunk):
    pos = lax.broadcasted_iota(jnp.int32, g.shape, 0) % chunk
    d = 1
    while d < chunk:
        g = g + jnp.where(pos >= d, pltpu.roll(g, d, 0), 0.0)
        d *= 2
    return g


def _conv_silu_tile(x, tail_ref, w_half):
    full = jnp.concatenate([tail_ref[...], x], axis=0)
    u = x * w_half[CONV_WIDTH - 1:CONV_WIDTH, :]
    for d in range(1, CONV_WIDTH):
        u = u + pltpu.roll(full, d, 0)[SUBLANES:] * w_half[CONV_WIDTH - 1 - d:CONV_WIDTH - d, :]
    tail_ref[...] = x[x.shape[0] - SUBLANES:]
    return u + u * jnp.tanh(u)


def _in_proj_kernel(x_ref, g_ref, w_ref, cqg_ref, ckvg_ref, lng_ref, lnb_ref, cw_ref,
                    alog_ref, dtb_ref, wuq_ref, wuk_ref, wiq_ref,
                    qlat_ref, qidx_ref, ckv_ref, kidx_ref, small_ref, gates_ref, q_ref, k_ref, v_ref,
                    z_ref, ga_ref, gb_ref, tail_ref, *, d_model, tiles_per_seq):
    @pl.when(pl.program_id(0) % tiles_per_seq == 0)
    def _():
        tail_ref[...] = jnp.zeros_like(tail_ref)

    h = _rms(x_ref[...], g_ref[...]).astype(BF16)

    def proj(c0, width):
        return jnp.dot(h, w_ref[:, c0:c0 + width], preferred_element_type=F32)

    pair = 2 * B_K_DIM
    plain, c0 = [], 3 * GDN_W
    for ref, width in ((z_ref, GDN_W), (ga_ref, d_model), (gb_ref, d_model)):
        plain += [(ref, c, c0 + c) for c in range(0, width, pair)]
        c0 += width
    steps = [(part, c) for part in range(3) for c in range(0, GDN_W, pair)]
    for idx, (part, c) in enumerate(steps):
        ref = (q_ref, k_ref, v_ref)[part]
        cc = part * GDN_W + c
        y = _conv_silu_tile(proj(cc, pair), tail_ref.at[:, cc:cc + pair],
                            cw_ref[:, cc:cc + pair] * 0.5)
        if part < 2:
            scale = B_K_DIM ** -0.5 if part == 0 else 1.0
            y = jnp.concatenate([_l2n(y[:, :B_K_DIM], scale), _l2n(y[:, B_K_DIM:], scale)], axis=1)
        ref[:, c:c + pair] = y.astype(ref.dtype)
        lo, hi = idx * len(plain) // len(steps), (idx + 1) * len(plain) // len(steps)
        for pref, pc, wc in plain[lo:hi]:
            pref[:, pc:pc + pair] = proj(wc, pair)
    cq = _rms(proj(c0, A_Q_RANK), cqg_ref[...]).astype(BF16)
    c0 += A_Q_RANK
    qa = jnp.dot(cq, wuq_ref[...], preferred_element_type=F32).astype(BF16)
    qi = jnp.dot(cq, wiq_ref[...], preferred_element_type=F32).astype(BF16)
    blocks = x_ref.shape[0] // Q_BLOCK
    for hd in range(A_HEADS):
        ql = (jnp.dot(qa[:, hd * A_QK_DIM:(hd + 1) * A_QK_DIM], wuk_ref[hd],
                      preferred_element_type=F32) * (A_QK_DIM ** -0.5 * LOG2_E)).astype(BF16)
        for j in range(blocks):
            qlat_ref[j, hd] = ql[j * Q_BLOCK:(j + 1) * Q_BLOCK]
    for hd in range(IDX_HEADS):
        for j in range(blocks):
            qidx_ref[j, hd] = qi[j * Q_BLOCK:(j + 1) * Q_BLOCK, hd * IDX_DIM:(hd + 1) * IDX_DIM]
    ckv_ref[...] = _rms(proj(c0, A_KV_RANK), ckvg_ref[...]).astype(BF16)
    c0 += A_KV_RANK
    small = proj(c0, SMALL_W)
    small_ref[...] = small
    kraw = small[:, :IDX_DIM]
    mu = jnp.mean(kraw, axis=-1, keepdims=True)
    kc = kraw - mu
    kn = kc * lax.rsqrt(jnp.mean(kc * kc, axis=-1, keepdims=True) + EPS)
    kidx_ref[...] = (kn * lng_ref[...] + lnb_ref[...]).astype(BF16)
    g = -jnp.exp(alog_ref[...]) * jax.nn.softplus(small + dtb_ref[...])
    lane = lax.broadcasted_iota(jnp.int32, small.shape, 1)
    gates_ref[...] = jnp.where(lane >= OFF_A, _chunk_cumsum(g, CHUNK), _sigmoid(small))


def _in_proj(x2, g, w_all, cq_g, ckv_g, ln_g, ln_b, conv_w, alog_v, dtb_v, w_uq, w_uk, w_iq,
             *, tm, seq):
    T, D = x2.shape
    n_all = w_all.shape[1]
    assert tm % CHUNK == 0 and seq % tm == 0 and tm % Q_BLOCK == 0
    row = lambda w: pl.BlockSpec((tm, w), lambda i: (i, 0))
    qblk = lambda heads, w: pl.BlockSpec((tm // Q_BLOCK, heads, Q_BLOCK, w),
                                         lambda i: (i, 0, 0, 0))
    wide = ((GDN_W, BF16), (GDN_W, BF16), (GDN_W, BF16), (GDN_W, F32),
            (D, F32), (D, F32))
    out_shapes = (
        jax.ShapeDtypeStruct((T // Q_BLOCK, A_HEADS, Q_BLOCK, A_KV_RANK), BF16),
        jax.ShapeDtypeStruct((T // Q_BLOCK, IDX_HEADS, Q_BLOCK, IDX_DIM), BF16),
        jax.ShapeDtypeStruct((T, A_KV_RANK), BF16),
        jax.ShapeDtypeStruct((T, IDX_DIM), BF16),
        jax.ShapeDtypeStruct((T, SMALL_W), F32),
        jax.ShapeDtypeStruct((T, SMALL_W), F32),
    ) + tuple(jax.ShapeDtypeStruct((T, w), dt) for w, dt in wide)
    out_specs = (qblk(A_HEADS, A_KV_RANK), qblk(IDX_HEADS, IDX_DIM), row(A_KV_RANK),
                 row(IDX_DIM), row(SMALL_W), row(SMALL_W)) + tuple(row(w) for w, _ in wide)
    return pl.pallas_call(
        functools.partial(_in_proj_kernel, d_model=D, tiles_per_seq=seq // tm),
        grid=(T // tm,),
        in_specs=[row(D), _resident((1, D)), _resident((D, n_all)),
                  _resident((1, A_Q_RANK)), _resident((1, A_KV_RANK)),
                  _resident((1, IDX_DIM)), _resident((1, IDX_DIM)),
                  _resident(conv_w.shape), _resident((1, SMALL_W)), _resident((1, SMALL_W)),
                  _resident(w_uq.shape), _resident(w_uk.shape), _resident(w_iq.shape)],
        out_specs=out_specs,
        out_shape=out_shapes,
        scratch_shapes=[pltpu.VMEM((SUBLANES, conv_w.shape[1]), F32)],
        compiler_params=pltpu.CompilerParams(
            dimension_semantics=("arbitrary",), vmem_limit_bytes=VMEM_LIMIT),
        name="in_proj",
    )(x2, g, w_all, cq_g, ckv_g, ln_g, ln_b, conv_w, alog_v, dtb_v, w_uq, w_uk, w_iq)


def _topk_bias(score_ref, bias_ref, row0, n_chunks, topk, tri_ones):
    _, nq, kc = score_ref.shape
    rg = nq // SEARCH_GROUPS
    groups = range(SEARCH_GROUPS)
    rsum = lambda x: jnp.sum(x, axis=1, keepdims=True)
    rmin = lambda x: jnp.min(x, axis=1, keepdims=True)

    def ld(g):
        return jnp.concatenate(
            [score_ref[c, g * rg:(g + 1) * rg, :] for c in range(n_chunks)], axis=1)

    pos = lambda g: row0 + g * rg + lax.broadcasted_iota(jnp.int32, (rg, 1), 0)
    k_eff = [jnp.minimum(pos(g) + 1, topk).astype(F32) for g in groups]
    lo = tuple(rmin(jnp.where(ld(g) == -jnp.inf, jnp.inf, ld(g))) for g in groups)
    hi = tuple(jnp.max(ld(g), axis=1, keepdims=True) for g in groups)

    def bisect(_, carry):
        lo, hi = carry
        mid = [0.5 * (a + b) for a, b in zip(lo, hi)]
        cnt = [rsum(jnp.where(ld(g) >= mid[g], 1.0, 0.0)) for g in groups]
        ge = [cnt[g] >= k_eff[g] for g in groups]
        return (tuple(jnp.where(ge[g], mid[g], lo[g]) for g in groups),
                tuple(jnp.where(ge[g], hi[g], mid[g]) for g in groups))

    lo, hi = lax.fori_loop(0, BISECT_STEPS, bisect, (lo, hi), unroll=BISECT_UNROLL)

    def above(g, v):
        s = ld(g)
        gt = s > v
        return rsum(jnp.where(gt, 1.0, 0.0)), rmin(jnp.where(gt, s, jnp.inf))

    v0 = tuple(rmin(jnp.where(ld(g) >= lo[g], ld(g), jnp.inf)) for g in groups)
    first = [above(g, v0[g]) for g in groups]

    def cond(c):
        _, n_gt, _ = c
        flags = [jnp.max(jnp.where(n_gt[g] >= k_eff[g], 1.0, 0.0)) for g in groups]
        return functools.reduce(jnp.maximum, flags) > 0.0

    def body(c):
        v, n_gt, nxt = c
        v = tuple(jnp.where(n_gt[g] >= k_eff[g], nxt[g], v[g]) for g in groups)
        nxt_state = [above(g, v[g]) for g in groups]
        return v, tuple(a for a, _ in nxt_state), tuple(b for _, b in nxt_state)

    tau, n_gt, _ = lax.while_loop(
        cond, body, (v0, tuple(a for a, _ in first), tuple(b for _, b in first)))
    k_all = jnp.concatenate(k_eff, axis=0)
    need = k_all - jnp.concatenate(n_gt, axis=0)
    tau = jnp.concatenate(tau, axis=0)

    n_ge = jnp.zeros((nq, LANES), F32)
    for c in range(n_chunks):
        for j in range(kc // LANES):
            ge = score_ref[c, :, j * LANES:(j + 1) * LANES] >= tau
            bias_ref[c, :, j * LANES:(j + 1) * LANES] = jnp.where(ge, 0.0, NEG_BIG)
            n_ge = n_ge + jnp.where(ge, 1.0, 0.0)
    surplus = jnp.max(jnp.where(rsum(n_ge) > k_all, 1.0, 0.0)) > 0.0

    @pl.when(surplus)
    def _():
        carry = jnp.zeros((nq, LANES), F32)
        for c in range(n_chunks):
            for j in range(kc // LANES):
                sc = score_ref[c, :, j * LANES:(j + 1) * LANES]
                eq = sc == tau
                r = jnp.dot(jnp.where(eq, 1.0, 0.0).astype(BF16), tri_ones,
                            preferred_element_type=F32)
                prefix = r[:, :LANES] + carry
                carry = carry + r[:, LANES:]
                sel = (sc > tau) | (eq & (prefix <= need))
                bias_ref[c, :, j * LANES:(j + 1) * LANES] = jnp.where(sel, 0.0, NEG_BIG)


def _dsa_kernel(qlat_ref, qidx_ref, small_ref, ckv_ref, kidx_ref, wuv_ref,
                o_ref, score_ref, bias_ref, s_ref, m_ref, l_ref, acc_ref, *, topk):
    i = pl.program_id(1)
    nq = Q_BLOCK
    max_chunks, _, kc = score_ref.shape
    n_kc = (i * nq) // kc + 1
    ng = A_HEADS // HEAD_GROUP
    rows = HEAD_GROUP * nq
    tiles = kc // LANES
    nt = (((1,), (1,)), ((), ()))
    w_idx = small_ref[0][:, OFF_WIDX:OFF_WIDX + IDX_HEADS] * (IDX_HEADS ** -0.5 * IDX_DIM ** -0.5)
    row_pos = i * nq + lax.broadcasted_iota(jnp.int32, (nq, 1), 0)

    r_i = lax.broadcasted_iota(jnp.int32, (LANES, 2 * LANES), 0)
    c_i = lax.broadcasted_iota(jnp.int32, (LANES, 2 * LANES), 1)
    tri_ones = jnp.where((c_i >= LANES) | (r_i <= c_i), 1.0, 0.0).astype(BF16)

    def key_rows(c):
        return slice(c * kc, (c + 1) * kc)

    def score_chunk(c):
        logits = lax.dot_general(qidx_ref[0].reshape(IDX_HEADS * nq, IDX_DIM),
                                 kidx_ref[0, key_rows(c), :], nt,
                                 preferred_element_type=F32)
        sc = jnp.zeros((nq, kc), F32)
        for h in range(IDX_HEADS):
            sc = sc + w_idx[:, h:h + 1] * jnp.maximum(logits[h * nq:(h + 1) * nq], 0.0)
        col = c * kc + lax.broadcasted_iota(jnp.int32, (nq, kc), 1)
        score_ref[c] = jnp.where(col <= row_pos, sc, -jnp.inf)

    def pass1(c):
        ckv = ckv_ref[0, key_rows(c), :]
        bias = bias_ref[c]
        ss = [lax.dot_general(
            qlat_ref[0, g * HEAD_GROUP:(g + 1) * HEAD_GROUP].reshape(rows, A_KV_RANK), ckv, nt,
            preferred_element_type=F32) for g in range(ng)]
        for g in range(ng):
            s = (ss[g].reshape(HEAD_GROUP, nq, kc) + bias[None]).reshape(rows, kc)
            s_ref[c, g] = s
            mt = s[:, :LANES]
            for j in range(1, tiles):
                mt = jnp.maximum(mt, s[:, j * LANES:(j + 1) * LANES])
            m_ref[g] = mt if c == 0 else jnp.maximum(m_ref[g], mt)

    def pass2(c):
        ckv = ckv_ref[0, key_rows(c), :]
        for g in range(ng):
            s = s_ref[c, g]
            m = m_ref[g]
            p = [jnp.exp2(s[:, j * LANES:(j + 1) * LANES] - m) for j in range(tiles)]
            lsum = functools.reduce(lambda a, b: a + b, p)
            pv = jnp.dot(jnp.concatenate(p, axis=1).astype(BF16), ckv,
                         preferred_element_type=F32)
            l_ref[g] = lsum if c == 0 else l_ref[g] + lsum
            acc_ref[g] = pv if c == 0 else acc_ref[g] + pv

    def block(nk, search):
        for c in range(nk):
            score_chunk(c)
        if search:
            _topk_bias(score_ref, bias_ref, i * nq, nk, topk, tri_ones)
        else:
            bias_ref[0] = jnp.where(score_ref[0] > -jnp.inf, 0.0, NEG_BIG)
        for c in range(nk):
            pass1(c)
        for g in range(ng):
            m_ref[g] = jnp.broadcast_to(jnp.max(m_ref[g], axis=1, keepdims=True), (rows, LANES))
        for c in range(nk):
            pass2(c)
        for g in range(ng):
            l = jnp.sum(l_ref[g], axis=1, keepdims=True)
            og = (acc_ref[g] * (1.0 / l)).astype(BF16)
            for hh in range(HEAD_GROUP):
                h = g * HEAD_GROUP + hh
                o_ref[0, :, h * A_V_DIM:(h + 1) * A_V_DIM] = jnp.dot(
                    og[hh * nq:(hh + 1) * nq], wuv_ref[h],
                    preferred_element_type=F32).astype(o_ref.dtype)

    keep_all = (i + 1) * nq <= topk
    pl.when(keep_all)(functools.partial(block, 1, False))
    for v in range(max_chunks):
        pl.when(jnp.logical_and(n_kc == v + 1, jnp.logical_not(keep_all)))(
            functools.partial(block, v + 1, True))


def _dsa(qlat, qidx, small, ckv, kidx, w_uv):
    B, L, _ = ckv.shape
    topk = min(TOPK_MAX, L // 4)
    kc = min(KEY_CHUNK, L)
    assert L % kc == 0 and kc % Q_BLOCK == 0 and topk <= kc
    n_chunks = L // kc
    nb = L // Q_BLOCK
    ng = A_HEADS // HEAD_GROUP
    rows = HEAD_GROUP * Q_BLOCK
    a_width = A_HEADS * A_V_DIM
    return pl.pallas_call(
        functools.partial(_dsa_kernel, topk=topk),
        grid=(B, nb),
        in_specs=[
            pl.BlockSpec((1, A_HEADS, Q_BLOCK, A_KV_RANK), lambda b, i: (b * nb + i, 0, 0, 0)),
            pl.BlockSpec((1, IDX_HEADS, Q_BLOCK, IDX_DIM), lambda b, i: (b * nb + i, 0, 0, 0)),
            pl.BlockSpec((1, Q_BLOCK, SMALL_W), lambda b, i: (b, i, 0)),
            pl.BlockSpec((1, L, A_KV_RANK), lambda b, i: (b, 0, 0)),
            pl.BlockSpec((1, L, IDX_DIM), lambda b, i: (b, 0, 0)),
            _resident(w_uv.shape),
        ],
        out_specs=pl.BlockSpec((1, Q_BLOCK, a_width), lambda b, i: (b, i, 0)),
        out_shape=jax.ShapeDtypeStruct((B, L, a_width), BF16),
        scratch_shapes=[
            pltpu.VMEM((n_chunks, Q_BLOCK, kc), F32),
            pltpu.VMEM((n_chunks, Q_BLOCK, kc), F32),
            pltpu.VMEM((n_chunks, ng, rows, kc), F32),
            pltpu.VMEM((ng, rows, LANES), F32),
            pltpu.VMEM((ng, rows, LANES), F32),
            pltpu.VMEM((ng, rows, A_KV_RANK), F32),
        ],
        compiler_params=pltpu.CompilerParams(
            dimension_semantics=("arbitrary", "arbitrary"), vmem_limit_bytes=VMEM_LIMIT),
        name="dsa",
    )(qlat, qidx, small, ckv, kidx, w_uv)


def _lane_bcast(x, c):
    lane = lax.broadcasted_iota(jnp.int32, x.shape, 1)
    col = jnp.sum(jnp.where(lane == c, x, 0.0), axis=1, keepdims=True)
    return jnp.broadcast_to(col, x.shape)


def _gdn_kernel(q_ref, k_ref, v_ref, z_ref, gates_ref, og_ref, o_ref,
                mneg_ref, r_ref, qeff_ref, o0_ref, at_ref, *, seq, hp):
    hg = pl.program_id(1)
    C = CHUNK
    n_chunks = seq // C
    dk = B_K_DIM

    R = PACK * C
    rr = lax.broadcasted_iota(jnp.int32, (R, R), 0)
    cc = lax.broadcasted_iota(jnp.int32, (R, R), 1)
    same = (rr // C) == (cc // C)
    tri = same & (rr >= cc)
    strict = same & (rr > cc)
    eye = jnp.where(rr == cc, 1.0, 0.0)
    kcol = lax.broadcasted_iota(jnp.int32, (dk, R), 1) // C
    nt = (((1,), (1,)), ((), ()))
    dot = functools.partial(jnp.dot, preferred_element_type=F32)

    def prep(t, _):
        ids = [(hh, t * PREP_GROUP + j) for j in range(PREP_GROUP) for hh in range(hp)]
        each = lambda f, *cols: [f(*a) for a in zip(*cols)]
        rows = [pl.ds(pl.multiple_of(n * R, R), R) for _, n in ids]
        load = lambda ref: [ref[0, r, hh * dk:(hh + 1) * dk] for (hh, _), r in zip(ids, rows)]
        qb, kb, vb = (load(r) for r in (q_ref, k_ref, v_ref))
        q, k, v = (each(lambda a: a.astype(F32), x) for x in (qb, kb, vb))
        gt = [gates_ref[0, r, :] for r in rows]
        beta = [_lane_bcast(a, OFF_BETA + hg * hp + hh) for a, (hh, _) in zip(gt, ids)]
        G = [_lane_bcast(a, OFF_A + hg * hp + hh) for a, (hh, _) in zip(gt, ids)]
        Gl = each(lambda g: jnp.concatenate(
            [jnp.broadcast_to(g[(p + 1) * C - 1:(p + 1) * C, :], (C, LANES)) for p in range(PACK)],
            axis=0), G)
        decay = each(lambda g: jnp.exp(jnp.where(tri, g[:, :R] - g.T[:R, :R], -jnp.inf)), G)
        eG = each(jnp.exp, G)
        qkk = each(lambda q_, k_: lax.dot_general(
            jnp.concatenate([q_, k_], axis=0), k_, nt, preferred_element_type=F32), qb, kb)
        N = each(lambda b, a, d: jnp.where(strict, b[:, :R] * a[R:] * d, 0.0), beta, qkk, decay)
        X = each(lambda a: eye - a, N)
        Nb = each(lambda a: a.astype(BF16), N)
        Pb = each(lambda a: dot(a, a).astype(BF16), Nb)
        steps = C.bit_length() - 2
        for it in range(steps):
            if it + 1 < steps:
                xp = each(lambda x, p: dot(jnp.concatenate([x.astype(BF16), p], axis=0), p), X, Pb)
                X = each(lambda x, a: x + a[:R], X, xp)
                Pb = each(lambda a: a[R:].astype(BF16), xp)
            else:
                X = each(lambda x, p: x + dot(x.astype(BF16), p), X, Pb)
        rhs = each(lambda v_, k_, b, e: jnp.concatenate(
            [v_ * b, k_ * (b * e)], axis=-1).astype(BF16), v, k, beta, eG)
        sol = each(lambda x, r: dot(x.astype(BF16), r).astype(BF16), X, rhs)
        qk = each(lambda a, d: (a[:R] * d).astype(BF16), qkk, decay)
        ktT = each(lambda k_, g, gl: (k_ * jnp.exp(gl - g)).T.astype(BF16), k, G, Gl)
        ktbd = each(lambda a: jnp.concatenate(
            [jnp.where(kcol == p, a, jnp.zeros_like(a)) for p in range(PACK)], axis=0), ktT)
        kts = each(dot, ktbd, sol)
        qks = each(dot, qk, sol)
        for i, (hh, n) in enumerate(ids):
            qe = q[i] * eG[i]
            for p in range(PACK):
                ch = n * PACK + p
                r_ref[hh, ch] = kts[i][p * dk:(p + 1) * dk, :B_V_DIM]
                mneg_ref[hh, ch] = kts[i][p * dk:(p + 1) * dk, B_V_DIM:].astype(BF16)
                o0_ref[hh, ch] = qks[i][p * C:(p + 1) * C, :B_V_DIM]
                qeff_ref[hh, ch] = (qe[p * C:(p + 1) * C]
                                    - qks[i][p * C:(p + 1) * C, B_V_DIM:]).astype(BF16)
                at_ref[hh, ch] = jnp.exp(G[i][(p + 1) * C - 1:(p + 1) * C, :])
        return 0

    lax.fori_loop(0, n_chunks // (PREP_GROUP * PACK), prep, 0)

    og = og_ref[...]

    def emit(n, o):
        rows = pl.ds(n * C if isinstance(n, int) else pl.multiple_of(n * C, C), C)
        for hh in range(hp):
            ls = slice(hh * dk, (hh + 1) * dk)
            o_ref[0, rows, ls] = (_rms(o[hh], og) * _silu(z_ref[0, rows, ls])).astype(o_ref.dtype)

    def scan(n, carry):
        S, o_prev = carry
        emit((n + n_chunks - 1) % n_chunks, o_prev)
        Sb = [s.astype(BF16) for s in S]
        ms = [dot(mneg_ref[hh, n], Sb[hh]) for hh in range(hp)]
        os_ = [dot(qeff_ref[hh, n], Sb[hh]) for hh in range(hp)]
        new = tuple(S[hh] * at_ref[hh, n] - ms[hh] + r_ref[hh, n] for hh in range(hp))
        return new, tuple(os_[hh] + o0_ref[hh, n] for hh in range(hp))

    zeros = lambda rows_: tuple(jnp.zeros((rows_, B_V_DIM), F32) for _ in range(hp))
    _, o_last = lax.fori_loop(0, n_chunks, scan, (zeros(dk), zeros(C)))
    emit(n_chunks - 1, o_last)


def _gdn(qn, kn, vn, zb, gates, onorm_g):
    B, L, W = qn.shape
    hp = GDN_HEADS_PER_STEP
    n_chunks = L // CHUNK
    heads = W // B_K_DIM
    assert heads % hp == 0 and n_chunks % (PREP_GROUP * PACK) == 0 and PACK * CHUNK <= LANES
    col = pl.BlockSpec((1, L, hp * B_K_DIM), lambda b, h: (b, 0, h))
    return pl.pallas_call(
        functools.partial(_gdn_kernel, seq=L, hp=hp),
        grid=(B, heads // hp),
        in_specs=[col, col, col, col,
                  pl.BlockSpec((1, L, SMALL_W), lambda b, h: (b, 0, 0)),
                  _resident((1, B_V_DIM))],
        out_specs=col,
        out_shape=jax.ShapeDtypeStruct((B, L, W), BF16),
        scratch_shapes=[
            pltpu.VMEM((hp, n_chunks, B_K_DIM, B_K_DIM), BF16),
            pltpu.VMEM((hp, n_chunks, B_K_DIM, B_V_DIM), F32),
            pltpu.VMEM((hp, n_chunks, CHUNK, B_K_DIM), BF16),
            pltpu.VMEM((hp, n_chunks, CHUNK, B_V_DIM), F32),
            pltpu.VMEM((hp, n_chunks, 1, LANES), F32),
        ],
        compiler_params=pltpu.CompilerParams(
            dimension_semantics=("arbitrary", "arbitrary"), vmem_limit_bytes=VMEM_LIMIT),
        name="gdn",
    )(qn, kn, vn, zb, gates, onorm_g)


def _merge_kernel(x_ref, oa_ref, ob_ref, ga_ref, gb_ref, wa_ref, wb_ref, wo_ref,
                  fg_ref, wg_ref, wu_ref, wd_ref, og_ref, o_ref, *, ff_chunk, final_norm):
    ya = jnp.dot(oa_ref[...], wa_ref[...], preferred_element_type=F32)
    yb = jnp.dot(ob_ref[...], wb_ref[...], preferred_element_type=F32)
    merged = _sigmoid(ga_ref[...]) * ya + _sigmoid(gb_ref[...]) * yb
    x1 = x_ref[...] + jnp.dot(merged.astype(BF16), wo_ref[...], preferred_element_type=F32)
    h = _rms(x1, fg_ref[...]).astype(BF16)
    d_ff = wg_ref.shape[1]
    acc = x1
    for c0 in range(0, d_ff, ff_chunk):
        gate = jnp.dot(h, wg_ref[:, c0:c0 + ff_chunk], preferred_element_type=F32)
        up = jnp.dot(h, wu_ref[:, c0:c0 + ff_chunk], preferred_element_type=F32)
        act = (_silu(gate) * up).astype(BF16)
        acc = acc + jnp.dot(act, wd_ref[c0:c0 + ff_chunk, :], preferred_element_type=F32)
    o_ref[...] = _rms(acc, og_ref[...]) if final_norm else acc


def _merge(x2, oa, ob, ga, gb, wa, wb, wo, fg, wg, wu, wd, og, *, tm, final_norm):
    T, D = x2.shape
    d_ff = wg.shape[1]
    ff_chunk = d_ff // 2 if (d_ff // 2) % LANES == 0 else d_ff
    row = pl.BlockSpec((tm, D), lambda i: (i, 0))
    return pl.pallas_call(
        functools.partial(_merge_kernel, ff_chunk=ff_chunk, final_norm=final_norm),
        grid=(T // tm,),
        in_specs=[row, row, row, row, row,
                  _resident(wa.shape), _resident(wb.shape), _resident(wo.shape),
                  _resident(fg.shape), _resident(wg.shape), _resident(wu.shape),
                  _resident(wd.shape), _resident(og.shape)],
        out_specs=row,
        out_shape=jax.ShapeDtypeStruct((T, D), F32),
        compiler_params=pltpu.CompilerParams(
            dimension_semantics=("arbitrary",), vmem_limit_bytes=VMEM_LIMIT),
        name="merge_ffn",
    )(x2, oa, ob, ga, gb, wa, wb, wo, fg, wg, wu, wd, og)


def _reorder_w_in(w):
    sizes = (A_Q_RANK, A_KV_RANK, IDX_DIM, IDX_HEADS,
             B_HEADS * B_K_DIM, B_HEADS * B_K_DIM, B_HEADS * B_V_DIM, B_HEADS, B_HEADS,
             B_HEADS * B_V_DIM, w.shape[0], w.shape[0])
    parts, c0 = [], 0
    for s in sizes:
        parts.append(w[:, c0:c0 + s])
        c0 += s
    (c_q, c_kv, k_idx, w_idx, q_b, k_b, v_b, beta_b, a_b, z_b, gate_a, gate_b) = parts
    pad = jnp.zeros((w.shape[0], SMALL_W - IDX_DIM - IDX_HEADS - 2 * B_HEADS), w.dtype)
    return jnp.concatenate(
        [q_b, k_b, v_b, z_b, gate_a, gate_b, c_q, c_kv, k_idx, w_idx, beta_b, a_b, pad],
        axis=1).astype(BF16)


def kernel(x, mix_norm_g, w_in, cq_norm_g, ckv_norm_g, w_uq, w_uk, w_uv, w_iq, kidx_ln_g, kidx_ln_b, w_branch_a, conv_w, a_log, dt_bias, onorm_g, w_branch_b, w_out, ffn_norm_g, w_gate, w_up, w_down, final_norm_g):
    B, L, D = x.shape
    depth = w_in.shape[0]
    T = B * L
    tm = min(TOKEN_TILE, L)
    x2 = x.reshape(T, D)
    vec = lambda a: a.reshape(1, -1).astype(F32)
    at_a = lambda a: jnp.zeros((1, SMALL_W), F32).at[0, OFF_A:OFF_A + B_HEADS].set(a.astype(F32))
    for l in range(depth):
        (qlat, qidx, ckv, kidx, small, gates, qn, kn, vn, zb, ga, gb) = _in_proj(
            x2, vec(mix_norm_g[l]), _reorder_w_in(w_in[l]), vec(cq_norm_g[l]),
            vec(ckv_norm_g[l]), vec(kidx_ln_g[l]), vec(kidx_ln_b[l]), conv_w[l].astype(F32),
            at_a(a_log[l]), at_a(dt_bias[l]), w_uq[l].astype(BF16), w_uk[l].astype(BF16),
            w_iq[l].astype(BF16), tm=tm, seq=L)
        seq = lambda a: a.reshape(B, L, a.shape[-1])
        o_a = _dsa(qlat, qidx, seq(small), seq(ckv), seq(kidx), w_uv[l].astype(BF16))
        o_b = _gdn(seq(qn), seq(kn), seq(vn), seq(zb), seq(gates), vec(onorm_g[l]))
        x2 = _merge(x2, o_a.reshape(T, -1), o_b.reshape(T, -1), ga, gb,
                    w_branch_a[l].astype(BF16), w_branch_b[l].astype(BF16),
                    w_out[l].astype(BF16), vec(ffn_norm_g[l]), w_gate[l].astype(BF16),
                    w_up[l].astype(BF16), w_down[l].astype(BF16), vec(final_norm_g),
                    tm=tm, final_norm=(l == depth - 1))
    return x2.reshape(B, L, D)
```

```python
import functools

import jax
import jax.numpy as jnp
from jax import lax
from jax.experimental import pallas as pl
from jax.experimental.pallas import tpu as pltpu

F32 = jnp.float32
BF16 = jnp.bfloat16

EPS = 1e-6
A_HEADS = 16
A_QK_DIM = 64
A_V_DIM = 64
A_Q_RANK = 256
A_KV_RANK = 256
IDX_HEADS = 8
IDX_DIM = 64
TOPK_MAX = 256
Q_BLOCK = 128
B_HEADS = 8
B_K_DIM = 128
B_V_DIM = 128
CONV_WIDTH = 4
CHUNK = 64

GDN_W = B_HEADS * B_K_DIM
assert B_K_DIM == B_V_DIM
LANES = 128
SUBLANES = 8
SMALL_W = LANES
OFF_WIDX = IDX_DIM
OFF_BETA = IDX_DIM + IDX_HEADS
OFF_A = IDX_DIM + IDX_HEADS + B_HEADS
VMEM_LIMIT = 56 * 1024 * 1024
NEG_BIG = -1e30
LOG2_E = 1.4426950408889634
BISECT_STEPS = 16
BISECT_UNROLL = 8
TOKEN_TILE = 256
MERGE_TILE = 512
MERGE_SUB = 2
HEAD_GROUP = 4
KEY_CHUNK = 512
SEARCH_GROUPS = 4
GDN_HEADS_PER_STEP = 4
PACK = 2
PREP_GROUP = 4


def _resident(shape):
    nd = len(shape)
    return pl.BlockSpec(shape, lambda *_: (0,) * nd, pipeline_mode=pl.Buffered(1))


def _rms(x, g):
    return x * lax.rsqrt(jnp.mean(x * x, axis=-1, keepdims=True) + EPS) * g


def _l2n(x, scale=1.0):
    return x * (lax.rsqrt(jnp.sum(x * x, axis=-1, keepdims=True) + EPS) * scale)


def _sigmoid(x):
    return 0.5 * jnp.tanh(0.5 * x) + 0.5


def _silu(x):
    u = 0.5 * x
    return u + u * jnp.tanh(u)


def _chunk_cumsum(g, chunk):
    pos = lax.broadcasted_iota(jnp.int32, g.shape, 0) % chunk
    d = 1
    while d < chunk:
        g = g + jnp.where(pos >= d, pltpu.roll(g, d, 0), 0.0)
        d *= 2
    return g


def _conv_silu_tile(x, tail_ref, w_half):
    full = jnp.concatenate([tail_ref[...], x], axis=0)
    u = x * w_half[CONV_WIDTH - 1:CONV_WIDTH, :]
    for d in range(1, CONV_WIDTH):
        u = u + pltpu.roll(full, d, 0)[SUBLANES:] * w_half[CONV_WIDTH - 1 - d:CONV_WIDTH - d, :]
    tail_ref[...] = x[x.shape[0] - SUBLANES:]
    return u + u * jnp.tanh(u)


def _in_proj_kernel(x_ref, g_ref, w_ref, cqg_ref, ckvg_ref, lng_ref, lnb_ref, cw_ref,
                    alog_ref, dtb_ref, wuq_ref, wuk_ref, wiq_ref,
                    qlat_ref, qidx_ref, ckv_ref, kidx_ref, small_ref, gates_ref, q_ref, k_ref, v_ref,
                    z_ref, ga_ref, gb_ref, tail_ref, *, d_model, tiles_per_seq):
    @pl.when(pl.program_id(0) % tiles_per_seq == 0)
    def _():
        tail_ref[...] = jnp.zeros_like(tail_ref)

    h = _rms(x_ref[...], g_ref[...]).astype(BF16)

    def proj(c0, width):
        return jnp.dot(h, w_ref[:, c0:c0 + width], preferred_element_type=F32)

    pair = 2 * B_K_DIM
    plain, c0 = [], 3 * GDN_W
    for ref, width in ((z_ref, GDN_W), (ga_ref, d_model), (gb_ref, d_model)):
        plain += [(ref, c, c0 + c) for c in range(0, width, pair)]
        c0 += width
    steps = [(part, c) for part in range(3) for c in range(0, GDN_W, pair)]
    for idx, (part, c) in enumerate(steps):
        ref = (q_ref, k_ref, v_ref)[part]
        cc = part * GDN_W + c
        y = _conv_silu_tile(proj(cc, pair), tail_ref.at[:, cc:cc + pair],
                            cw_ref[:, cc:cc + pair] * 0.5)
        if part < 2:
            scale = B_K_DIM ** -0.5 if part == 0 else 1.0
            y = jnp.concatenate([_l2n(y[:, :B_K_DIM], scale), _l2n(y[:, B_K_DIM:], scale)], axis=1)
        ref[:, c:c + pair] = y.astype(ref.dtype)
        lo, hi = idx * len(plain) // len(steps), (idx + 1) * len(plain) // len(steps)
        for pref, pc, wc in plain[lo:hi]:
            pref[:, pc:pc + pair] = proj(wc, pair)
    cq = _rms(proj(c0, A_Q_RANK), cqg_ref[...]).astype(BF16)
    c0 += A_Q_RANK
    qa = jnp.dot(cq, wuq_ref[...], preferred_element_type=F32).astype(BF16)
    qi = jnp.dot(cq, wiq_ref[...], preferred_element_type=F32).astype(BF16)
    blocks = x_ref.shape[0] // Q_BLOCK
    for hd in range(A_HEADS):
        ql = (jnp.dot(qa[:, hd * A_QK_DIM:(hd + 1) * A_QK_DIM], wuk_ref[hd],
                      preferred_element_type=F32) * (A_QK_DIM ** -0.5 * LOG2_E)).astype(BF16)
        for j in range(blocks):
            qlat_ref[j, hd] = ql[j * Q_BLOCK:(j + 1) * Q_BLOCK]
    for hd in range(IDX_HEADS):
        for j in range(blocks):
            qidx_ref[j, hd] = qi[j * Q_BLOCK:(j + 1) * Q_BLOCK, hd * IDX_DIM:(hd + 1) * IDX_DIM]
    ckv_ref[...] = _rms(proj(c0, A_KV_RANK), ckvg_ref[...]).astype(BF16)
    c0 += A_KV_RANK
    small = proj(c0, SMALL_W)
    small_ref[...] = small
    kraw = small[:, :IDX_DIM]
    mu = jnp.mean(kraw, axis=-1, keepdims=True)
    kc = kraw - mu
    kn = kc * lax.rsqrt(jnp.mean(kc * kc, axis=-1, keepdims=True) + EPS)
    kidx_ref[...] = (kn * lng_ref[...] + lnb_ref[...]).astype(BF16)
    g = -jnp.exp(alog_ref[...]) * jax.nn.softplus(small + dtb_ref[...])
    lane = lax.broadcasted_iota(jnp.int32, small.shape, 1)
    gates_ref[...] = jnp.where(lane >= OFF_A, _chunk_cumsum(g, CHUNK), _sigmoid(small))


def _in_proj(x2, g, w_all, cq_g, ckv_g, ln_g, ln_b, conv_w, alog_v, dtb_v, w_uq, w_uk, w_iq,
             *, tm, seq):
    T, D = x2.shape
    n_all = w_all.shape[1]
    assert tm % CHUNK == 0 and seq % tm == 0 and tm % Q_BLOCK == 0
    row = lambda w: pl.BlockSpec((tm, w), lambda i: (i, 0))
    qblk = lambda heads, w: pl.BlockSpec((tm // Q_BLOCK, heads, Q_BLOCK, w),
                                         lambda i: (i, 0, 0, 0))
    wide = ((GDN_W, BF16), (GDN_W, BF16), (GDN_W, BF16), (GDN_W, F32),
            (D, F32), (D, F32))
    out_shapes = (
        jax.ShapeDtypeStruct((T // Q_BLOCK, A_HEADS, Q_BLOCK, A_KV_RANK), BF16),
        jax.ShapeDtypeStruct((T // Q_BLOCK, IDX_HEADS, Q_BLOCK, IDX_DIM), BF16),
        jax.ShapeDtypeStruct((T, A_KV_RANK), BF16),
        jax.ShapeDtypeStruct((T, IDX_DIM), BF16),
        jax.ShapeDtypeStruct((T, SMALL_W), F32),
        jax.ShapeDtypeStruct((T, SMALL_W), F32),
    ) + tuple(jax.ShapeDtypeStruct((T, w), dt) for w, dt in wide)
    out_specs = (qblk(A_HEADS, A_KV_RANK), qblk(IDX_HEADS, IDX_DIM), row(A_KV_RANK),
                 row(IDX_DIM), row(SMALL_W), row(SMALL_W)) + tuple(row(w) for w, _ in wide)
    return pl.pallas_call(
        functools.partial(_in_proj_kernel, d_model=D, tiles_per_seq=seq // tm),
        grid=(T // tm,),
        in_specs=[row(D), _resident((1, D)), _resident((D, n_all)),
                  _resident((1, A_Q_RANK)), _resident((1, A_KV_RANK)),
                  _resident((1, IDX_DIM)), _resident((1, IDX_DIM)),
                  _resident(conv_w.shape), _resident((1, SMALL_W)), _resident((1, SMALL_W)),
                  _resident(w_uq.shape), _resident(w_uk.shape), _resident(w_iq.shape)],
        out_specs=out_specs,
        out_shape=out_shapes,
        scratch_shapes=[pltpu.VMEM((SUBLANES, conv_w.shape[1]), F32)],
        compiler_params=pltpu.CompilerParams(
            dimension_semantics=("arbitrary",), vmem_limit_bytes=VMEM_LIMIT),
        name="in_proj",
    )(x2, g, w_all, cq_g, ckv_g, ln_g, ln_b, conv_w, alog_v, dtb_v, w_uq, w_uk, w_iq)


def _topk_bias(score_ref, bias_ref, row0, n_chunks, topk, tri_ones):
    _, nq, kc = score_ref.shape
    rg = nq // SEARCH_GROUPS
    groups = range(SEARCH_GROUPS)
    rsum = lambda x: jnp.sum(x, axis=1, keepdims=True)
    rmin = lambda x: jnp.min(x, axis=1, keepdims=True)

    def ld(g):
        return jnp.concatenate(
            [score_ref[c, g * rg:(g + 1) * rg, :] for c in range(n_chunks)], axis=1)

    pos = lambda g: row0 + g * rg + lax.broadcasted_iota(jnp.int32, (rg, 1), 0)
    k_eff = [jnp.minimum(pos(g) + 1, topk).astype(F32) for g in groups]
    lo = tuple(rmin(jnp.where(ld(g) == -jnp.inf, jnp.inf, ld(g))) for g in groups)
    hi = tuple(jnp.max(ld(g), axis=1, keepdims=True) for g in groups)

    def bisect(_, carry):
        lo, hi = carry
        mid = [0.5 * (a + b) for a, b in zip(lo, hi)]
        cnt = [rsum(jnp.where(ld(g) >= mid[g], 1.0, 0.0)) for g in groups]
        ge = [cnt[g] >= k_eff[g] for g in groups]
        return (tuple(jnp.where(ge[g], mid[g], lo[g]) for g in groups),
                tuple(jnp.where(ge[g], hi[g], mid[g]) for g in groups))

    lo, hi = lax.fori_loop(0, BISECT_STEPS, bisect, (lo, hi), unroll=BISECT_UNROLL)

    def above(g, v):
        s = ld(g)
        gt = s > v
        return rsum(jnp.where(gt, 1.0, 0.0)), rmin(jnp.where(gt, s, jnp.inf))

    v0 = tuple(rmin(jnp.where(ld(g) >= lo[g], ld(g), jnp.inf)) for g in groups)
    first = [above(g, v0[g]) for g in groups]

    def cond(c):
        _, n_gt, _ = c
        flags = [jnp.max(jnp.where(n_gt[g] >= k_eff[g], 1.0, 0.0)) for g in groups]
        return functools.reduce(jnp.maximum, flags) > 0.0

    def body(c):
        v, n_gt, nxt = c
        v = tuple(jnp.where(n_gt[g] >= k_eff[g], nxt[g], v[g]) for g in groups)
        nxt_state = [above(g, v[g]) for g in groups]
        return v, tuple(a for a, _ in nxt_state), tuple(b for _, b in nxt_state)

    tau, n_gt, _ = lax.while_loop(
        cond, body, (v0, tuple(a for a, _ in first), tuple(b for _, b in first)))
    k_all = jnp.concatenate(k_eff, axis=0)
    need = k_all - jnp.concatenate(n_gt, axis=0)
    tau = jnp.concatenate(tau, axis=0)

    n_ge = jnp.zeros((nq, LANES), F32)
    for c in range(n_chunks):
        for j in range(kc // LANES):
            ge = score_ref[c, :, j * LANES:(j + 1) * LANES] >= tau
            bias_ref[c, :, j * LANES:(j + 1) * LANES] = jnp.where(ge, 0.0, NEG_BIG)
            n_ge = n_ge + jnp.where(ge, 1.0, 0.0)
    surplus = jnp.max(jnp.where(rsum(n_ge) > k_all, 1.0, 0.0)) > 0.0

    @pl.when(surplus)
    def _():
        carry = jnp.zeros((nq, LANES), F32)
        for c in range(n_chunks):
            for j in range(kc // LANES):
                sc = score_ref[c, :, j * LANES:(j + 1) * LANES]
                eq = sc == tau
                r = jnp.dot(jnp.where(eq, 1.0, 0.0).astype(BF16), tri_ones,
                            preferred_element_type=F32)
                prefix = r[:, :LANES] + carry
                carry = carry + r[:, LANES:]
                sel = (sc > tau) | (eq & (prefix <= need))
                bias_ref[c, :, j * LANES:(j + 1) * LANES] = jnp.where(sel, 0.0, NEG_BIG)


def _dsa_kernel(qlat_ref, qidx_ref, small_ref, ckv_ref, kidx_ref, wuv_ref,
                o_ref, score_ref, bias_ref, s_ref, m_ref, l_ref, acc_ref, *, topk):
    i = pl.program_id(1)
    nq = Q_BLOCK
    max_chunks, _, kc = score_ref.shape
    n_kc = (i * nq) // kc + 1
    ng = A_HEADS // HEAD_GROUP
    rows = HEAD_GROUP * nq
    tiles = kc // LANES
    nt = (((1,), (1,)), ((), ()))
    w_idx = small_ref[0][:, OFF_WIDX:OFF_WIDX + IDX_HEADS] * (IDX_HEADS ** -0.5 * IDX_DIM ** -0.5)
    row_pos = i * nq + lax.broadcasted_iota(jnp.int32, (nq, 1), 0)

    r_i = lax.broadcasted_iota(jnp.int32, (LANES, 2 * LANES), 0)
    c_i = lax.broadcasted_iota(jnp.int32, (LANES, 2 * LANES), 1)
    tri_ones = jnp.where((c_i >= LANES) | (r_i <= c_i), 1.0, 0.0).astype(BF16)

    def key_rows(c):
        return slice(c * kc, (c + 1) * kc)

    def score_chunk(c):
        logits = lax.dot_general(qidx_ref[0].reshape(IDX_HEADS * nq, IDX_DIM),
                                 kidx_ref[0, key_rows(c), :], nt,
                                 preferred_element_type=F32)
        sc = jnp.zeros((nq, kc), F32)
        for h in range(IDX_HEADS):
            sc = sc + w_idx[:, h:h + 1] * jnp.maximum(logits[h * nq:(h + 1) * nq], 0.0)
        col = c * kc + lax.broadcasted_iota(jnp.int32, (nq, kc), 1)
        score_ref[c] = jnp.where(col <= row_pos, sc, -jnp.inf)

    def pass1(c):
        ckv = ckv_ref[0, key_rows(c), :]
        bias = bias_ref[c]
        ss = [lax.dot_general(
            qlat_ref[0, g * HEAD_GROUP:(g + 1) * HEAD_GROUP].reshape(rows, A_KV_RANK), ckv, nt,
            preferred_element_type=F32) for g in range(ng)]
        for g in range(ng):
            s = (ss[g].reshape(HEAD_GROUP, nq, kc) + bias[None]).reshape(rows, kc)
            s_ref[c, g] = s
            mt = s[:, :LANES]
            for j in range(1, tiles):
                mt = jnp.maximum(mt, s[:, j * LANES:(j + 1) * LANES])
            m_ref[g] = mt if c == 0 else jnp.maximum(m_ref[g], mt)

    def pass2(c):
        ckv = ckv_ref[0, key_rows(c), :]
        for g in range(ng):
            s = s_ref[c, g]
            m = m_ref[g]
            p = [jnp.exp2(s[:, j * LANES:(j + 1) * LANES] - m) for j in range(tiles)]
            lsum = functools.reduce(lambda a, b: a + b, p)
            pv = jnp.dot(jnp.concatenate(p, axis=1).astype(BF16), ckv,
                         preferred_element_type=F32)
            l_ref[g] = lsum if c == 0 else l_ref[g] + lsum
            acc_ref[g] = pv if c == 0 else acc_ref[g] + pv

    def block(nk, search):
        for c in range(nk):
            score_chunk(c)
        if search:
            _topk_bias(score_ref, bias_ref, i * nq, nk, topk, tri_ones)
        else:
            bias_ref[0] = jnp.where(score_ref[0] > -jnp.inf, 0.0, NEG_BIG)
        for c in range(nk):
            pass1(c)
        for g in range(ng):
            m_ref[g] = jnp.broadcast_to(jnp.max(m_ref[g], axis=1, keepdims=True), (rows, LANES))
        for c in range(nk):
            pass2(c)
        for g in range(ng):
            l = jnp.sum(l_ref[g], axis=1, keepdims=True)
            og = (acc_ref[g] * (1.0 / l)).astype(BF16)
            for hh in range(HEAD_GROUP):
                h = g * HEAD_GROUP + hh
                o_ref[0, :, h * A_V_DIM:(h + 1) * A_V_DIM] = jnp.dot(
                    og[hh * nq:(hh + 1) * nq], wuv_ref[h],
                    preferred_element_type=F32).astype(o_ref.dtype)

    keep_all = (i + 1) * nq <= topk
    pl.when(keep_all)(functools.partial(block, 1, False))
    for v in range(max_chunks):
        pl.when(jnp.logical_and(n_kc == v + 1, jnp.logical_not(keep_all)))(
            functools.partial(block, v + 1, True))


def _dsa(qlat, qidx, small, ckv, kidx, w_uv):
    B, L, _ = ckv.shape
    topk = min(TOPK_MAX, L // 4)
    kc = min(KEY_CHUNK, L)
    assert L % kc == 0 and kc % Q_BLOCK == 0 and topk <= kc
    n_chunks = L // kc
    nb = L // Q_BLOCK
    ng = A_HEADS // HEAD_GROUP
    rows = HEAD_GROUP * Q_BLOCK
    a_width = A_HEADS * A_V_DIM
    return pl.pallas_call(
        functools.partial(_dsa_kernel, topk=topk),
        grid=(B, nb),
        in_specs=[
            pl.BlockSpec((1, A_HEADS, Q_BLOCK, A_KV_RANK), lambda b, i: (b * nb + i, 0, 0, 0)),
            pl.BlockSpec((1, IDX_HEADS, Q_BLOCK, IDX_DIM), lambda b, i: (b * nb + i, 0, 0, 0)),
            pl.BlockSpec((1, Q_BLOCK, SMALL_W), lambda b, i: (b, i, 0)),
            pl.BlockSpec((1, L, A_KV_RANK), lambda b, i: (b, 0, 0)),
            pl.BlockSpec((1, L, IDX_DIM), lambda b, i: (b, 0, 0)),
            _resident(w_uv.shape),
        ],
        out_specs=pl.BlockSpec((1, Q_BLOCK, a_width), lambda b, i: (b, i, 0)),
        out_shape=jax.ShapeDtypeStruct((B, L, a_width), BF16),
        scratch_shapes=[
            pltpu.VMEM((n_chunks, Q_BLOCK, kc), F32),
            pltpu.VMEM((n_chunks, Q_BLOCK, kc), F32),
            pltpu.VMEM((n_chunks, ng, rows, kc), F32),
            pltpu.VMEM((ng, rows, LANES), F32),
            pltpu.VMEM((ng, rows, LANES), F32),
            pltpu.VMEM((ng, rows, A_KV_RANK), F32),
        ],
        compiler_params=pltpu.CompilerParams(
            dimension_semantics=("arbitrary", "arbitrary"), vmem_limit_bytes=VMEM_LIMIT),
        name="dsa",
    )(qlat, qidx, small, ckv, kidx, w_uv)


def _lane_bcast(x, c):
    lane = lax.broadcasted_iota(jnp.int32, x.shape, 1)
    col = jnp.sum(jnp.where(lane == c, x, 0.0), axis=1, keepdims=True)
    return jnp.broadcast_to(col, x.shape)


def _gdn_kernel(q_ref, k_ref, v_ref, z_ref, gates_ref, og_ref, o_ref,
                mneg_ref, r_ref, qeff_ref, o0_ref, at_ref, *, seq, hp):
    hg = pl.program_id(1)
    C = CHUNK
    n_chunks = seq // C
    dk = B_K_DIM

    R = PACK * C
    rr = lax.broadcasted_iota(jnp.int32, (R, R), 0)
    cc = lax.broadcasted_iota(jnp.int32, (R, R), 1)
    same = (rr // C) == (cc // C)
    tri = same & (rr >= cc)
    strict = same & (rr > cc)
    eye = jnp.where(rr == cc, 1.0, 0.0)
    kcol = lax.broadcasted_iota(jnp.int32, (dk, R), 1) // C
    nt = (((1,), (1,)), ((), ()))
    dot = functools.partial(jnp.dot, preferred_element_type=F32)

    def prep(t, _):
        ids = [(hh, t * PREP_GROUP + j) for j in range(PREP_GROUP) for hh in range(hp)]
        each = lambda f, *cols: [f(*a) for a in zip(*cols)]
        rows = [pl.ds(pl.multiple_of(n * R, R), R) for _, n in ids]
        load = lambda ref: [ref[0, r, hh * dk:(hh + 1) * dk] for (hh, _), r in zip(ids, rows)]
        qb, kb, vb = (load(r) for r in (q_ref, k_ref, v_ref))
        q, k, v = (each(lambda a: a.astype(F32), x) for x in (qb, kb, vb))
        gt = [gates_ref[0, r, :] for r in rows]
        beta = [_lane_bcast(a, OFF_BETA + hg * hp + hh) for a, (hh, _) in zip(gt, ids)]
        G = [_lane_bcast(a, OFF_A + hg * hp + hh) for a, (hh, _) in zip(gt, ids)]
        Gl = each(lambda g: jnp.concatenate(
            [jnp.broadcast_to(g[(p + 1) * C - 1:(p + 1) * C, :], (C, LANES)) for p in range(PACK)],
            axis=0), G)
        decay = each(lambda g: jnp.exp(jnp.where(tri, g[:, :R] - g.T[:R, :R], -jnp.inf)), G)
        eG = each(jnp.exp, G)
        qkk = each(lambda q_, k_: lax.dot_general(
            jnp.concatenate([q_, k_], axis=0), k_, nt, preferred_element_type=F32), qb, kb)
        N = each(lambda b, a, d: jnp.where(strict, b[:, :R] * a[R:] * d, 0.0), beta, qkk, decay)
        X = each(lambda a: eye - a, N)
        Nb = each(lambda a: a.astype(BF16), N)
        Pb = each(lambda a: dot(a, a).astype(BF16), Nb)
        steps = C.bit_length() - 2
        for it in range(steps):
            if it + 1 < steps:
                xp = each(lambda x, p: dot(jnp.concatenate([x.astype(BF16), p], axis=0), p), X, Pb)
                X = each(lambda x, a: x + a[:R], X, xp)
                Pb = each(lambda a: a[R:].astype(BF16), xp)
            else:
                X = each(lambda x, p: x + dot(x.astype(BF16), p), X, Pb)
        rhs = each(lambda v_, k_, b, e: jnp.concatenate(
            [v_ * b, k_ * (b * e)], axis=-1).astype(BF16), v, k, beta, eG)
        sol = each(lambda x, r: dot(x.astype(BF16), r).astype(BF16), X, rhs)
        qk = each(lambda a, d: (a[:R] * d).astype(BF16), qkk, decay)
        ktT = each(lambda k_, g, gl: (k_ * jnp.exp(gl - g)).T.astype(BF16), k, G, Gl)
        ktbd = each(lambda a: jnp.concatenate(
            [jnp.where(kcol == p, a, jnp.zeros_like(a)) for p in range(PACK)], axis=0), ktT)
        kts = each(dot, ktbd, sol)
        qks = each(dot, qk, sol)
        for i, (hh, n) in enumerate(ids):
            qe = q[i] * eG[i]
            for p in range(PACK):
                ch = n * PACK + p
                r_ref[hh, ch] = kts[i][p * dk:(p + 1) * dk, :B_V_DIM]
                mneg_ref[hh, ch] = kts[i][p * dk:(p + 1) * dk, B_V_DIM:].astype(BF16)
                o0_ref[hh, ch] = qks[i][p * C:(p + 1) * C, :B_V_DIM]
                qeff_ref[hh, ch] = (qe[p * C:(p + 1) * C]
                                    - qks[i][p * C:(p + 1) * C, B_V_DIM:]).astype(BF16)
                at_ref[hh, ch] = jnp.exp(G[i][(p + 1) * C - 1:(p + 1) * C, :])
        return 0

    lax.fori_loop(0, n_chunks // (PREP_GROUP * PACK), prep, 0)

    og = og_ref[...]

    def emit(n, o):
        rows = pl.ds(n * C if isinstance(n, int) else pl.multiple_of(n * C, C), C)
        for hh in range(hp):
            ls = slice(hh * dk, (hh + 1) * dk)
            o_ref[0, rows, ls] = (_rms(o[hh], og) * _silu(z_ref[0, rows, ls])).astype(o_ref.dtype)

    def scan(n, carry):
        S, o_prev = carry
        emit((n + n_chunks - 1) % n_chunks, o_prev)
        Sb = [s.astype(BF16) for s in S]
        ms = [dot(mneg_ref[hh, n], Sb[hh]) for hh in range(hp)]
        os_ = [dot(qeff_ref[hh, n], Sb[hh]) for hh in range(hp)]
        new = tuple(S[hh] * at_ref[hh, n] - ms[hh] + r_ref[hh, n] for hh in range(hp))
        return new, tuple(os_[hh] + o0_ref[hh, n] for hh in range(hp))

    zeros = lambda rows_: tuple(jnp.zeros((rows_, B_V_DIM), F32) for _ in range(hp))
    _, o_last = lax.fori_loop(0, n_chunks, scan, (zeros(dk), zeros(C)))
    emit(n_chunks - 1, o_last)


def _gdn(qn, kn, vn, zb, gates, onorm_g):
    B, L, W = qn.shape
    hp = GDN_HEADS_PER_STEP
    n_chunks = L // CHUNK
    heads = W // B_K_DIM
    assert heads % hp == 0 and n_chunks % (PREP_GROUP * PACK) == 0 and PACK * CHUNK <= LANES
    col = pl.BlockSpec((1, L, hp * B_K_DIM), lambda b, h: (b, 0, h))
    return pl.pallas_call(
        functools.partial(_gdn_kernel, seq=L, hp=hp),
        grid=(B, heads // hp),
        in_specs=[col, col, col, col,
                  pl.BlockSpec((1, L, SMALL_W), lambda b, h: (b, 0, 0)),
                  _resident((1, B_V_DIM))],
        out_specs=col,
        out_shape=jax.ShapeDtypeStruct((B, L, W), BF16),
        scratch_shapes=[
            pltpu.VMEM((hp, n_chunks, B_K_DIM, B_K_DIM), BF16),
            pltpu.VMEM((hp, n_chunks, B_K_DIM, B_V_DIM), F32),
            pltpu.VMEM((hp, n_chunks, CHUNK, B_K_DIM), BF16),
            pltpu.VMEM((hp, n_chunks, CHUNK, B_V_DIM), F32),
            pltpu.VMEM((hp, n_chunks, 1, LANES), F32),
        ],
        compiler_params=pltpu.CompilerParams(
            dimension_semantics=("arbitrary", "arbitrary"), vmem_limit_bytes=VMEM_LIMIT),
        name="gdn",
    )(qn, kn, vn, zb, gates, onorm_g)


def _merge_kernel(x_ref, oa_ref, ob_ref, ga_ref, gb_ref, wa_ref, wb_ref, wo_ref,
                  fg_ref, wg_ref, wu_ref, wd_ref, og_ref, o_ref, *, ff_chunk, final_norm):
    dot = functools.partial(jnp.dot, preferred_element_type=F32)
    n = x_ref.shape[0] // MERGE_SUB
    subs = [slice(s * n, (s + 1) * n) for s in range(MERGE_SUB)]
    ya = [dot(oa_ref[r, :], wa_ref[...]) for r in subs]
    yb = [dot(ob_ref[r, :], wb_ref[...]) for r in subs]
    merged = [(_sigmoid(ga_ref[r, :]) * a + _sigmoid(gb_ref[r, :]) * b).astype(BF16)
              for r, a, b in zip(subs, ya, yb)]
    acc = [x_ref[r, :] + dot(m, wo_ref[...]) for r, m in zip(subs, merged)]
    h = [_rms(a, fg_ref[...]).astype(BF16) for a in acc]
    d_ff = wg_ref.shape[1]
    for c0 in range(0, d_ff, ff_chunk):
        gate = [dot(v, wg_ref[:, c0:c0 + ff_chunk]) for v in h]
        up = [dot(v, wu_ref[:, c0:c0 + ff_chunk]) for v in h]
        act = [(_silu(g) * u).astype(BF16) for g, u in zip(gate, up)]
        acc = [a + dot(v, wd_ref[c0:c0 + ff_chunk, :]) for a, v in zip(acc, act)]
    for r, a in zip(subs, acc):
        o_ref[r, :] = _rms(a, og_ref[...]) if final_norm else a


def _merge(x2, oa, ob, ga, gb, wa, wb, wo, fg, wg, wu, wd, og, *, tm, final_norm):
    T, D = x2.shape
    d_ff = wg.shape[1]
    ff_chunk = d_ff // 2 if (d_ff // 2) % LANES == 0 else d_ff
    row = pl.BlockSpec((tm, D), lambda i: (i, 0))
    return pl.pallas_call(
        functools.partial(_merge_kernel, ff_chunk=ff_chunk, final_norm=final_norm),
        grid=(T // tm,),
        in_specs=[row, row, row, row, row,
                  _resident(wa.shape), _resident(wb.shape), _resident(wo.shape),
                  _resident(fg.shape), _resident(wg.shape), _resident(wu.shape),
                  _resident(wd.shape), _resident(og.shape)],
        out_specs=row,
        out_shape=jax.ShapeDtypeStruct((T, D), F32),
        compiler_params=pltpu.CompilerParams(
            dimension_semantics=("arbitrary",), vmem_limit_bytes=VMEM_LIMIT),
        name="merge_ffn",
    )(x2, oa, ob, ga, gb, wa, wb, wo, fg, wg, wu, wd, og)


def _reorder_w_in(w):
    sizes = (A_Q_RANK, A_KV_RANK, IDX_DIM, IDX_HEADS,
             B_HEADS * B_K_DIM, B_HEADS * B_K_DIM, B_HEADS * B_V_DIM, B_HEADS, B_HEADS,
             B_HEADS * B_V_DIM, w.shape[0], w.shape[0])
    parts, c0 = [], 0
    for s in sizes:
        parts.append(w[:, c0:c0 + s])
        c0 += s
    (c_q, c_kv, k_idx, w_idx, q_b, k_b, v_b, beta_b, a_b, z_b, gate_a, gate_b) = parts
    pad = jnp.zeros((w.shape[0], SMALL_W - IDX_DIM - IDX_HEADS - 2 * B_HEADS), w.dtype)
    return jnp.concatenate(
        [q_b, k_b, v_b, z_b, gate_a, gate_b, c_q, c_kv, k_idx, w_idx, beta_b, a_b, pad],
        axis=1).astype(BF16)


def kernel(x, mix_norm_g, w_in, cq_norm_g, ckv_norm_g, w_uq, w_uk, w_uv, w_iq, kidx_ln_g, kidx_ln_b, w_branch_a, conv_w, a_log, dt_bias, onorm_g, w_branch_b, w_out, ffn_norm_g, w_gate, w_up, w_down, final_norm_g):
    B, L, D = x.shape
    depth = w_in.shape[0]
    T = B * L
    tm = min(TOKEN_TILE, L)
    x2 = x.reshape(T, D)
    vec = lambda a: a.reshape(1, -1).astype(F32)
    at_a = lambda a: jnp.zeros((1, SMALL_W), F32).at[0, OFF_A:OFF_A + B_HEADS].set(a.astype(F32))
    for l in range(depth):
        (qlat, qidx, ckv, kidx, small, gates, qn, kn, vn, zb, ga, gb) = _in_proj(
            x2, vec(mix_norm_g[l]), _reorder_w_in(w_in[l]), vec(cq_norm_g[l]),
            vec(ckv_norm_g[l]), vec(kidx_ln_g[l]), vec(kidx_ln_b[l]), conv_w[l].astype(F32),
            at_a(a_log[l]), at_a(dt_bias[l]), w_uq[l].astype(BF16), w_uk[l].astype(BF16),
            w_iq[l].astype(BF16), tm=tm, seq=L)
        seq = lambda a: a.reshape(B, L, a.shape[-1])
        o_a = _dsa(qlat, qidx, seq(small), seq(ckv), seq(kidx), w_uv[l].astype(BF16))
        o_b = _gdn(seq(qn), seq(kn), seq(vn), seq(zb), seq(gates), vec(onorm_g[l]))
        x2 = _merge(x2, o_a.reshape(T, -1), o_b.reshape(T, -1), ga, gb,
                    w_branch_a[l].astype(BF16), w_branch_b[l].astype(BF16),
                    w_out[l].astype(BF16), vec(ffn_norm_g[l]), w_gate[l].astype(BF16),
                    w_up[l].astype(BF16), w_down[l].astype(BF16), vec(final_norm_g),
                    tm=min(MERGE_TILE, T), final_norm=(l == depth - 1))
    return x2.reshape(B, L, D)
```

```python
import functools

import jax
import jax.numpy as jnp
from jax import lax
from jax.experimental import pallas as pl
from jax.experimental.pallas import tpu as pltpu

F32 = jnp.float32
BF16 = jnp.bfloat16

EPS = 1e-6
A_HEADS = 16
A_QK_DIM = 64
A_V_DIM = 64
A_Q_RANK = 256
A_KV_RANK = 256
IDX_HEADS = 8
IDX_DIM = 64
TOPK_MAX = 256
Q_BLOCK = 128
B_HEADS = 8
B_K_DIM = 128
B_V_DIM = 128
CONV_WIDTH = 4
CHUNK = 64

GDN_W = B_HEADS * B_K_DIM
assert B_K_DIM == B_V_DIM
LANES = 128
SUBLANES = 8
SMALL_W = LANES
OFF_WIDX = IDX_DIM
OFF_BETA = IDX_DIM + IDX_HEADS
OFF_A = IDX_DIM + IDX_HEADS + B_HEADS
VMEM_LIMIT = 56 * 1024 * 1024
NEG_BIG = -1e30
LOG2_E = 1.4426950408889634
BISECT_STEPS = 16
BISECT_UNROLL = 8
TOKEN_TILE = 512
IN_SUB = 2
MERGE_TILE = 512
MERGE_SUB = 2
HEAD_GROUP = 4
KEY_CHUNK = 512
SEARCH_GROUPS = 4
GDN_HEADS_PER_STEP = 4
PACK = 2
PREP_GROUP = 4


def _resident(shape):
    nd = len(shape)
    return pl.BlockSpec(shape, lambda *_: (0,) * nd, pipeline_mode=pl.Buffered(1))


def _rms(x, g):
    return x * lax.rsqrt(jnp.mean(x * x, axis=-1, keepdims=True) + EPS) * g


def _l2n(x, scale=1.0):
    return x * (lax.rsqrt(jnp.sum(x * x, axis=-1, keepdims=True) + EPS) * scale)


def _sigmoid(x):
    return 0.5 * jnp.tanh(0.5 * x) + 0.5


def _silu(x):
    u = 0.5 * x
    return u + u * jnp.tanh(u)


def _chunk_cumsum(g, chunk):
    pos = lax.broadcasted_iota(jnp.int32, g.shape, 0) % chunk
    d = 1
    while d < chunk:
        g = g + jnp.where(pos >= d, pltpu.roll(g, d, 0), 0.0)
        d *= 2
    return g


def _conv_silu_tile(x, tail_ref, w_half):
    full = jnp.concatenate([tail_ref[...], x], axis=0)
    u = x * w_half[CONV_WIDTH - 1:CONV_WIDTH, :]
    for d in range(1, CONV_WIDTH):
        u = u + pltpu.roll(full, d, 0)[SUBLANES:] * w_half[CONV_WIDTH - 1 - d:CONV_WIDTH - d, :]
    tail_ref[...] = x[x.shape[0] - SUBLANES:]
    return u + u * jnp.tanh(u)


def _in_proj_kernel(x_ref, g_ref, w_ref, cqg_ref, ckvg_ref, lng_ref, lnb_ref, cw_ref,
                    alog_ref, dtb_ref, wuq_ref, wuk_ref, wiq_ref,
                    qlat_ref, qidx_ref, ckv_ref, kidx_ref, small_ref, gates_ref, q_ref, k_ref, v_ref,
                    z_ref, ga_ref, gb_ref, tail_ref, *, d_model, tiles_per_seq):
    @pl.when(pl.program_id(0) % tiles_per_seq == 0)
    def _():
        tail_ref[...] = jnp.zeros_like(tail_ref)

    n = x_ref.shape[0] // IN_SUB
    subs = [slice(s * n, (s + 1) * n) for s in range(IN_SUB)]
    hs = [_rms(x_ref[r, :], g_ref[...]).astype(BF16) for r in subs]

    def proj(s, c0, width):
        return jnp.dot(hs[s], w_ref[:, c0:c0 + width], preferred_element_type=F32)

    pair = 2 * B_K_DIM
    plain, c0 = [], 3 * GDN_W
    for ref, width in ((z_ref, GDN_W), (ga_ref, d_model), (gb_ref, d_model)):
        plain += [(ref, c, c0 + c) for c in range(0, width, pair)]
        c0 += width
    steps = [(part, c) for part in range(3) for c in range(0, GDN_W, pair)]
    for idx, (part, c) in enumerate(steps):
        ref = (q_ref, k_ref, v_ref)[part]
        cc = part * GDN_W + c
        lo, hi = idx * len(plain) // len(steps), (idx + 1) * len(plain) // len(steps)
        for s, r in enumerate(subs):
            y = _conv_silu_tile(proj(s, cc, pair), tail_ref.at[:, cc:cc + pair],
                                cw_ref[:, cc:cc + pair] * 0.5)
            if part < 2:
                scale = B_K_DIM ** -0.5 if part == 0 else 1.0
                y = jnp.concatenate(
                    [_l2n(y[:, :B_K_DIM], scale), _l2n(y[:, B_K_DIM:], scale)], axis=1)
            ref[r, c:c + pair] = y.astype(ref.dtype)
            for pref, pc, wc in plain[lo:hi]:
                pref[r, pc:pc + pair] = proj(s, wc, pair)
    blocks = n // Q_BLOCK
    for s, r in enumerate(subs):
        c1 = c0
        cq = _rms(proj(s, c1, A_Q_RANK), cqg_ref[...]).astype(BF16)
        c1 += A_Q_RANK
        qa = jnp.dot(cq, wuq_ref[...], preferred_element_type=F32).astype(BF16)
        qi = jnp.dot(cq, wiq_ref[...], preferred_element_type=F32).astype(BF16)
        for hd in range(A_HEADS):
            ql = (jnp.dot(qa[:, hd * A_QK_DIM:(hd + 1) * A_QK_DIM], wuk_ref[hd],
                          preferred_element_type=F32) * (A_QK_DIM ** -0.5 * LOG2_E)).astype(BF16)
            for j in range(blocks):
                qlat_ref[s * blocks + j, hd] = ql[j * Q_BLOCK:(j + 1) * Q_BLOCK]
        for hd in range(IDX_HEADS):
            for j in range(blocks):
                qidx_ref[s * blocks + j, hd] = qi[j * Q_BLOCK:(j + 1) * Q_BLOCK,
                                                  hd * IDX_DIM:(hd + 1) * IDX_DIM]
        ckv_ref[r, :] = _rms(proj(s, c1, A_KV_RANK), ckvg_ref[...]).astype(BF16)
        c1 += A_KV_RANK
        small = proj(s, c1, SMALL_W)
        small_ref[r, :] = small
        kraw = small[:, :IDX_DIM]
        mu = jnp.mean(kraw, axis=-1, keepdims=True)
        kc = kraw - mu
        kn = kc * lax.rsqrt(jnp.mean(kc * kc, axis=-1, keepdims=True) + EPS)
        kidx_ref[r, :] = (kn * lng_ref[...] + lnb_ref[...]).astype(BF16)
        g = -jnp.exp(alog_ref[...]) * jax.nn.softplus(small + dtb_ref[...])
        lane = lax.broadcasted_iota(jnp.int32, small.shape, 1)
        gates_ref[r, :] = jnp.where(lane >= OFF_A, _chunk_cumsum(g, CHUNK), _sigmoid(small))


def _in_proj(x2, g, w_all, cq_g, ckv_g, ln_g, ln_b, conv_w, alog_v, dtb_v, w_uq, w_uk, w_iq,
             *, tm, seq):
    T, D = x2.shape
    n_all = w_all.shape[1]
    assert seq % tm == 0 and tm % (IN_SUB * Q_BLOCK) == 0 and Q_BLOCK % CHUNK == 0
    row = lambda w: pl.BlockSpec((tm, w), lambda i: (i, 0))
    qblk = lambda heads, w: pl.BlockSpec((tm // Q_BLOCK, heads, Q_BLOCK, w),
                                         lambda i: (i, 0, 0, 0))
    wide = ((GDN_W, BF16), (GDN_W, BF16), (GDN_W, BF16), (GDN_W, F32),
            (D, F32), (D, F32))
    out_shapes = (
        jax.ShapeDtypeStruct((T // Q_BLOCK, A_HEADS, Q_BLOCK, A_KV_RANK), BF16),
        jax.ShapeDtypeStruct((T // Q_BLOCK, IDX_HEADS, Q_BLOCK, IDX_DIM), BF16),
        jax.ShapeDtypeStruct((T, A_KV_RANK), BF16),
        jax.ShapeDtypeStruct((T, IDX_DIM), BF16),
        jax.ShapeDtypeStruct((T, SMALL_W), F32),
        jax.ShapeDtypeStruct((T, SMALL_W), F32),
    ) + tuple(jax.ShapeDtypeStruct((T, w), dt) for w, dt in wide)
    out_specs = (qblk(A_HEADS, A_KV_RANK), qblk(IDX_HEADS, IDX_DIM), row(A_KV_RANK),
                 row(IDX_DIM), row(SMALL_W), row(SMALL_W)) + tuple(row(w) for w, _ in wide)
    return pl.pallas_call(
        functools.partial(_in_proj_kernel, d_model=D, tiles_per_seq=seq // tm),
        grid=(T // tm,),
        in_specs=[row(D), _resident((1, D)), _resident((D, n_all)),
                  _resident((1, A_Q_RANK)), _resident((1, A_KV_RANK)),
                  _resident((1, IDX_DIM)), _resident((1, IDX_DIM)),
                  _resident(conv_w.shape), _resident((1, SMALL_W)), _resident((1, SMALL_W)),
                  _resident(w_uq.shape), _resident(w_uk.shape), _resident(w_iq.shape)],
        out_specs=out_specs,
        out_shape=out_shapes,
        scratch_shapes=[pltpu.VMEM((SUBLANES, conv_w.shape[1]), F32)],
        compiler_params=pltpu.CompilerParams(
            dimension_semantics=("arbitrary",), vmem_limit_bytes=VMEM_LIMIT),
        name="in_proj",
    )(x2, g, w_all, cq_g, ckv_g, ln_g, ln_b, conv_w, alog_v, dtb_v, w_uq, w_uk, w_iq)


def _topk_bias(score_ref, bias_ref, row0, n_chunks, topk, tri_ones):
    _, nq, kc = score_ref.shape
    rg = nq // SEARCH_GROUPS
    groups = range(SEARCH_GROUPS)
    rsum = lambda x: jnp.sum(x, axis=1, keepdims=True)
    rmin = lambda x: jnp.min(x, axis=1, keepdims=True)

    def ld(g):
        return jnp.concatenate(
            [score_ref[c, g * rg:(g + 1) * rg, :] for c in range(n_chunks)], axis=1)

    pos = lambda g: row0 + g * rg + lax.broadcasted_iota(jnp.int32, (rg, 1), 0)
    k_eff = [jnp.minimum(pos(g) + 1, topk).astype(F32) for g in groups]
    lo = tuple(rmin(jnp.where(ld(g) == -jnp.inf, jnp.inf, ld(g))) for g in groups)
    hi = tuple(jnp.max(ld(g), axis=1, keepdims=True) for g in groups)

    def bisect(_, carry):
        lo, hi = carry
        mid = [0.5 * (a + b) for a, b in zip(lo, hi)]
        cnt = [rsum(jnp.where(ld(g) >= mid[g], 1.0, 0.0)) for g in groups]
        ge = [cnt[g] >= k_eff[g] for g in groups]
        return (tuple(jnp.where(ge[g], mid[g], lo[g]) for g in groups),
                tuple(jnp.where(ge[g], hi[g], mid[g]) for g in groups))

    lo, hi = lax.fori_loop(0, BISECT_STEPS, bisect, (lo, hi), unroll=BISECT_UNROLL)

    def above(g, v):
        s = ld(g)
        gt = s > v
        return rsum(jnp.where(gt, 1.0, 0.0)), rmin(jnp.where(gt, s, jnp.inf))

    v0 = tuple(rmin(jnp.where(ld(g) >= lo[g], ld(g), jnp.inf)) for g in groups)
    first = [above(g, v0[g]) for g in groups]

    def cond(c):
        _, n_gt, _ = c
        flags = [jnp.max(jnp.where(n_gt[g] >= k_eff[g], 1.0, 0.0)) for g in groups]
        return functools.reduce(jnp.maximum, flags) > 0.0

    def body(c):
        v, n_gt, nxt = c
        v = tuple(jnp.where(n_gt[g] >= k_eff[g], nxt[g], v[g]) for g in groups)
        nxt_state = [above(g, v[g]) for g in groups]
        return v, tuple(a for a, _ in nxt_state), tuple(b for _, b in nxt_state)

    tau, n_gt, _ = lax.while_loop(
        cond, body, (v0, tuple(a for a, _ in first), tuple(b for _, b in first)))
    k_all = jnp.concatenate(k_eff, axis=0)
    need = k_all - jnp.concatenate(n_gt, axis=0)
    tau = jnp.concatenate(tau, axis=0)

    n_ge = jnp.zeros((nq, LANES), F32)
    for c in range(n_chunks):
        for j in range(kc // LANES):
            ge = score_ref[c, :, j * LANES:(j + 1) * LANES] >= tau
            bias_ref[c, :, j * LANES:(j + 1) * LANES] = jnp.where(ge, 0.0, NEG_BIG)
            n_ge = n_ge + jnp.where(ge, 1.0, 0.0)
    surplus = jnp.max(jnp.where(rsum(n_ge) > k_all, 1.0, 0.0)) > 0.0

    @pl.when(surplus)
    def _():
        carry = jnp.zeros((nq, LANES), F32)
        for c in range(n_chunks):
            for j in range(kc // LANES):
                sc = score_ref[c, :, j * LANES:(j + 1) * LANES]
                eq = sc == tau
                r = jnp.dot(jnp.where(eq, 1.0, 0.0).astype(BF16), tri_ones,
                            preferred_element_type=F32)
                prefix = r[:, :LANES] + carry
                carry = carry + r[:, LANES:]
                sel = (sc > tau) | (eq & (prefix <= need))
                bias_ref[c, :, j * LANES:(j + 1) * LANES] = jnp.where(sel, 0.0, NEG_BIG)


def _dsa_kernel(qlat_ref, qidx_ref, small_ref, ckv_ref, kidx_ref, wuv_ref,
                o_ref, score_ref, bias_ref, s_ref, m_ref, l_ref, acc_ref, *, topk):
    i = pl.program_id(1)
    nq = Q_BLOCK
    max_chunks, _, kc = score_ref.shape
    n_kc = (i * nq) // kc + 1
    ng = A_HEADS // HEAD_GROUP
    rows = HEAD_GROUP * nq
    tiles = kc // LANES
    nt = (((1,), (1,)), ((), ()))
    w_idx = small_ref[0][:, OFF_WIDX:OFF_WIDX + IDX_HEADS] * (IDX_HEADS ** -0.5 * IDX_DIM ** -0.5)
    row_pos = i * nq + lax.broadcasted_iota(jnp.int32, (nq, 1), 0)

    r_i = lax.broadcasted_iota(jnp.int32, (LANES, 2 * LANES), 0)
    c_i = lax.broadcasted_iota(jnp.int32, (LANES, 2 * LANES), 1)
    tri_ones = jnp.where((c_i >= LANES) | (r_i <= c_i), 1.0, 0.0).astype(BF16)

    def key_rows(c):
        return slice(c * kc, (c + 1) * kc)

    def score_chunk(c):
        logits = lax.dot_general(qidx_ref[0].reshape(IDX_HEADS * nq, IDX_DIM),
                                 kidx_ref[0, key_rows(c), :], nt,
                                 preferred_element_type=F32)
        sc = jnp.zeros((nq, kc), F32)
        for h in range(IDX_HEADS):
            sc = sc + w_idx[:, h:h + 1] * jnp.maximum(logits[h * nq:(h + 1) * nq], 0.0)
        col = c * kc + lax.broadcasted_iota(jnp.int32, (nq, kc), 1)
        score_ref[c] = jnp.where(col <= row_pos, sc, -jnp.inf)

    def pass1(c):
        ckv = ckv_ref[0, key_rows(c), :]
        bias = bias_ref[c]
        ss = [lax.dot_general(
            qlat_ref[0, g * HEAD_GROUP:(g + 1) * HEAD_GROUP].reshape(rows, A_KV_RANK), ckv, nt,
            preferred_element_type=F32) for g in range(ng)]
        for g in range(ng):
            s = (ss[g].reshape(HEAD_GROUP, nq, kc) + bias[None]).reshape(rows, kc)
            s_ref[c, g] = s
            mt = s[:, :LANES]
            for j in range(1, tiles):
                mt = jnp.maximum(mt, s[:, j * LANES:(j + 1) * LANES])
            m_ref[g] = mt if c == 0 else jnp.maximum(m_ref[g], mt)

    def pass2(c):
        ckv = ckv_ref[0, key_rows(c), :]
        for g in range(ng):
            s = s_ref[c, g]
            m = m_ref[g]
            p = [jnp.exp2(s[:, j * LANES:(j + 1) * LANES] - m) for j in range(tiles)]
            lsum = functools.reduce(lambda a, b: a + b, p)
            pv = jnp.dot(jnp.concatenate(p, axis=1).astype(BF16), ckv,
                         preferred_element_type=F32)
            l_ref[g] = lsum if c == 0 else l_ref[g] + lsum
            acc_ref[g] = pv if c == 0 else acc_ref[g] + pv

    def block(nk, search):
        for c in range(nk):
            score_chunk(c)
        if search:
            _topk_bias(score_ref, bias_ref, i * nq, nk, topk, tri_ones)
        else:
            bias_ref[0] = jnp.where(score_ref[0] > -jnp.inf, 0.0, NEG_BIG)
        for c in range(nk):
            pass1(c)
        for g in range(ng):
            m_ref[g] = jnp.broadcast_to(jnp.max(m_ref[g], axis=1, keepdims=True), (rows, LANES))
        for c in range(nk):
            pass2(c)
        for g in range(ng):
            l = jnp.sum(l_ref[g], axis=1, keepdims=True)
            og = (acc_ref[g] * (1.0 / l)).astype(BF16)
            for hh in range(HEAD_GROUP):
                h = g * HEAD_GROUP + hh
                o_ref[0, :, h * A_V_DIM:(h + 1) * A_V_DIM] = jnp.dot(
                    og[hh * nq:(hh + 1) * nq], wuv_ref[h],
                    preferred_element_type=F32).astype(o_ref.dtype)

    keep_all = (i + 1) * nq <= topk
    pl.when(keep_all)(functools.partial(block, 1, False))
    for v in range(max_chunks):
        pl.when(jnp.logical_and(n_kc == v + 1, jnp.logical_not(keep_all)))(
            functools.partial(block, v + 1, True))


def _dsa(qlat, qidx, small, ckv, kidx, w_uv):
    B, L, _ = ckv.shape
    topk = min(TOPK_MAX, L // 4)
    kc = min(KEY_CHUNK, L)
    assert L % kc == 0 and kc % Q_BLOCK == 0 and topk <= kc
    n_chunks = L // kc
    nb = L // Q_BLOCK
    ng = A_HEADS // HEAD_GROUP
    rows = HEAD_GROUP * Q_BLOCK
    a_width = A_HEADS * A_V_DIM
    return pl.pallas_call(
        functools.partial(_dsa_kernel, topk=topk),
        grid=(B, nb),
        in_specs=[
            pl.BlockSpec((1, A_HEADS, Q_BLOCK, A_KV_RANK), lambda b, i: (b * nb + i, 0, 0, 0)),
            pl.BlockSpec((1, IDX_HEADS, Q_BLOCK, IDX_DIM), lambda b, i: (b * nb + i, 0, 0, 0)),
            pl.BlockSpec((1, Q_BLOCK, SMALL_W), lambda b, i: (b, i, 0)),
            pl.BlockSpec((1, L, A_KV_RANK), lambda b, i: (b, 0, 0)),
            pl.BlockSpec((1, L, IDX_DIM), lambda b, i: (b, 0, 0)),
            _resident(w_uv.shape),
        ],
        out_specs=pl.BlockSpec((1, Q_BLOCK, a_width), lambda b, i: (b, i, 0)),
        out_shape=jax.ShapeDtypeStruct((B, L, a_width), BF16),
        scratch_shapes=[
            pltpu.VMEM((n_chunks, Q_BLOCK, kc), F32),
            pltpu.VMEM((n_chunks, Q_BLOCK, kc), F32),
            pltpu.VMEM((n_chunks, ng, rows, kc), F32),
            pltpu.VMEM((ng, rows, LANES), F32),
            pltpu.VMEM((ng, rows, LANES), F32),
            pltpu.VMEM((ng, rows, A_KV_RANK), F32),
        ],
        compiler_params=pltpu.CompilerParams(
            dimension_semantics=("arbitrary", "arbitrary"), vmem_limit_bytes=VMEM_LIMIT),
        name="dsa",
    )(qlat, qidx, small, ckv, kidx, w_uv)


def _lane_bcast(x, c):
    lane = lax.broadcasted_iota(jnp.int32, x.shape, 1)
    col = jnp.sum(jnp.where(lane == c, x, 0.0), axis=1, keepdims=True)
    return jnp.broadcast_to(col, x.shape)


def _gdn_kernel(q_ref, k_ref, v_ref, z_ref, gates_ref, og_ref, o_ref,
                mneg_ref, r_ref, qeff_ref, o0_ref, at_ref, *, seq, hp):
    hg = pl.program_id(1)
    C = CHUNK
    n_chunks = seq // C
    dk = B_K_DIM

    R = PACK * C
    rr = lax.broadcasted_iota(jnp.int32, (R, R), 0)
    cc = lax.broadcasted_iota(jnp.int32, (R, R), 1)
    same = (rr // C) == (cc // C)
    tri = same & (rr >= cc)
    strict = same & (rr > cc)
    eye = jnp.where(rr == cc, 1.0, 0.0)
    kcol = lax.broadcasted_iota(jnp.int32, (dk, R), 1) // C
    nt = (((1,), (1,)), ((), ()))
    dot = functools.partial(jnp.dot, preferred_element_type=F32)

    def prep(t, _):
        ids = [(hh, t * PREP_GROUP + j) for j in range(PREP_GROUP) for hh in range(hp)]
        each = lambda f, *cols: [f(*a) for a in zip(*cols)]
        rows = [pl.ds(pl.multiple_of(n * R, R), R) for _, n in ids]
        load = lambda ref: [ref[0, r, hh * dk:(hh + 1) * dk] for (hh, _), r in zip(ids, rows)]
        qb, kb, vb = (load(r) for r in (q_ref, k_ref, v_ref))
        q, k, v = (each(lambda a: a.astype(F32), x) for x in (qb, kb, vb))
        gt = [gates_ref[0, r, :] for r in rows]
        beta = [_lane_bcast(a, OFF_BETA + hg * hp + hh) for a, (hh, _) in zip(gt, ids)]
        G = [_lane_bcast(a, OFF_A + hg * hp + hh) for a, (hh, _) in zip(gt, ids)]
        Gl = each(lambda g: jnp.concatenate(
            [jnp.broadcast_to(g[(p + 1) * C - 1:(p + 1) * C, :], (C, LANES)) for p in range(PACK)],
            axis=0), G)
        decay = each(lambda g: jnp.exp(jnp.where(tri, g[:, :R] - g.T[:R, :R], -jnp.inf)), G)
        eG = each(jnp.exp, G)
        qkk = each(lambda q_, k_: lax.dot_general(
            jnp.concatenate([q_, k_], axis=0), k_, nt, preferred_element_type=F32), qb, kb)
        N = each(lambda b, a, d: jnp.where(strict, b[:, :R] * a[R:] * d, 0.0), beta, qkk, decay)
        X = each(lambda a: eye - a, N)
        Nb = each(lambda a: a.astype(BF16), N)
        Pb = each(lambda a: dot(a, a).astype(BF16), Nb)
        steps = C.bit_length() - 2
        for it in range(steps):
            if it + 1 < steps:
                xp = each(lambda x, p: dot(jnp.concatenate([x.astype(BF16), p], axis=0), p), X, Pb)
                X = each(lambda x, a: x + a[:R], X, xp)
                Pb = each(lambda a: a[R:].astype(BF16), xp)
            else:
                X = each(lambda x, p: x + dot(x.astype(BF16), p), X, Pb)
        rhs = each(lambda v_, k_, b, e: jnp.concatenate(
            [v_ * b, k_ * (b * e)], axis=-1).astype(BF16), v, k, beta, eG)
        sol = each(lambda x, r: dot(x.astype(BF16), r).astype(BF16), X, rhs)
        qk = each(lambda a, d: (a[:R] * d).astype(BF16), qkk, decay)
        ktT = each(lambda k_, g, gl: (k_ * jnp.exp(gl - g)).T.astype(BF16), k, G, Gl)
        ktbd = each(lambda a: jnp.concatenate(
            [jnp.where(kcol == p, a, jnp.zeros_like(a)) for p in range(PACK)], axis=0), ktT)
        kts = each(dot, ktbd, sol)
        qks = each(dot, qk, sol)
        for i, (hh, n) in enumerate(ids):
            qe = q[i] * eG[i]
            for p in range(PACK):
                ch = n * PACK + p
                r_ref[hh, ch] = kts[i][p * dk:(p + 1) * dk, :B_V_DIM]
                mneg_ref[hh, ch] = kts[i][p * dk:(p + 1) * dk, B_V_DIM:].astype(BF16)
                o0_ref[hh, ch] = qks[i][p * C:(p + 1) * C, :B_V_DIM]
                qeff_ref[hh, ch] = (qe[p * C:(p + 1) * C]
                                    - qks[i][p * C:(p + 1) * C, B_V_DIM:]).astype(BF16)
                at_ref[hh, ch] = jnp.exp(G[i][(p + 1) * C - 1:(p + 1) * C, :])
        return 0

    lax.fori_loop(0, n_chunks // (PREP_GROUP * PACK), prep, 0)

    og = og_ref[...]

    def emit(n, o):
        rows = pl.ds(n * C if isinstance(n, int) else pl.multiple_of(n * C, C), C)
        for hh in range(hp):
            ls = slice(hh * dk, (hh + 1) * dk)
            o_ref[0, rows, ls] = (_rms(o[hh], og) * _silu(z_ref[0, rows, ls])).astype(o_ref.dtype)

    def scan(n, carry):
        S, o_prev = carry
        emit((n + n_chunks - 1) % n_chunks, o_prev)
        Sb = [s.astype(BF16) for s in S]
        ms = [dot(mneg_ref[hh, n], Sb[hh]) for hh in range(hp)]
        os_ = [dot(qeff_ref[hh, n], Sb[hh]) for hh in range(hp)]
        new = tuple(S[hh] * at_ref[hh, n] - ms[hh] + r_ref[hh, n] for hh in range(hp))
        return new, tuple(os_[hh] + o0_ref[hh, n] for hh in range(hp))

    zeros = lambda rows_: tuple(jnp.zeros((rows_, B_V_DIM), F32) for _ in range(hp))
    _, o_last = lax.fori_loop(0, n_chunks, scan, (zeros(dk), zeros(C)))
    emit(n_chunks - 1, o_last)


def _gdn(qn, kn, vn, zb, gates, onorm_g):
    B, L, W = qn.shape
    hp = GDN_HEADS_PER_STEP
    n_chunks = L // CHUNK
    heads = W // B_K_DIM
    assert heads % hp == 0 and n_chunks % (PREP_GROUP * PACK) == 0 and PACK * CHUNK <= LANES
    col = pl.BlockSpec((1, L, hp * B_K_DIM), lambda b, h: (b, 0, h))
    return pl.pallas_call(
        functools.partial(_gdn_kernel, seq=L, hp=hp),
        grid=(B, heads // hp),
        in_specs=[col, col, col, col,
                  pl.BlockSpec((1, L, SMALL_W), lambda b, h: (b, 0, 0)),
                  _resident((1, B_V_DIM))],
        out_specs=col,
        out_shape=jax.ShapeDtypeStruct((B, L, W), BF16),
        scratch_shapes=[
            pltpu.VMEM((hp, n_chunks, B_K_DIM, B_K_DIM), BF16),
            pltpu.VMEM((hp, n_chunks, B_K_DIM, B_V_DIM), F32),
            pltpu.VMEM((hp, n_chunks, CHUNK, B_K_DIM), BF16),
            pltpu.VMEM((hp, n_chunks, CHUNK, B_V_DIM), F32),
            pltpu.VMEM((hp, n_chunks, 1, LANES), F32),
        ],
        compiler_params=pltpu.CompilerParams(
            dimension_semantics=("arbitrary", "arbitrary"), vmem_limit_bytes=VMEM_LIMIT),
        name="gdn",
    )(qn, kn, vn, zb, gates, onorm_g)


def _merge_kernel(x_ref, oa_ref, ob_ref, ga_ref, gb_ref, wa_ref, wb_ref, wo_ref,
                  fg_ref, wg_ref, wu_ref, wd_ref, og_ref, o_ref, *, ff_chunk, final_norm):
    dot = functools.partial(jnp.dot, preferred_element_type=F32)
    n = x_ref.shape[0] // MERGE_SUB
    subs = [slice(s * n, (s + 1) * n) for s in range(MERGE_SUB)]
    ya = [dot(oa_ref[r, :], wa_ref[...]) for r in subs]
    yb = [dot(ob_ref[r, :], wb_ref[...]) for r in subs]
    merged = [(_sigmoid(ga_ref[r, :]) * a + _sigmoid(gb_ref[r, :]) * b).astype(BF16)
              for r, a, b in zip(subs, ya, yb)]
    acc = [x_ref[r, :] + dot(m, wo_ref[...]) for r, m in zip(subs, merged)]
    h = [_rms(a, fg_ref[...]).astype(BF16) for a in acc]
    d_ff = wg_ref.shape[1]
    for c0 in range(0, d_ff, ff_chunk):
        gate = [dot(v, wg_ref[:, c0:c0 + ff_chunk]) for v in h]
        up = [dot(v, wu_ref[:, c0:c0 + ff_chunk]) for v in h]
        act = [(_silu(g) * u).astype(BF16) for g, u in zip(gate, up)]
        acc = [a + dot(v, wd_ref[c0:c0 + ff_chunk, :]) for a, v in zip(acc, act)]
    for r, a in zip(subs, acc):
        o_ref[r, :] = _rms(a, og_ref[...]) if final_norm else a


def _merge(x2, oa, ob, ga, gb, wa, wb, wo, fg, wg, wu, wd, og, *, tm, final_norm):
    T, D = x2.shape
    d_ff = wg.shape[1]
    ff_chunk = d_ff // 2 if (d_ff // 2) % LANES == 0 else d_ff
    row = pl.BlockSpec((tm, D), lambda i: (i, 0))
    return pl.pallas_call(
        functools.partial(_merge_kernel, ff_chunk=ff_chunk, final_norm=final_norm),
        grid=(T // tm,),
        in_specs=[row, row, row, row, row,
                  _resident(wa.shape), _resident(wb.shape), _resident(wo.shape),
                  _resident(fg.shape), _resident(wg.shape), _resident(wu.shape),
                  _resident(wd.shape), _resident(og.shape)],
        out_specs=row,
        out_shape=jax.ShapeDtypeStruct((T, D), F32),
        compiler_params=pltpu.CompilerParams(
            dimension_semantics=("arbitrary",), vmem_limit_bytes=VMEM_LIMIT),
        name="merge_ffn",
    )(x2, oa, ob, ga, gb, wa, wb, wo, fg, wg, wu, wd, og)


def _reorder_w_in(w):
    sizes = (A_Q_RANK, A_KV_RANK, IDX_DIM, IDX_HEADS,
             B_HEADS * B_K_DIM, B_HEADS * B_K_DIM, B_HEADS * B_V_DIM, B_HEADS, B_HEADS,
             B_HEADS * B_V_DIM, w.shape[0], w.shape[0])
    parts, c0 = [], 0
    for s in sizes:
        parts.append(w[:, c0:c0 + s])
        c0 += s
    (c_q, c_kv, k_idx, w_idx, q_b, k_b, v_b, beta_b, a_b, z_b, gate_a, gate_b) = parts
    pad = jnp.zeros((w.shape[0], SMALL_W - IDX_DIM - IDX_HEADS - 2 * B_HEADS), w.dtype)
    return jnp.concatenate(
        [q_b, k_b, v_b, z_b, gate_a, gate_b, c_q, c_kv, k_idx, w_idx, beta_b, a_b, pad],
        axis=1).astype(BF16)


def kernel(x, mix_norm_g, w_in, cq_norm_g, ckv_norm_g, w_uq, w_uk, w_uv, w_iq, kidx_ln_g, kidx_ln_b, w_branch_a, conv_w, a_log, dt_bias, onorm_g, w_branch_b, w_out, ffn_norm_g, w_gate, w_up, w_down, final_norm_g):
    B, L, D = x.shape
    depth = w_in.shape[0]
    T = B * L
    tm = min(TOKEN_TILE, L)
    x2 = x.reshape(T, D)
    vec = lambda a: a.reshape(1, -1).astype(F32)
    at_a = lambda a: jnp.zeros((1, SMALL_W), F32).at[0, OFF_A:OFF_A + B_HEADS].set(a.astype(F32))
    for l in range(depth):
        (qlat, qidx, ckv, kidx, small, gates, qn, kn, vn, zb, ga, gb) = _in_proj(
            x2, vec(mix_norm_g[l]), _reorder_w_in(w_in[l]), vec(cq_norm_g[l]),
            vec(ckv_norm_g[l]), vec(kidx_ln_g[l]), vec(kidx_ln_b[l]), conv_w[l].astype(F32),
            at_a(a_log[l]), at_a(dt_bias[l]), w_uq[l].astype(BF16), w_uk[l].astype(BF16),
            w_iq[l].astype(BF16), tm=tm, seq=L)
        seq = lambda a: a.reshape(B, L, a.shape[-1])
        o_a = _dsa(qlat, qidx, seq(small), seq(ckv), seq(kidx), w_uv[l].astype(BF16))
        o_b = _gdn(seq(qn), seq(kn), seq(vn), seq(zb), seq(gates), vec(onorm_g[l]))
        x2 = _merge(x2, o_a.reshape(T, -1), o_b.reshape(T, -1), ga, gb,
                    w_branch_a[l].astype(BF16), w_branch_b[l].astype(BF16),
                    w_out[l].astype(BF16), vec(ffn_norm_g[l]), w_gate[l].astype(BF16),
                    w_up[l].astype(BF16), w_down[l].astype(BF16), vec(final_norm_g),
                    tm=min(MERGE_TILE, T), final_norm=(l == depth - 1))
    return x2.reshape(B, L, D)
```

```python
import functools

import jax
import jax.numpy as jnp
from jax import lax
from jax.experimental import pallas as pl
from jax.experimental.pallas import tpu as pltpu

F32 = jnp.float32
BF16 = jnp.bfloat16

EPS = 1e-6
A_HEADS = 16
A_QK_DIM = 64
A_V_DIM = 64
A_Q_RANK = 256
A_KV_RANK = 256
IDX_HEADS = 8
IDX_DIM = 64
TOPK_MAX = 256
Q_BLOCK = 128
B_HEADS = 8
B_K_DIM = 128
B_V_DIM = 128
CONV_WIDTH = 4
CHUNK = 64

GDN_W = B_HEADS * B_K_DIM
assert B_K_DIM == B_V_DIM
LANES = 128
SUBLANES = 8
SMALL_W = LANES
OFF_WIDX = IDX_DIM
OFF_BETA = IDX_DIM + IDX_HEADS
OFF_A = IDX_DIM + IDX_HEADS + B_HEADS
VMEM_LIMIT = 56 * 1024 * 1024
NEG_BIG = -1e30
LOG2_E = 1.4426950408889634
BISECT_STEPS = 16
BISECT_UNROLL = 8
TOKEN_TILE = 512
IN_SUB = 2
MERGE_TILE = 512
MERGE_SUB = 2
FF_CHUNK = 256
HEAD_GROUP = 4
KEY_CHUNK = 512
SEARCH_GROUPS = 4
GDN_HEADS_PER_STEP = 4
PACK = 2
PREP_GROUP = 4


def _resident(shape):
    nd = len(shape)
    return pl.BlockSpec(shape, lambda *_: (0,) * nd, pipeline_mode=pl.Buffered(1))


def _rms(x, g):
    return x * lax.rsqrt(jnp.mean(x * x, axis=-1, keepdims=True) + EPS) * g


def _l2n(x, scale=1.0):
    return x * (lax.rsqrt(jnp.sum(x * x, axis=-1, keepdims=True) + EPS) * scale)


def _sigmoid(x):
    return 0.5 * jnp.tanh(0.5 * x) + 0.5


def _silu(x):
    u = 0.5 * x
    return u + u * jnp.tanh(u)


def _chunk_cumsum(g, chunk):
    pos = lax.broadcasted_iota(jnp.int32, g.shape, 0) % chunk
    d = 1
    while d < chunk:
        g = g + jnp.where(pos >= d, pltpu.roll(g, d, 0), 0.0)
        d *= 2
    return g


def _conv_silu_tile(x, tail_ref, w_half):
    full = jnp.concatenate([tail_ref[...], x], axis=0)
    u = x * w_half[CONV_WIDTH - 1:CONV_WIDTH, :]
    for d in range(1, CONV_WIDTH):
        u = u + pltpu.roll(full, d, 0)[SUBLANES:] * w_half[CONV_WIDTH - 1 - d:CONV_WIDTH - d, :]
    tail_ref[...] = x[x.shape[0] - SUBLANES:]
    return u + u * jnp.tanh(u)


def _in_proj_kernel(x_ref, g_ref, w_ref, cqg_ref, ckvg_ref, lng_ref, lnb_ref, cw_ref,
                    alog_ref, dtb_ref, wuq_ref, wuk_ref, wiq_ref,
                    qlat_ref, qidx_ref, ckv_ref, kidx_ref, small_ref, gates_ref, q_ref, k_ref, v_ref,
                    z_ref, ga_ref, gb_ref, tail_ref, *, d_model, tiles_per_seq):
    @pl.when(pl.program_id(0) % tiles_per_seq == 0)
    def _():
        tail_ref[...] = jnp.zeros_like(tail_ref)

    n = x_ref.shape[0] // IN_SUB
    subs = [slice(s * n, (s + 1) * n) for s in range(IN_SUB)]
    hs = [_rms(x_ref[r, :], g_ref[...]).astype(BF16) for r in subs]

    def proj(s, c0, width):
        return jnp.dot(hs[s], w_ref[:, c0:c0 + width], preferred_element_type=F32)

    pair = 2 * B_K_DIM
    plain, c0 = [], 3 * GDN_W
    for ref, width in ((z_ref, GDN_W), (ga_ref, d_model), (gb_ref, d_model)):
        plain += [(ref, c, c0 + c) for c in range(0, width, pair)]
        c0 += width
    steps = [(part, c) for part in range(3) for c in range(0, GDN_W, pair)]
    for idx, (part, c) in enumerate(steps):
        ref = (q_ref, k_ref, v_ref)[part]
        cc = part * GDN_W + c
        lo, hi = idx * len(plain) // len(steps), (idx + 1) * len(plain) // len(steps)
        for s, r in enumerate(subs):
            y = _conv_silu_tile(proj(s, cc, pair), tail_ref.at[:, cc:cc + pair],
                                cw_ref[:, cc:cc + pair] * 0.5)
            if part < 2:
                scale = B_K_DIM ** -0.5 if part == 0 else 1.0
                y = jnp.concatenate(
                    [_l2n(y[:, :B_K_DIM], scale), _l2n(y[:, B_K_DIM:], scale)], axis=1)
            ref[r, c:c + pair] = y.astype(ref.dtype)
            for pref, pc, wc in plain[lo:hi]:
                pref[r, pc:pc + pair] = proj(s, wc, pair)
    blocks = n // Q_BLOCK
    for s, r in enumerate(subs):
        c1 = c0
        cq = _rms(proj(s, c1, A_Q_RANK), cqg_ref[...]).astype(BF16)
        c1 += A_Q_RANK
        qa = jnp.dot(cq, wuq_ref[...], preferred_element_type=F32).astype(BF16)
        qi = jnp.dot(cq, wiq_ref[...], preferred_element_type=F32).astype(BF16)
        for hd in range(A_HEADS):
            ql = (jnp.dot(qa[:, hd * A_QK_DIM:(hd + 1) * A_QK_DIM], wuk_ref[hd],
                          preferred_element_type=F32) * (A_QK_DIM ** -0.5 * LOG2_E)).astype(BF16)
            for j in range(blocks):
                qlat_ref[s * blocks + j, hd] = ql[j * Q_BLOCK:(j + 1) * Q_BLOCK]
        for hd in range(IDX_HEADS):
            for j in range(blocks):
                qidx_ref[s * blocks + j, hd] = qi[j * Q_BLOCK:(j + 1) * Q_BLOCK,
                                                  hd * IDX_DIM:(hd + 1) * IDX_DIM]
        ckv_ref[r, :] = _rms(proj(s, c1, A_KV_RANK), ckvg_ref[...]).astype(BF16)
        c1 += A_KV_RANK
        small = proj(s, c1, SMALL_W)
        small_ref[r, :] = small
        kraw = small[:, :IDX_DIM]
        mu = jnp.mean(kraw, axis=-1, keepdims=True)
        kc = kraw - mu
        kn = kc * lax.rsqrt(jnp.mean(kc * kc, axis=-1, keepdims=True) + EPS)
        kidx_ref[r, :] = (kn * lng_ref[...] + lnb_ref[...]).astype(BF16)
        g = -jnp.exp(alog_ref[...]) * jax.nn.softplus(small + dtb_ref[...])
        lane = lax.broadcasted_iota(jnp.int32, small.shape, 1)
        gates_ref[r, :] = jnp.where(lane >= OFF_A, _chunk_cumsum(g, CHUNK), _sigmoid(small))


def _in_proj(x2, g, w_all, cq_g, ckv_g, ln_g, ln_b, conv_w, alog_v, dtb_v, w_uq, w_uk, w_iq,
             *, tm, seq):
    T, D = x2.shape
    n_all = w_all.shape[1]
    assert seq % tm == 0 and tm % (IN_SUB * Q_BLOCK) == 0 and Q_BLOCK % CHUNK == 0
    row = lambda w: pl.BlockSpec((tm, w), lambda i: (i, 0))
    qblk = lambda heads, w: pl.BlockSpec((tm // Q_BLOCK, heads, Q_BLOCK, w),
                                         lambda i: (i, 0, 0, 0))
    wide = ((GDN_W, BF16), (GDN_W, BF16), (GDN_W, BF16), (GDN_W, F32),
            (D, F32), (D, F32))
    out_shapes = (
        jax.ShapeDtypeStruct((T // Q_BLOCK, A_HEADS, Q_BLOCK, A_KV_RANK), BF16),
        jax.ShapeDtypeStruct((T // Q_BLOCK, IDX_HEADS, Q_BLOCK, IDX_DIM), BF16),
        jax.ShapeDtypeStruct((T, A_KV_RANK), BF16),
        jax.ShapeDtypeStruct((T, IDX_DIM), BF16),
        jax.ShapeDtypeStruct((T, SMALL_W), F32),
        jax.ShapeDtypeStruct((T, SMALL_W), F32),
    ) + tuple(jax.ShapeDtypeStruct((T, w), dt) for w, dt in wide)
    out_specs = (qblk(A_HEADS, A_KV_RANK), qblk(IDX_HEADS, IDX_DIM), row(A_KV_RANK),
                 row(IDX_DIM), row(SMALL_W), row(SMALL_W)) + tuple(row(w) for w, _ in wide)
    return pl.pallas_call(
        functools.partial(_in_proj_kernel, d_model=D, tiles_per_seq=seq // tm),
        grid=(T // tm,),
        in_specs=[row(D), _resident((1, D)), _resident((D, n_all)),
                  _resident((1, A_Q_RANK)), _resident((1, A_KV_RANK)),
                  _resident((1, IDX_DIM)), _resident((1, IDX_DIM)),
                  _resident(conv_w.shape), _resident((1, SMALL_W)), _resident((1, SMALL_W)),
                  _resident(w_uq.shape), _resident(w_uk.shape), _resident(w_iq.shape)],
        out_specs=out_specs,
        out_shape=out_shapes,
        scratch_shapes=[pltpu.VMEM((SUBLANES, conv_w.shape[1]), F32)],
        compiler_params=pltpu.CompilerParams(
            dimension_semantics=("arbitrary",), vmem_limit_bytes=VMEM_LIMIT),
        name="in_proj",
    )(x2, g, w_all, cq_g, ckv_g, ln_g, ln_b, conv_w, alog_v, dtb_v, w_uq, w_uk, w_iq)


def _topk_bias(score_ref, bias_ref, row0, n_chunks, topk, tri_ones):
    _, nq, kc = score_ref.shape
    rg = nq // SEARCH_GROUPS
    groups = range(SEARCH_GROUPS)
    rsum = lambda x: jnp.sum(x, axis=1, keepdims=True)
    rmin = lambda x: jnp.min(x, axis=1, keepdims=True)

    def ld(g):
        return jnp.concatenate(
            [score_ref[c, g * rg:(g + 1) * rg, :] for c in range(n_chunks)], axis=1)

    pos = lambda g: row0 + g * rg + lax.broadcasted_iota(jnp.int32, (rg, 1), 0)
    k_eff = [jnp.minimum(pos(g) + 1, topk).astype(F32) for g in groups]
    lo = tuple(rmin(jnp.where(ld(g) == -jnp.inf, jnp.inf, ld(g))) for g in groups)
    hi = tuple(jnp.max(ld(g), axis=1, keepdims=True) for g in groups)

    def bisect(_, carry):
        lo, hi = carry
        mid = [0.5 * (a + b) for a, b in zip(lo, hi)]
        cnt = [rsum(jnp.where(ld(g) >= mid[g], 1.0, 0.0)) for g in groups]
        ge = [cnt[g] >= k_eff[g] for g in groups]
        return (tuple(jnp.where(ge[g], mid[g], lo[g]) for g in groups),
                tuple(jnp.where(ge[g], hi[g], mid[g]) for g in groups))

    lo, hi = lax.fori_loop(0, BISECT_STEPS, bisect, (lo, hi), unroll=BISECT_UNROLL)

    def above(g, v):
        s = ld(g)
        gt = s > v
        return rsum(jnp.where(gt, 1.0, 0.0)), rmin(jnp.where(gt, s, jnp.inf))

    v0 = tuple(rmin(jnp.where(ld(g) >= lo[g], ld(g), jnp.inf)) for g in groups)
    first = [above(g, v0[g]) for g in groups]

    def cond(c):
        _, n_gt, _ = c
        flags = [jnp.max(jnp.where(n_gt[g] >= k_eff[g], 1.0, 0.0)) for g in groups]
        return functools.reduce(jnp.maximum, flags) > 0.0

    def body(c):
        v, n_gt, nxt = c
        v = tuple(jnp.where(n_gt[g] >= k_eff[g], nxt[g], v[g]) for g in groups)
        nxt_state = [above(g, v[g]) for g in groups]
        return v, tuple(a for a, _ in nxt_state), tuple(b for _, b in nxt_state)

    tau, n_gt, _ = lax.while_loop(
        cond, body, (v0, tuple(a for a, _ in first), tuple(b for _, b in first)))
    k_all = jnp.concatenate(k_eff, axis=0)
    need = k_all - jnp.concatenate(n_gt, axis=0)
    tau = jnp.concatenate(tau, axis=0)

    n_ge = jnp.zeros((nq, LANES), F32)
    for c in range(n_chunks):
        for j in range(kc // LANES):
            ge = score_ref[c, :, j * LANES:(j + 1) * LANES] >= tau
            bias_ref[c, :, j * LANES:(j + 1) * LANES] = jnp.where(ge, 0.0, NEG_BIG)
            n_ge = n_ge + jnp.where(ge, 1.0, 0.0)
    surplus = jnp.max(jnp.where(rsum(n_ge) > k_all, 1.0, 0.0)) > 0.0

    @pl.when(surplus)
    def _():
        carry = jnp.zeros((nq, LANES), F32)
        for c in range(n_chunks):
            for j in range(kc // LANES):
                sc = score_ref[c, :, j * LANES:(j + 1) * LANES]
                eq = sc == tau
                r = jnp.dot(jnp.where(eq, 1.0, 0.0).astype(BF16), tri_ones,
                            preferred_element_type=F32)
                prefix = r[:, :LANES] + carry
                carry = carry + r[:, LANES:]
                sel = (sc > tau) | (eq & (prefix <= need))
                bias_ref[c, :, j * LANES:(j + 1) * LANES] = jnp.where(sel, 0.0, NEG_BIG)


def _dsa_kernel(qlat_ref, qidx_ref, small_ref, ckv_ref, kidx_ref, wuv_ref,
                o_ref, score_ref, bias_ref, s_ref, m_ref, l_ref, acc_ref, *, topk):
    i = pl.program_id(1)
    nq = Q_BLOCK
    max_chunks, _, kc = score_ref.shape
    n_kc = (i * nq) // kc + 1
    ng = A_HEADS // HEAD_GROUP
    rows = HEAD_GROUP * nq
    tiles = kc // LANES
    nt = (((1,), (1,)), ((), ()))
    w_idx = small_ref[0][:, OFF_WIDX:OFF_WIDX + IDX_HEADS] * (IDX_HEADS ** -0.5 * IDX_DIM ** -0.5)
    row_pos = i * nq + lax.broadcasted_iota(jnp.int32, (nq, 1), 0)

    r_i = lax.broadcasted_iota(jnp.int32, (LANES, 2 * LANES), 0)
    c_i = lax.broadcasted_iota(jnp.int32, (LANES, 2 * LANES), 1)
    tri_ones = jnp.where((c_i >= LANES) | (r_i <= c_i), 1.0, 0.0).astype(BF16)

    def key_rows(c):
        return slice(c * kc, (c + 1) * kc)

    def score_chunk(c):
        logits = lax.dot_general(qidx_ref[0].reshape(IDX_HEADS * nq, IDX_DIM),
                                 kidx_ref[0, key_rows(c), :], nt,
                                 preferred_element_type=F32)
        sc = jnp.zeros((nq, kc), F32)
        for h in range(IDX_HEADS):
            sc = sc + w_idx[:, h:h + 1] * jnp.maximum(logits[h * nq:(h + 1) * nq], 0.0)
        col = c * kc + lax.broadcasted_iota(jnp.int32, (nq, kc), 1)
        score_ref[c] = jnp.where(col <= row_pos, sc, -jnp.inf)

    def pass1(c):
        ckv = ckv_ref[0, key_rows(c), :]
        bias = bias_ref[c]
        ss = [lax.dot_general(
            qlat_ref[0, g * HEAD_GROUP:(g + 1) * HEAD_GROUP].reshape(rows, A_KV_RANK), ckv, nt,
            preferred_element_type=F32) for g in range(ng)]
        for g in range(ng):
            s = (ss[g].reshape(HEAD_GROUP, nq, kc) + bias[None]).reshape(rows, kc)
            s_ref[c, g] = s
            mt = s[:, :LANES]
            for j in range(1, tiles):
                mt = jnp.maximum(mt, s[:, j * LANES:(j + 1) * LANES])
            m_ref[g] = mt if c == 0 else jnp.maximum(m_ref[g], mt)

    def pass2(c):
        ckv = ckv_ref[0, key_rows(c), :]
        for g in range(ng):
            s = s_ref[c, g]
            m = m_ref[g]
            p = [jnp.exp2(s[:, j * LANES:(j + 1) * LANES] - m) for j in range(tiles)]
            lsum = functools.reduce(lambda a, b: a + b, p)
            pv = jnp.dot(jnp.concatenate(p, axis=1).astype(BF16), ckv,
                         preferred_element_type=F32)
            l_ref[g] = lsum if c == 0 else l_ref[g] + lsum
            acc_ref[g] = pv if c == 0 else acc_ref[g] + pv

    def block(nk, search):
        for c in range(nk):
            score_chunk(c)
        if search:
            _topk_bias(score_ref, bias_ref, i * nq, nk, topk, tri_ones)
        else:
            bias_ref[0] = jnp.where(score_ref[0] > -jnp.inf, 0.0, NEG_BIG)
        for c in range(nk):
            pass1(c)
        for g in range(ng):
            m_ref[g] = jnp.broadcast_to(jnp.max(m_ref[g], axis=1, keepdims=True), (rows, LANES))
        for c in range(nk):
            pass2(c)
        for g in range(ng):
            l = jnp.sum(l_ref[g], axis=1, keepdims=True)
            og = (acc_ref[g] * (1.0 / l)).astype(BF16)
            for hh in range(HEAD_GROUP):
                h = g * HEAD_GROUP + hh
                o_ref[0, :, h * A_V_DIM:(h + 1) * A_V_DIM] = jnp.dot(
                    og[hh * nq:(hh + 1) * nq], wuv_ref[h],
                    preferred_element_type=F32).astype(o_ref.dtype)

    keep_all = (i + 1) * nq <= topk
    pl.when(keep_all)(functools.partial(block, 1, False))
    for v in range(max_chunks):
        pl.when(jnp.logical_and(n_kc == v + 1, jnp.logical_not(keep_all)))(
            functools.partial(block, v + 1, True))


def _dsa(qlat, qidx, small, ckv, kidx, w_uv):
    B, L, _ = ckv.shape
    topk = min(TOPK_MAX, L // 4)
    kc = min(KEY_CHUNK, L)
    assert L % kc == 0 and kc % Q_BLOCK == 0 and topk <= kc
    n_chunks = L // kc
    nb = L // Q_BLOCK
    ng = A_HEADS // HEAD_GROUP
    rows = HEAD_GROUP * Q_BLOCK
    a_width = A_HEADS * A_V_DIM
    return pl.pallas_call(
        functools.partial(_dsa_kernel, topk=topk),
        grid=(B, nb),
        in_specs=[
            pl.BlockSpec((1, A_HEADS, Q_BLOCK, A_KV_RANK), lambda b, i: (b * nb + i, 0, 0, 0)),
            pl.BlockSpec((1, IDX_HEADS, Q_BLOCK, IDX_DIM), lambda b, i: (b * nb + i, 0, 0, 0)),
            pl.BlockSpec((1, Q_BLOCK, SMALL_W), lambda b, i: (b, i, 0)),
            pl.BlockSpec((1, L, A_KV_RANK), lambda b, i: (b, 0, 0)),
            pl.BlockSpec((1, L, IDX_DIM), lambda b, i: (b, 0, 0)),
            _resident(w_uv.shape),
        ],
        out_specs=pl.BlockSpec((1, Q_BLOCK, a_width), lambda b, i: (b, i, 0)),
        out_shape=jax.ShapeDtypeStruct((B, L, a_width), BF16),
        scratch_shapes=[
            pltpu.VMEM((n_chunks, Q_BLOCK, kc), F32),
            pltpu.VMEM((n_chunks, Q_BLOCK, kc), F32),
            pltpu.VMEM((n_chunks, ng, rows, kc), F32),
            pltpu.VMEM((ng, rows, LANES), F32),
            pltpu.VMEM((ng, rows, LANES), F32),
            pltpu.VMEM((ng, rows, A_KV_RANK), F32),
        ],
        compiler_params=pltpu.CompilerParams(
            dimension_semantics=("arbitrary", "arbitrary"), vmem_limit_bytes=VMEM_LIMIT),
        name="dsa",
    )(qlat, qidx, small, ckv, kidx, w_uv)


def _lane_bcast(x, c):
    lane = lax.broadcasted_iota(jnp.int32, x.shape, 1)
    col = jnp.sum(jnp.where(lane == c, x, 0.0), axis=1, keepdims=True)
    return jnp.broadcast_to(col, x.shape)


def _gdn_kernel(q_ref, k_ref, v_ref, z_ref, gates_ref, og_ref, o_ref,
                mneg_ref, r_ref, qeff_ref, o0_ref, at_ref, *, seq, hp):
    hg = pl.program_id(1)
    C = CHUNK
    n_chunks = seq // C
    dk = B_K_DIM

    R = PACK * C
    rr = lax.broadcasted_iota(jnp.int32, (R, R), 0)
    cc = lax.broadcasted_iota(jnp.int32, (R, R), 1)
    same = (rr // C) == (cc // C)
    tri = same & (rr >= cc)
    strict = same & (rr > cc)
    eye = jnp.where(rr == cc, 1.0, 0.0)
    kcol = lax.broadcasted_iota(jnp.int32, (dk, R), 1) // C
    nt = (((1,), (1,)), ((), ()))
    dot = functools.partial(jnp.dot, preferred_element_type=F32)

    def prep(t, _):
        ids = [(hh, t * PREP_GROUP + j) for j in range(PREP_GROUP) for hh in range(hp)]
        each = lambda f, *cols: [f(*a) for a in zip(*cols)]
        rows = [pl.ds(pl.multiple_of(n * R, R), R) for _, n in ids]
        load = lambda ref: [ref[0, r, hh * dk:(hh + 1) * dk] for (hh, _), r in zip(ids, rows)]
        qb, kb, vb = (load(r) for r in (q_ref, k_ref, v_ref))
        q, k, v = (each(lambda a: a.astype(F32), x) for x in (qb, kb, vb))
        gt = [gates_ref[0, r, :] for r in rows]
        beta = [_lane_bcast(a, OFF_BETA + hg * hp + hh) for a, (hh, _) in zip(gt, ids)]
        G = [_lane_bcast(a, OFF_A + hg * hp + hh) for a, (hh, _) in zip(gt, ids)]
        Gl = each(lambda g: jnp.concatenate(
            [jnp.broadcast_to(g[(p + 1) * C - 1:(p + 1) * C, :], (C, LANES)) for p in range(PACK)],
            axis=0), G)
        decay = each(lambda g: jnp.exp(jnp.where(tri, g[:, :R] - g.T[:R, :R], -jnp.inf)), G)
        eG = each(jnp.exp, G)
        qkk = each(lambda q_, k_: lax.dot_general(
            jnp.concatenate([q_, k_], axis=0), k_, nt, preferred_element_type=F32), qb, kb)
        N = each(lambda b, a, d: jnp.where(strict, b[:, :R] * a[R:] * d, 0.0), beta, qkk, decay)
        X = each(lambda a: eye - a, N)
        Nb = each(lambda a: a.astype(BF16), N)
        Pb = each(lambda a: dot(a, a).astype(BF16), Nb)
        steps = C.bit_length() - 2
        for it in range(steps):
            if it + 1 < steps:
                xp = each(lambda x, p: dot(jnp.concatenate([x.astype(BF16), p], axis=0), p), X, Pb)
                X = each(lambda x, a: x + a[:R], X, xp)
                Pb = each(lambda a: a[R:].astype(BF16), xp)
            else:
                X = each(lambda x, p: x + dot(x.astype(BF16), p), X, Pb)
        rhs = each(lambda v_, k_, b, e: jnp.concatenate(
            [v_ * b, k_ * (b * e)], axis=-1).astype(BF16), v, k, beta, eG)
        sol = each(lambda x, r: dot(x.astype(BF16), r).astype(BF16), X, rhs)
        qk = each(lambda a, d: (a[:R] * d).astype(BF16), qkk, decay)
        ktT = each(lambda k_, g, gl: (k_ * jnp.exp(gl - g)).T.astype(BF16), k, G, Gl)
        ktbd = each(lambda a: jnp.concatenate(
            [jnp.where(kcol == p, a, jnp.zeros_like(a)) for p in range(PACK)], axis=0), ktT)
        kts = each(dot, ktbd, sol)
        qks = each(dot, qk, sol)
        for i, (hh, n) in enumerate(ids):
            qe = q[i] * eG[i]
            for p in range(PACK):
                ch = n * PACK + p
                r_ref[hh, ch] = kts[i][p * dk:(p + 1) * dk, :B_V_DIM]
                mneg_ref[hh, ch] = kts[i][p * dk:(p + 1) * dk, B_V_DIM:].astype(BF16)
                o0_ref[hh, ch] = qks[i][p * C:(p + 1) * C, :B_V_DIM]
                qeff_ref[hh, ch] = (qe[p * C:(p + 1) * C]
                                    - qks[i][p * C:(p + 1) * C, B_V_DIM:]).astype(BF16)
                at_ref[hh, ch] = jnp.exp(G[i][(p + 1) * C - 1:(p + 1) * C, :])
        return 0

    lax.fori_loop(0, n_chunks // (PREP_GROUP * PACK), prep, 0)

    og = og_ref[...]

    def emit(n, o):
        rows = pl.ds(n * C if isinstance(n, int) else pl.multiple_of(n * C, C), C)
        for hh in range(hp):
            ls = slice(hh * dk, (hh + 1) * dk)
            o_ref[0, rows, ls] = (_rms(o[hh], og) * _silu(z_ref[0, rows, ls])).astype(o_ref.dtype)

    def scan(n, carry):
        S, o_prev = carry
        emit((n + n_chunks - 1) % n_chunks, o_prev)
        Sb = [s.astype(BF16) for s in S]
        ms = [dot(mneg_ref[hh, n], Sb[hh]) for hh in range(hp)]
        os_ = [dot(qeff_ref[hh, n], Sb[hh]) for hh in range(hp)]
        new = tuple(S[hh] * at_ref[hh, n] - ms[hh] + r_ref[hh, n] for hh in range(hp))
        return new, tuple(os_[hh] + o0_ref[hh, n] for hh in range(hp))

    zeros = lambda rows_: tuple(jnp.zeros((rows_, B_V_DIM), F32) for _ in range(hp))
    _, o_last = lax.fori_loop(0, n_chunks, scan, (zeros(dk), zeros(C)))
    emit(n_chunks - 1, o_last)


def _gdn(qn, kn, vn, zb, gates, onorm_g):
    B, L, W = qn.shape
    hp = GDN_HEADS_PER_STEP
    n_chunks = L // CHUNK
    heads = W // B_K_DIM
    assert heads % hp == 0 and n_chunks % (PREP_GROUP * PACK) == 0 and PACK * CHUNK <= LANES
    col = pl.BlockSpec((1, L, hp * B_K_DIM), lambda b, h: (b, 0, h))
    return pl.pallas_call(
        functools.partial(_gdn_kernel, seq=L, hp=hp),
        grid=(B, heads // hp),
        in_specs=[col, col, col, col,
                  pl.BlockSpec((1, L, SMALL_W), lambda b, h: (b, 0, 0)),
                  _resident((1, B_V_DIM))],
        out_specs=col,
        out_shape=jax.ShapeDtypeStruct((B, L, W), BF16),
        scratch_shapes=[
            pltpu.VMEM((hp, n_chunks, B_K_DIM, B_K_DIM), BF16),
            pltpu.VMEM((hp, n_chunks, B_K_DIM, B_V_DIM), F32),
            pltpu.VMEM((hp, n_chunks, CHUNK, B_K_DIM), BF16),
            pltpu.VMEM((hp, n_chunks, CHUNK, B_V_DIM), F32),
            pltpu.VMEM((hp, n_chunks, 1, LANES), F32),
        ],
        compiler_params=pltpu.CompilerParams(
            dimension_semantics=("arbitrary", "arbitrary"), vmem_limit_bytes=VMEM_LIMIT),
        name="gdn",
    )(qn, kn, vn, zb, gates, onorm_g)


def _merge_kernel(x_ref, oa_ref, ob_ref, ga_ref, gb_ref, wa_ref, wb_ref, wo_ref,
                  fg_ref, wg_ref, wu_ref, wd_ref, og_ref, o_ref, *, ff_chunk, final_norm):
    dot = functools.partial(jnp.dot, preferred_element_type=F32)
    n = x_ref.shape[0] // MERGE_SUB
    subs = [slice(s * n, (s + 1) * n) for s in range(MERGE_SUB)]
    ya = [dot(oa_ref[r, :], wa_ref[...]) for r in subs]
    yb = [dot(ob_ref[r, :], wb_ref[...]) for r in subs]
    merged = [(_sigmoid(ga_ref[r, :]) * a + _sigmoid(gb_ref[r, :]) * b).astype(BF16)
              for r, a, b in zip(subs, ya, yb)]
    acc = [x_ref[r, :] + dot(m, wo_ref[...]) for r, m in zip(subs, merged)]
    h = [_rms(a, fg_ref[...]).astype(BF16) for a in acc]
    d_ff = wg_ref.shape[1]
    for c0 in range(0, d_ff, ff_chunk):
        gate = [dot(v, wg_ref[:, c0:c0 + ff_chunk]) for v in h]
        up = [dot(v, wu_ref[:, c0:c0 + ff_chunk]) for v in h]
        act = [(_silu(g) * u).astype(BF16) for g, u in zip(gate, up)]
        acc = [a + dot(v, wd_ref[c0:c0 + ff_chunk, :]) for a, v in zip(acc, act)]
    for r, a in zip(subs, acc):
        o_ref[r, :] = _rms(a, og_ref[...]) if final_norm else a


def _merge(x2, oa, ob, ga, gb, wa, wb, wo, fg, wg, wu, wd, og, *, tm, final_norm):
    T, D = x2.shape
    d_ff = wg.shape[1]
    ff_chunk = next(c for c in (2 * FF_CHUNK, FF_CHUNK, LANES, d_ff) if d_ff % c == 0)
    row = pl.BlockSpec((tm, D), lambda i: (i, 0))
    return pl.pallas_call(
        functools.partial(_merge_kernel, ff_chunk=ff_chunk, final_norm=final_norm),
        grid=(T // tm,),
        in_specs=[row, row, row, row, row,
                  _resident(wa.shape), _resident(wb.shape), _resident(wo.shape),
                  _resident(fg.shape), _resident(wg.shape), _resident(wu.shape),
                  _resident(wd.shape), _resident(og.shape)],
        out_specs=row,
        out_shape=jax.ShapeDtypeStruct((T, D), F32),
        compiler_params=pltpu.CompilerParams(
            dimension_semantics=("arbitrary",), vmem_limit_bytes=VMEM_LIMIT),
        name="merge_ffn",
    )(x2, oa, ob, ga, gb, wa, wb, wo, fg, wg, wu, wd, og)


def _reorder_w_in(w):
    sizes = (A_Q_RANK, A_KV_RANK, IDX_DIM, IDX_HEADS,
             B_HEADS * B_K_DIM, B_HEADS * B_K_DIM, B_HEADS * B_V_DIM, B_HEADS, B_HEADS,
             B_HEADS * B_V_DIM, w.shape[0], w.shape[0])
    w = w.astype(BF16)
    parts, c0 = [], 0
    for s in sizes:
        parts.append(w[:, c0:c0 + s])
        c0 += s
    (c_q, c_kv, k_idx, w_idx, q_b, k_b, v_b, beta_b, a_b, z_b, gate_a, gate_b) = parts
    pad = jnp.zeros((w.shape[0], SMALL_W - IDX_DIM - IDX_HEADS - 2 * B_HEADS), w.dtype)
    return jnp.concatenate(
        [q_b, k_b, v_b, z_b, gate_a, gate_b, c_q, c_kv, k_idx, w_idx, beta_b, a_b, pad], axis=1)


def kernel(x, mix_norm_g, w_in, cq_norm_g, ckv_norm_g, w_uq, w_uk, w_uv, w_iq, kidx_ln_g, kidx_ln_b, w_branch_a, conv_w, a_log, dt_bias, onorm_g, w_branch_b, w_out, ffn_norm_g, w_gate, w_up, w_down, final_norm_g):
    B, L, D = x.shape
    depth = w_in.shape[0]
    T = B * L
    tm = min(TOKEN_TILE, L)
    x2 = x.reshape(T, D)
    vec = lambda a: a.reshape(1, -1).astype(F32)
    at_a = lambda a: jnp.zeros((1, SMALL_W), F32).at[0, OFF_A:OFF_A + B_HEADS].set(a.astype(F32))
    for l in range(depth):
        (qlat, qidx, ckv, kidx, small, gates, qn, kn, vn, zb, ga, gb) = _in_proj(
            x2, vec(mix_norm_g[l]), _reorder_w_in(w_in[l]), vec(cq_norm_g[l]),
            vec(ckv_norm_g[l]), vec(kidx_ln_g[l]), vec(kidx_ln_b[l]), conv_w[l].astype(F32),
            at_a(a_log[l]), at_a(dt_bias[l]), w_uq[l].astype(BF16), w_uk[l].astype(BF16),
            w_iq[l].astype(BF16), tm=tm, seq=L)
        seq = lambda a: a.reshape(B, L, a.shape[-1])
        o_a = _dsa(qlat, qidx, seq(small), seq(ckv), seq(kidx), w_uv[l].astype(BF16))
        o_b = _gdn(seq(qn), seq(kn), seq(vn), seq(zb), seq(gates), vec(onorm_g[l]))
        x2 = _merge(x2, o_a.reshape(T, -1), o_b.reshape(T, -1), ga, gb,
                    w_branch_a[l].astype(BF16), w_branch_b[l].astype(BF16),
                    w_out[l].astype(BF16), vec(ffn_norm_g[l]), w_gate[l].astype(BF16),
                    w_up[l].astype(BF16), w_down[l].astype(BF16), vec(final_norm_g),
                    tm=min(MERGE_TILE, T), final_norm=(l == depth - 1))
    return x2.reshape(B, L, D)
```

```python
import functools

import jax
import jax.numpy as jnp
from jax import lax
from jax.experimental import pallas as pl
from jax.experimental.pallas import tpu as pltpu

F32 = jnp.float32
BF16 = jnp.bfloat16

EPS = 1e-6
A_HEADS = 16
A_QK_DIM = 64
A_V_DIM = 64
A_Q_RANK = 256
A_KV_RANK = 256
IDX_HEADS = 8
IDX_DIM = 64
TOPK_MAX = 256
Q_BLOCK = 128
B_HEADS = 8
B_K_DIM = 128
B_V_DIM = 128
CONV_WIDTH = 4
CHUNK = 64

GDN_W = B_HEADS * B_K_DIM
assert B_K_DIM == B_V_DIM
LANES = 128
SUBLANES = 8
SMALL_W = LANES
OFF_WIDX = IDX_DIM
OFF_BETA = IDX_DIM + IDX_HEADS
OFF_A = IDX_DIM + IDX_HEADS + B_HEADS
VMEM_LIMIT = 56 * 1024 * 1024
NEG_BIG = -1e30
LOG2_E = 1.4426950408889634
BISECT_STEPS = 24
BISECT_UNROLL = 8
TOKEN_TILE = 512
IN_SUB = 2
MERGE_TILE = 512
MERGE_SUB = 2
HEAD_GROUP = 4
KEY_CHUNK = 512
SEARCH_GROUPS = 4
GDN_HEADS_PER_STEP = 4
PACK = 2
PREP_GROUP = 4


def _resident(shape):
    nd = len(shape)
    return pl.BlockSpec(shape, lambda *_: (0,) * nd, pipeline_mode=pl.Buffered(1))


def _rms(x, g):
    return x * lax.rsqrt(jnp.mean(x * x, axis=-1, keepdims=True) + EPS) * g


def _l2n(x, scale=1.0):
    return x * (lax.rsqrt(jnp.sum(x * x, axis=-1, keepdims=True) + EPS) * scale)


def _sigmoid(x):
    return 0.5 * jnp.tanh(0.5 * x) + 0.5


def _silu(x):
    u = 0.5 * x
    return u + u * jnp.tanh(u)


def _chunk_cumsum(g, chunk):
    pos = lax.broadcasted_iota(jnp.int32, g.shape, 0) % chunk
    d = 1
    while d < chunk:
        g = g + jnp.where(pos >= d, pltpu.roll(g, d, 0), 0.0)
        d *= 2
    return g


def _conv_silu_tile(x, tail_ref, w_half):
    full = jnp.concatenate([tail_ref[...], x], axis=0)
    u = x * w_half[CONV_WIDTH - 1:CONV_WIDTH, :]
    for d in range(1, CONV_WIDTH):
        u = u + pltpu.roll(full, d, 0)[SUBLANES:] * w_half[CONV_WIDTH - 1 - d:CONV_WIDTH - d, :]
    tail_ref[...] = x[x.shape[0] - SUBLANES:]
    return u + u * jnp.tanh(u)


def _in_proj_kernel(x_ref, g_ref, w_ref, cqg_ref, ckvg_ref, lng_ref, lnb_ref, cw_ref,
                    alog_ref, dtb_ref, wuq_ref, wuk_ref, wiq_ref,
                    qlat_ref, qidx_ref, ckv_ref, kidx_ref, small_ref, gates_ref, q_ref, k_ref, v_ref,
                    z_ref, ga_ref, gb_ref, tail_ref, *, d_model, tiles_per_seq):
    @pl.when(pl.program_id(0) % tiles_per_seq == 0)
    def _():
        tail_ref[...] = jnp.zeros_like(tail_ref)

    n = x_ref.shape[0] // IN_SUB
    subs = [slice(s * n, (s + 1) * n) for s in range(IN_SUB)]
    hs = [_rms(x_ref[r, :], g_ref[...]).astype(BF16) for r in subs]

    def proj(s, c0, width):
        return jnp.dot(hs[s], w_ref[:, c0:c0 + width], preferred_element_type=F32)

    pair = 2 * B_K_DIM
    plain, c0 = [], 3 * GDN_W
    for ref, width in ((z_ref, GDN_W), (ga_ref, d_model), (gb_ref, d_model)):
        plain += [(ref, c, c0 + c) for c in range(0, width, pair)]
        c0 += width
    steps = [(part, c) for part in range(3) for c in range(0, GDN_W, pair)]
    for idx, (part, c) in enumerate(steps):
        ref = (q_ref, k_ref, v_ref)[part]
        cc = part * GDN_W + c
        lo, hi = idx * len(plain) // len(steps), (idx + 1) * len(plain) // len(steps)
        for s, r in enumerate(subs):
            y = _conv_silu_tile(proj(s, cc, pair), tail_ref.at[:, cc:cc + pair],
                                cw_ref[:, cc:cc + pair] * 0.5)
            if part < 2:
                scale = B_K_DIM ** -0.5 if part == 0 else 1.0
                y = jnp.concatenate(
                    [_l2n(y[:, :B_K_DIM], scale), _l2n(y[:, B_K_DIM:], scale)], axis=1)
            ref[r, c:c + pair] = y.astype(ref.dtype)
            for pref, pc, wc in plain[lo:hi]:
                pref[r, pc:pc + pair] = proj(s, wc, pair)
    blocks = n // Q_BLOCK
    for s, r in enumerate(subs):
        c1 = c0
        cq = _rms(proj(s, c1, A_Q_RANK), cqg_ref[...]).astype(BF16)
        c1 += A_Q_RANK
        qa = jnp.dot(cq, wuq_ref[...], preferred_element_type=F32).astype(BF16)
        qi = jnp.dot(cq, wiq_ref[...], preferred_element_type=F32).astype(BF16)
        for hd in range(A_HEADS):
            ql = (jnp.dot(qa[:, hd * A_QK_DIM:(hd + 1) * A_QK_DIM], wuk_ref[hd],
                          preferred_element_type=F32) * (A_QK_DIM ** -0.5 * LOG2_E)).astype(BF16)
            for j in range(blocks):
                qlat_ref[s * blocks + j, hd] = ql[j * Q_BLOCK:(j + 1) * Q_BLOCK]
        for hd in range(IDX_HEADS):
            for j in range(blocks):
                qidx_ref[s * blocks + j, hd] = qi[j * Q_BLOCK:(j + 1) * Q_BLOCK,
                                                  hd * IDX_DIM:(hd + 1) * IDX_DIM]
        ckv_ref[r, :] = _rms(proj(s, c1, A_KV_RANK), ckvg_ref[...]).astype(BF16)
        c1 += A_KV_RANK
        small = proj(s, c1, SMALL_W)
        small_ref[r, :] = small
        kraw = small[:, :IDX_DIM]
        mu = jnp.mean(kraw, axis=-1, keepdims=True)
        kc = kraw - mu
        kn = kc * lax.rsqrt(jnp.mean(kc * kc, axis=-1, keepdims=True) + EPS)
        kidx_ref[r, :] = (kn * lng_ref[...] + lnb_ref[...]).astype(BF16)
        g = -jnp.exp(alog_ref[...]) * jax.nn.softplus(small + dtb_ref[...])
        lane = lax.broadcasted_iota(jnp.int32, small.shape, 1)
        gates_ref[r, :] = jnp.where(lane >= OFF_A, _chunk_cumsum(g, CHUNK), _sigmoid(small))


def _in_proj(x2, g, w_all, cq_g, ckv_g, ln_g, ln_b, conv_w, alog_v, dtb_v, w_uq, w_uk, w_iq,
             *, tm, seq):
    T, D = x2.shape
    n_all = w_all.shape[1]
    assert seq % tm == 0 and tm % (IN_SUB * Q_BLOCK) == 0 and Q_BLOCK % CHUNK == 0
    row = lambda w: pl.BlockSpec((tm, w), lambda i: (i, 0))
    qblk = lambda heads, w: pl.BlockSpec((tm // Q_BLOCK, heads, Q_BLOCK, w),
                                         lambda i: (i, 0, 0, 0))
    wide = ((GDN_W, BF16), (GDN_W, BF16), (GDN_W, BF16), (GDN_W, F32),
            (D, F32), (D, F32))
    out_shapes = (
        jax.ShapeDtypeStruct((T // Q_BLOCK, A_HEADS, Q_BLOCK, A_KV_RANK), BF16),
        jax.ShapeDtypeStruct((T // Q_BLOCK, IDX_HEADS, Q_BLOCK, IDX_DIM), BF16),
        jax.ShapeDtypeStruct((T, A_KV_RANK), BF16),
        jax.ShapeDtypeStruct((T, IDX_DIM), BF16),
        jax.ShapeDtypeStruct((T, SMALL_W), F32),
        jax.ShapeDtypeStruct((T, SMALL_W), F32),
    ) + tuple(jax.ShapeDtypeStruct((T, w), dt) for w, dt in wide)
    out_specs = (qblk(A_HEADS, A_KV_RANK), qblk(IDX_HEADS, IDX_DIM), row(A_KV_RANK),
                 row(IDX_DIM), row(SMALL_W), row(SMALL_W)) + tuple(row(w) for w, _ in wide)
    return pl.pallas_call(
        functools.partial(_in_proj_kernel, d_model=D, tiles_per_seq=seq // tm),
        grid=(T // tm,),
        in_specs=[row(D), _resident((1, D)), _resident((D, n_all)),
                  _resident((1, A_Q_RANK)), _resident((1, A_KV_RANK)),
                  _resident((1, IDX_DIM)), _resident((1, IDX_DIM)),
                  _resident(conv_w.shape), _resident((1, SMALL_W)), _resident((1, SMALL_W)),
                  _resident(w_uq.shape), _resident(w_uk.shape), _resident(w_iq.shape)],
        out_specs=out_specs,
        out_shape=out_shapes,
        scratch_shapes=[pltpu.VMEM((SUBLANES, conv_w.shape[1]), F32)],
        compiler_params=pltpu.CompilerParams(
            dimension_semantics=("arbitrary",), vmem_limit_bytes=VMEM_LIMIT),
        name="in_proj",
    )(x2, g, w_all, cq_g, ckv_g, ln_g, ln_b, conv_w, alog_v, dtb_v, w_uq, w_uk, w_iq)


def _topk_bias(score_ref, bias_ref, row0, n_chunks, topk, tri_ones):
    _, nq, kc = score_ref.shape
    rg = nq // SEARCH_GROUPS
    groups = range(SEARCH_GROUPS)
    rsum = lambda x: jnp.sum(x, axis=1, keepdims=True)
    rmin = lambda x: jnp.min(x, axis=1, keepdims=True)

    def ld(g):
        return jnp.concatenate(
            [score_ref[c, g * rg:(g + 1) * rg, :] for c in range(n_chunks)], axis=1)

    pos = lambda g: row0 + g * rg + lax.broadcasted_iota(jnp.int32, (rg, 1), 0)
    k_eff = [jnp.minimum(pos(g) + 1, topk).astype(F32) for g in groups]
    lo = tuple(rmin(jnp.where(ld(g) == -jnp.inf, jnp.inf, ld(g))) for g in groups)
    hi = tuple(jnp.max(ld(g), axis=1, keepdims=True) for g in groups)

    def bisect(_, carry):
        lo, hi, c_lo = carry
        mid = [0.5 * (a + b) for a, b in zip(lo, hi)]
        cnt = [rsum(jnp.where(ld(g) >= mid[g], 1.0, 0.0)) for g in groups]
        ge = [cnt[g] >= k_eff[g] for g in groups]
        return (tuple(jnp.where(ge[g], mid[g], lo[g]) for g in groups),
                tuple(jnp.where(ge[g], hi[g], mid[g]) for g in groups),
                tuple(jnp.where(ge[g], cnt[g], c_lo[g]) for g in groups))

    n_valid = tuple((pos(g) + 1).astype(F32) for g in groups)
    lo, hi, c_lo = lax.fori_loop(0, BISECT_STEPS, bisect, (lo, hi, n_valid),
                                 unroll=BISECT_UNROLL)

    flags = [jnp.max(jnp.where(c_lo[g] != k_eff[g], 1.0, 0.0)) for g in groups]
    exact_cut = functools.reduce(jnp.maximum, flags) == 0.0

    @pl.when(exact_cut)
    def _():
        lo_all = jnp.concatenate(lo, axis=0)
        for c in range(n_chunks):
            bias_ref[c] = jnp.where(score_ref[c] >= lo_all, 0.0, NEG_BIG)

    pl.when(jnp.logical_not(exact_cut))(functools.partial(
        _topk_walk, score_ref, bias_ref, n_chunks, k_eff, lo, tri_ones))


def _topk_walk(score_ref, bias_ref, n_chunks, k_eff, lo, tri_ones):
    _, nq, kc = score_ref.shape
    rg = nq // SEARCH_GROUPS
    groups = range(SEARCH_GROUPS)
    rsum = lambda x: jnp.sum(x, axis=1, keepdims=True)
    rmin = lambda x: jnp.min(x, axis=1, keepdims=True)

    def ld(g):
        return jnp.concatenate(
            [score_ref[c, g * rg:(g + 1) * rg, :] for c in range(n_chunks)], axis=1)

    def above(g, v):
        s = ld(g)
        gt = s > v
        return rsum(jnp.where(gt, 1.0, 0.0)), rmin(jnp.where(gt, s, jnp.inf))

    v0 = tuple(rmin(jnp.where(ld(g) >= lo[g], ld(g), jnp.inf)) for g in groups)
    first = [above(g, v0[g]) for g in groups]

    def cond(c):
        _, n_gt, _ = c
        flags = [jnp.max(jnp.where(n_gt[g] >= k_eff[g], 1.0, 0.0)) for g in groups]
        return functools.reduce(jnp.maximum, flags) > 0.0

    def body(c):
        v, n_gt, nxt = c
        v = tuple(jnp.where(n_gt[g] >= k_eff[g], nxt[g], v[g]) for g in groups)
        nxt_state = [above(g, v[g]) for g in groups]
        return v, tuple(a for a, _ in nxt_state), tuple(b for _, b in nxt_state)

    tau, n_gt, _ = lax.while_loop(
        cond, body, (v0, tuple(a for a, _ in first), tuple(b for _, b in first)))
    k_all = jnp.concatenate(k_eff, axis=0)
    need = k_all - jnp.concatenate(n_gt, axis=0)
    tau = jnp.concatenate(tau, axis=0)

    n_ge = jnp.zeros((nq, LANES), F32)
    for c in range(n_chunks):
        for j in range(kc // LANES):
            ge = score_ref[c, :, j * LANES:(j + 1) * LANES] >= tau
            bias_ref[c, :, j * LANES:(j + 1) * LANES] = jnp.where(ge, 0.0, NEG_BIG)
            n_ge = n_ge + jnp.where(ge, 1.0, 0.0)
    surplus = jnp.max(jnp.where(rsum(n_ge) > k_all, 1.0, 0.0)) > 0.0

    @pl.when(surplus)
    def _():
        carry = jnp.zeros((nq, LANES), F32)
        for c in range(n_chunks):
            for j in range(kc // LANES):
                sc = score_ref[c, :, j * LANES:(j + 1) * LANES]
                eq = sc == tau
                r = jnp.dot(jnp.where(eq, 1.0, 0.0).astype(BF16), tri_ones,
                            preferred_element_type=F32)
                prefix = r[:, :LANES] + carry
                carry = carry + r[:, LANES:]
                sel = (sc > tau) | (eq & (prefix <= need))
                bias_ref[c, :, j * LANES:(j + 1) * LANES] = jnp.where(sel, 0.0, NEG_BIG)


def _dsa_kernel(qlat_ref, qidx_ref, small_ref, ckv_ref, kidx_ref, wuv_ref,
                o_ref, score_ref, bias_ref, s_ref, m_ref, l_ref, acc_ref, *, topk):
    i = pl.program_id(1)
    nq = Q_BLOCK
    max_chunks, _, kc = score_ref.shape
    n_kc = (i * nq) // kc + 1
    ng = A_HEADS // HEAD_GROUP
    rows = HEAD_GROUP * nq
    tiles = kc // LANES
    nt = (((1,), (1,)), ((), ()))
    w_idx = small_ref[0][:, OFF_WIDX:OFF_WIDX + IDX_HEADS] * (IDX_HEADS ** -0.5 * IDX_DIM ** -0.5)
    row_pos = i * nq + lax.broadcasted_iota(jnp.int32, (nq, 1), 0)

    r_i = lax.broadcasted_iota(jnp.int32, (LANES, 2 * LANES), 0)
    c_i = lax.broadcasted_iota(jnp.int32, (LANES, 2 * LANES), 1)
    tri_ones = jnp.where((c_i >= LANES) | (r_i <= c_i), 1.0, 0.0).astype(BF16)

    def key_rows(c):
        return slice(c * kc, (c + 1) * kc)

    def score_chunk(c):
        logits = lax.dot_general(qidx_ref[0].reshape(IDX_HEADS * nq, IDX_DIM),
                                 kidx_ref[0, key_rows(c), :], nt,
                                 preferred_element_type=F32)
        sc = jnp.zeros((nq, kc), F32)
        for h in range(IDX_HEADS):
            sc = sc + w_idx[:, h:h + 1] * jnp.maximum(logits[h * nq:(h + 1) * nq], 0.0)
        col = c * kc + lax.broadcasted_iota(jnp.int32, (nq, kc), 1)
        score_ref[c] = jnp.where(col <= row_pos, sc, -jnp.inf)

    def pass1(c):
        ckv = ckv_ref[0, key_rows(c), :]
        bias = bias_ref[c]
        ss = [lax.dot_general(
            qlat_ref[0, g * HEAD_GROUP:(g + 1) * HEAD_GROUP].reshape(rows, A_KV_RANK), ckv, nt,
            preferred_element_type=F32) for g in range(ng)]
        for g in range(ng):
            s = (ss[g].reshape(HEAD_GROUP, nq, kc) + bias[None]).reshape(rows, kc)
            s_ref[c, g] = s
            mt = s[:, :LANES]
            for j in range(1, tiles):
                mt = jnp.maximum(mt, s[:, j * LANES:(j + 1) * LANES])
            m_ref[g] = mt if c == 0 else jnp.maximum(m_ref[g], mt)

    def pass2(c):
        ckv = ckv_ref[0, key_rows(c), :]
        for g in range(ng):
            s = s_ref[c, g]
            m = m_ref[g]
            p = [jnp.exp2(s[:, j * LANES:(j + 1) * LANES] - m) for j in range(tiles)]
            lsum = functools.reduce(lambda a, b: a + b, p)
            pv = jnp.dot(jnp.concatenate(p, axis=1).astype(BF16), ckv,
                         preferred_element_type=F32)
            l_ref[g] = lsum if c == 0 else l_ref[g] + lsum
            acc_ref[g] = pv if c == 0 else acc_ref[g] + pv

    def block(nk, search):
        for c in range(nk):
            score_chunk(c)
        if search:
            _topk_bias(score_ref, bias_ref, i * nq, nk, topk, tri_ones)
        else:
            bias_ref[0] = jnp.where(score_ref[0] > -jnp.inf, 0.0, NEG_BIG)
        for c in range(nk):
            pass1(c)
        for g in range(ng):
            m_ref[g] = jnp.broadcast_to(jnp.max(m_ref[g], axis=1, keepdims=True), (rows, LANES))
        for c in range(nk):
            pass2(c)
        for g in range(ng):
            l = jnp.sum(l_ref[g], axis=1, keepdims=True)
            og = (acc_ref[g] * (1.0 / l)).astype(BF16)
            for hh in range(HEAD_GROUP):
                h = g * HEAD_GROUP + hh
                o_ref[0, :, h * A_V_DIM:(h + 1) * A_V_DIM] = jnp.dot(
                    og[hh * nq:(hh + 1) * nq], wuv_ref[h],
                    preferred_element_type=F32).astype(o_ref.dtype)

    keep_all = (i + 1) * nq <= topk
    pl.when(keep_all)(functools.partial(block, 1, False))
    for v in range(max_chunks):
        pl.when(jnp.logical_and(n_kc == v + 1, jnp.logical_not(keep_all)))(
            functools.partial(block, v + 1, True))


def _dsa(qlat, qidx, small, ckv, kidx, w_uv):
    B, L, _ = ckv.shape
    topk = min(TOPK_MAX, L // 4)
    kc = min(KEY_CHUNK, L)
    assert L % kc == 0 and kc % Q_BLOCK == 0 and topk <= kc
    n_chunks = L // kc
    nb = L // Q_BLOCK
    ng = A_HEADS // HEAD_GROUP
    rows = HEAD_GROUP * Q_BLOCK
    a_width = A_HEADS * A_V_DIM
    return pl.pallas_call(
        functools.partial(_dsa_kernel, topk=topk),
        grid=(B, nb),
        in_specs=[
            pl.BlockSpec((1, A_HEADS, Q_BLOCK, A_KV_RANK), lambda b, i: (b * nb + i, 0, 0, 0)),
            pl.BlockSpec((1, IDX_HEADS, Q_BLOCK, IDX_DIM), lambda b, i: (b * nb + i, 0, 0, 0)),
            pl.BlockSpec((1, Q_BLOCK, SMALL_W), lambda b, i: (b, i, 0)),
            pl.BlockSpec((1, L, A_KV_RANK), lambda b, i: (b, 0, 0)),
            pl.BlockSpec((1, L, IDX_DIM), lambda b, i: (b, 0, 0)),
            _resident(w_uv.shape),
        ],
        out_specs=pl.BlockSpec((1, Q_BLOCK, a_width), lambda b, i: (b, i, 0)),
        out_shape=jax.ShapeDtypeStruct((B, L, a_width), BF16),
        scratch_shapes=[
            pltpu.VMEM((n_chunks, Q_BLOCK, kc), F32),
            pltpu.VMEM((n_chunks, Q_BLOCK, kc), F32),
            pltpu.VMEM((n_chunks, ng, rows, kc), F32),
            pltpu.VMEM((ng, rows, LANES), F32),
            pltpu.VMEM((ng, rows, LANES), F32),
            pltpu.VMEM((ng, rows, A_KV_RANK), F32),
        ],
        compiler_params=pltpu.CompilerParams(
            dimension_semantics=("arbitrary", "arbitrary"), vmem_limit_bytes=VMEM_LIMIT),
        name="dsa",
    )(qlat, qidx, small, ckv, kidx, w_uv)


def _lane_bcast(x, c):
    lane = lax.broadcasted_iota(jnp.int32, x.shape, 1)
    col = jnp.sum(jnp.where(lane == c, x, 0.0), axis=1, keepdims=True)
    return jnp.broadcast_to(col, x.shape)


def _gdn_kernel(q_ref, k_ref, v_ref, z_ref, gates_ref, og_ref, o_ref,
                mneg_ref, r_ref, qeff_ref, o0_ref, at_ref, *, seq, hp):
    hg = pl.program_id(1)
    C = CHUNK
    n_chunks = seq // C
    dk = B_K_DIM

    R = PACK * C
    rr = lax.broadcasted_iota(jnp.int32, (R, R), 0)
    cc = lax.broadcasted_iota(jnp.int32, (R, R), 1)
    same = (rr // C) == (cc // C)
    tri = same & (rr >= cc)
    strict = same & (rr > cc)
    eye = jnp.where(rr == cc, 1.0, 0.0)
    kcol = lax.broadcasted_iota(jnp.int32, (dk, R), 1) // C
    nt = (((1,), (1,)), ((), ()))
    dot = functools.partial(jnp.dot, preferred_element_type=F32)

    def prep(t, _):
        ids = [(hh, t * PREP_GROUP + j) for j in range(PREP_GROUP) for hh in range(hp)]
        each = lambda f, *cols: [f(*a) for a in zip(*cols)]
        rows = [pl.ds(pl.multiple_of(n * R, R), R) for _, n in ids]
        load = lambda ref: [ref[0, r, hh * dk:(hh + 1) * dk] for (hh, _), r in zip(ids, rows)]
        qb, kb, vb = (load(r) for r in (q_ref, k_ref, v_ref))
        q, k, v = (each(lambda a: a.astype(F32), x) for x in (qb, kb, vb))
        gt = [gates_ref[0, r, :] for r in rows]
        beta = [_lane_bcast(a, OFF_BETA + hg * hp + hh) for a, (hh, _) in zip(gt, ids)]
        G = [_lane_bcast(a, OFF_A + hg * hp + hh) for a, (hh, _) in zip(gt, ids)]
        Gl = each(lambda g: jnp.concatenate(
            [jnp.broadcast_to(g[(p + 1) * C - 1:(p + 1) * C, :], (C, LANES)) for p in range(PACK)],
            axis=0), G)
        decay = each(lambda g: jnp.exp(jnp.where(tri, g[:, :R] - g.T[:R, :R], -jnp.inf)), G)
        eG = each(jnp.exp, G)
        qkk = each(lambda q_, k_: lax.dot_general(
            jnp.concatenate([q_, k_], axis=0), k_, nt, preferred_element_type=F32), qb, kb)
        N = each(lambda b, a, d: jnp.where(strict, b[:, :R] * a[R:] * d, 0.0), beta, qkk, decay)
        X = each(lambda a: eye - a, N)
        Nb = each(lambda a: a.astype(BF16), N)
        Pb = each(lambda a: dot(a, a).astype(BF16), Nb)
        steps = C.bit_length() - 2
        for it in range(steps):
            if it + 1 < steps:
                xp = each(lambda x, p: dot(jnp.concatenate([x.astype(BF16), p], axis=0), p), X, Pb)
                X = each(lambda x, a: x + a[:R], X, xp)
                Pb = each(lambda a: a[R:].astype(BF16), xp)
            else:
                X = each(lambda x, p: x + dot(x.astype(BF16), p), X, Pb)
        rhs = each(lambda v_, k_, b, e: jnp.concatenate(
            [v_ * b, k_ * (b * e)], axis=-1).astype(BF16), v, k, beta, eG)
        sol = each(lambda x, r: dot(x.astype(BF16), r).astype(BF16), X, rhs)
        qk = each(lambda a, d: (a[:R] * d).astype(BF16), qkk, decay)
        ktT = each(lambda k_, g, gl: (k_ * jnp.exp(gl - g)).T.astype(BF16), k, G, Gl)
        ktbd = each(lambda a: jnp.concatenate(
            [jnp.where(kcol == p, a, jnp.zeros_like(a)) for p in range(PACK)], axis=0), ktT)
        kts = each(dot, ktbd, sol)
        qks = each(dot, qk, sol)
        for i, (hh, n) in enumerate(ids):
            qe = q[i] * eG[i]
            for p in range(PACK):
                ch = n * PACK + p
                r_ref[hh, ch] = kts[i][p * dk:(p + 1) * dk, :B_V_DIM]
                mneg_ref[hh, ch] = kts[i][p * dk:(p + 1) * dk, B_V_DIM:].astype(BF16)
                o0_ref[hh, ch] = qks[i][p * C:(p + 1) * C, :B_V_DIM]
                qeff_ref[hh, ch] = (qe[p * C:(p + 1) * C]
                                    - qks[i][p * C:(p + 1) * C, B_V_DIM:]).astype(BF16)
                at_ref[hh, ch] = jnp.exp(G[i][(p + 1) * C - 1:(p + 1) * C, :])
        return 0

    lax.fori_loop(0, n_chunks // (PREP_GROUP * PACK), prep, 0)

    og = og_ref[...]

    def emit(n, o):
        rows = pl.ds(n * C if isinstance(n, int) else pl.multiple_of(n * C, C), C)
        for hh in range(hp):
            ls = slice(hh * dk, (hh + 1) * dk)
            o_ref[0, rows, ls] = (_rms(o[hh], og) * _silu(z_ref[0, rows, ls])).astype(o_ref.dtype)

    def scan(n, carry):
        S, o_prev = carry
        emit((n + n_chunks - 1) % n_chunks, o_prev)
        Sb = [s.astype(BF16) for s in S]
        ms = [dot(mneg_ref[hh, n], Sb[hh]) for hh in range(hp)]
        os_ = [dot(qeff_ref[hh, n], Sb[hh]) for hh in range(hp)]
        new = tuple(S[hh] * at_ref[hh, n] - ms[hh] + r_ref[hh, n] for hh in range(hp))
        return new, tuple(os_[hh] + o0_ref[hh, n] for hh in range(hp))

    zeros = lambda rows_: tuple(jnp.zeros((rows_, B_V_DIM), F32) for _ in range(hp))
    _, o_last = lax.fori_loop(0, n_chunks, scan, (zeros(dk), zeros(C)))
    emit(n_chunks - 1, o_last)


def _gdn(qn, kn, vn, zb, gates, onorm_g):
    B, L, W = qn.shape
    hp = GDN_HEADS_PER_STEP
    n_chunks = L // CHUNK
    heads = W // B_K_DIM
    assert heads % hp == 0 and n_chunks % (PREP_GROUP * PACK) == 0 and PACK * CHUNK <= LANES
    col = pl.BlockSpec((1, L, hp * B_K_DIM), lambda b, h: (b, 0, h))
    return pl.pallas_call(
        functools.partial(_gdn_kernel, seq=L, hp=hp),
        grid=(B, heads // hp),
        in_specs=[col, col, col, col,
                  pl.BlockSpec((1, L, SMALL_W), lambda b, h: (b, 0, 0)),
                  _resident((1, B_V_DIM))],
        out_specs=col,
        out_shape=jax.ShapeDtypeStruct((B, L, W), BF16),
        scratch_shapes=[
            pltpu.VMEM((hp, n_chunks, B_K_DIM, B_K_DIM), BF16),
            pltpu.VMEM((hp, n_chunks, B_K_DIM, B_V_DIM), F32),
            pltpu.VMEM((hp, n_chunks, CHUNK, B_K_DIM), BF16),
            pltpu.VMEM((hp, n_chunks, CHUNK, B_V_DIM), F32),
            pltpu.VMEM((hp, n_chunks, 1, LANES), F32),
        ],
        compiler_params=pltpu.CompilerParams(
            dimension_semantics=("arbitrary", "arbitrary"), vmem_limit_bytes=VMEM_LIMIT),
        name="gdn",
    )(qn, kn, vn, zb, gates, onorm_g)


def _merge_kernel(x_ref, oa_ref, ob_ref, ga_ref, gb_ref, wa_ref, wb_ref, wo_ref,
                  fg_ref, wg_ref, wu_ref, wd_ref, og_ref, o_ref, *, ff_chunk, final_norm):
    dot = functools.partial(jnp.dot, preferred_element_type=F32)
    n = x_ref.shape[0] // MERGE_SUB
    subs = [slice(s * n, (s + 1) * n) for s in range(MERGE_SUB)]
    ya = [dot(oa_ref[r, :], wa_ref[...]) for r in subs]
    yb = [dot(ob_ref[r, :], wb_ref[...]) for r in subs]
    merged = [(_sigmoid(ga_ref[r, :]) * a + _sigmoid(gb_ref[r, :]) * b).astype(BF16)
              for r, a, b in zip(subs, ya, yb)]
    acc = [x_ref[r, :] + dot(m, wo_ref[...]) for r, m in zip(subs, merged)]
    h = [_rms(a, fg_ref[...]).astype(BF16) for a in acc]
    d_ff = wg_ref.shape[1]
    for c0 in range(0, d_ff, ff_chunk):
        gate = [dot(v, wg_ref[:, c0:c0 + ff_chunk]) for v in h]
        up = [dot(v, wu_ref[:, c0:c0 + ff_chunk]) for v in h]
        act = [(_silu(g) * u).astype(BF16) for g, u in zip(gate, up)]
        acc = [a + dot(v, wd_ref[c0:c0 + ff_chunk, :]) for a, v in zip(acc, act)]
    for r, a in zip(subs, acc):
        o_ref[r, :] = _rms(a, og_ref[...]) if final_norm else a


def _merge(x2, oa, ob, ga, gb, wa, wb, wo, fg, wg, wu, wd, og, *, tm, final_norm):
    T, D = x2.shape
    d_ff = wg.shape[1]
    ff_chunk = d_ff // 2 if (d_ff // 2) % LANES == 0 else d_ff
    row = pl.BlockSpec((tm, D), lambda i: (i, 0))
    return pl.pallas_call(
        functools.partial(_merge_kernel, ff_chunk=ff_chunk, final_norm=final_norm),
        grid=(T // tm,),
        in_specs=[row, row, row, row, row,
                  _resident(wa.shape), _resident(wb.shape), _resident(wo.shape),
                  _resident(fg.shape), _resident(wg.shape), _resident(wu.shape),
                  _resident(wd.shape), _resident(og.shape)],
        out_specs=row,
        out_shape=jax.ShapeDtypeStruct((T, D), F32),
        compiler_params=pltpu.CompilerParams(
            dimension_semantics=("arbitrary",), vmem_limit_bytes=VMEM_LIMIT),
        name="merge_ffn",
    )(x2, oa, ob, ga, gb, wa, wb, wo, fg, wg, wu, wd, og)


def _reorder_w_in(w):
    sizes = (A_Q_RANK, A_KV_RANK, IDX_DIM, IDX_HEADS,
             B_HEADS * B_K_DIM, B_HEADS * B_K_DIM, B_HEADS * B_V_DIM, B_HEADS, B_HEADS,
             B_HEADS * B_V_DIM, w.shape[0], w.shape[0])
    parts, c0 = [], 0
    for s in sizes:
        parts.append(w[:, c0:c0 + s])
        c0 += s
    (c_q, c_kv, k_idx, w_idx, q_b, k_b, v_b, beta_b, a_b, z_b, gate_a, gate_b) = parts
    pad = jnp.zeros((w.shape[0], SMALL_W - IDX_DIM - IDX_HEADS - 2 * B_HEADS), w.dtype)
    return jnp.concatenate(
        [q_b, k_b, v_b, z_b, gate_a, gate_b, c_q, c_kv, k_idx, w_idx, beta_b, a_b, pad],
        axis=1).astype(BF16)


def kernel(x, mix_norm_g, w_in, cq_norm_g, ckv_norm_g, w_uq, w_uk, w_uv, w_iq, kidx_ln_g, kidx_ln_b, w_branch_a, conv_w, a_log, dt_bias, onorm_g, w_branch_b, w_out, ffn_norm_g, w_gate, w_up, w_down, final_norm_g):
    B, L, D = x.shape
    depth = w_in.shape[0]
    T = B * L
    tm = min(TOKEN_TILE, L)
    x2 = x.reshape(T, D)
    vec = lambda a: a.reshape(1, -1).astype(F32)
    at_a = lambda a: jnp.zeros((1, SMALL_W), F32).at[0, OFF_A:OFF_A + B_HEADS].set(a.astype(F32))
    for l in range(depth):
        (qlat, qidx, ckv, kidx, small, gates, qn, kn, vn, zb, ga, gb) = _in_proj(
            x2, vec(mix_norm_g[l]), _reorder_w_in(w_in[l]), vec(cq_norm_g[l]),
            vec(ckv_norm_g[l]), vec(kidx_ln_g[l]), vec(kidx_ln_b[l]), conv_w[l].astype(F32),
            at_a(a_log[l]), at_a(dt_bias[l]), w_uq[l].astype(BF16), w_uk[l].astype(BF16),
            w_iq[l].astype(BF16), tm=tm, seq=L)
        seq = lambda a: a.reshape(B, L, a.shape[-1])
        o_a = _dsa(qlat, qidx, seq(small), seq(ckv), seq(kidx), w_uv[l].astype(BF16))
        o_b = _gdn(seq(qn), seq(kn), seq(vn), seq(zb), seq(gates), vec(onorm_g[l]))
        x2 = _merge(x2, o_a.reshape(T, -1), o_b.reshape(T, -1), ga, gb,
                    w_branch_a[l].astype(BF16), w_branch_b[l].astype(BF16),
                    w_out[l].astype(BF16), vec(ffn_norm_g[l]), w_gate[l].astype(BF16),
                    w_up[l].astype(BF16), w_down[l].astype(BF16), vec(final_norm_g),
                    tm=min(MERGE_TILE, T), final_norm=(l == depth - 1))
    return x2.reshape(B, L, D)
```

```python
import functools

import jax
import jax.numpy as jnp
from jax import lax
from jax.experimental import pallas as pl
from jax.experimental.pallas import tpu as pltpu

F32 = jnp.float32
BF16 = jnp.bfloat16

EPS = 1e-6
A_HEADS = 16
A_QK_DIM = 64
A_V_DIM = 64
A_Q_RANK = 256
A_KV_RANK = 256
IDX_HEADS = 8
IDX_DIM = 64
TOPK_MAX = 256
Q_BLOCK = 128
B_HEADS = 8
B_K_DIM = 128
B_V_DIM = 128
CONV_WIDTH = 4
CHUNK = 64

GDN_W = B_HEADS * B_K_DIM
assert B_K_DIM == B_V_DIM
LANES = 128
SUBLANES = 8
SMALL_W = LANES
OFF_WIDX = IDX_DIM
OFF_BETA = IDX_DIM + IDX_HEADS
OFF_A = IDX_DIM + IDX_HEADS + B_HEADS
VMEM_LIMIT = 56 * 1024 * 1024
NEG_BIG = -1e30
LOG2_E = 1.4426950408889634
BISECT_STEPS = 16
BISECT_UNROLL = 8
TOKEN_TILE = 512
IN_SUB = 2
MERGE_TILE = 512
MERGE_SUB = 2
HEAD_GROUP = 4
KEY_CHUNK = 256
SEARCH_GROUPS = 4
GDN_HEADS_PER_STEP = 4
PACK = 2
PREP_GROUP = 4


def _resident(shape):
    nd = len(shape)
    return pl.BlockSpec(shape, lambda *_: (0,) * nd, pipeline_mode=pl.Buffered(1))


def _rms(x, g):
    return x * lax.rsqrt(jnp.mean(x * x, axis=-1, keepdims=True) + EPS) * g


def _l2n(x, scale=1.0):
    return x * (lax.rsqrt(jnp.sum(x * x, axis=-1, keepdims=True) + EPS) * scale)


def _sigmoid(x):
    return 0.5 * jnp.tanh(0.5 * x) + 0.5


def _silu(x):
    u = 0.5 * x
    return u + u * jnp.tanh(u)


def _chunk_cumsum(g, chunk):
    pos = lax.broadcasted_iota(jnp.int32, g.shape, 0) % chunk
    d = 1
    while d < chunk:
        g = g + jnp.where(pos >= d, pltpu.roll(g, d, 0), 0.0)
        d *= 2
    return g


def _conv_silu_tile(x, tail_ref, w_half):
    full = jnp.concatenate([tail_ref[...], x], axis=0)
    u = x * w_half[CONV_WIDTH - 1:CONV_WIDTH, :]
    for d in range(1, CONV_WIDTH):
        u = u + pltpu.roll(full, d, 0)[SUBLANES:] * w_half[CONV_WIDTH - 1 - d:CONV_WIDTH - d, :]
    tail_ref[...] = x[x.shape[0] - SUBLANES:]
    return u + u * jnp.tanh(u)


def _in_proj_kernel(x_ref, g_ref, w_ref, cqg_ref, ckvg_ref, lng_ref, lnb_ref, cw_ref,
                    alog_ref, dtb_ref, wuq_ref, wuk_ref, wiq_ref,
                    qlat_ref, qidx_ref, ckv_ref, kidx_ref, small_ref, gates_ref, q_ref, k_ref, v_ref,
                    z_ref, ga_ref, gb_ref, tail_ref, *, d_model, tiles_per_seq):
    @pl.when(pl.program_id(0) % tiles_per_seq == 0)
    def _():
        tail_ref[...] = jnp.zeros_like(tail_ref)

    n = x_ref.shape[0] // IN_SUB
    subs = [slice(s * n, (s + 1) * n) for s in range(IN_SUB)]
    hs = [_rms(x_ref[r, :], g_ref[...]).astype(BF16) for r in subs]

    def proj(s, c0, width):
        return jnp.dot(hs[s], w_ref[:, c0:c0 + width], preferred_element_type=F32)

    pair = 2 * B_K_DIM
    plain, c0 = [], 3 * GDN_W
    for ref, width in ((z_ref, GDN_W), (ga_ref, d_model), (gb_ref, d_model)):
        plain += [(ref, c, c0 + c) for c in range(0, width, pair)]
        c0 += width
    steps = [(part, c) for part in range(3) for c in range(0, GDN_W, pair)]
    for idx, (part, c) in enumerate(steps):
        ref = (q_ref, k_ref, v_ref)[part]
        cc = part * GDN_W + c
        lo, hi = idx * len(plain) // len(steps), (idx + 1) * len(plain) // len(steps)
        for s, r in enumerate(subs):
            y = _conv_silu_tile(proj(s, cc, pair), tail_ref.at[:, cc:cc + pair],
                                cw_ref[:, cc:cc + pair] * 0.5)
            if part < 2:
                scale = B_K_DIM ** -0.5 if part == 0 else 1.0
                y = jnp.concatenate(
                    [_l2n(y[:, :B_K_DIM], scale), _l2n(y[:, B_K_DIM:], scale)], axis=1)
            ref[r, c:c + pair] = y.astype(ref.dtype)
            for pref, pc, wc in plain[lo:hi]:
                pref[r, pc:pc + pair] = proj(s, wc, pair)
    blocks = n // Q_BLOCK
    for s, r in enumerate(subs):
        c1 = c0
        cq = _rms(proj(s, c1, A_Q_RANK), cqg_ref[...]).astype(BF16)
        c1 += A_Q_RANK
        qa = jnp.dot(cq, wuq_ref[...], preferred_element_type=F32).astype(BF16)
        qi = jnp.dot(cq, wiq_ref[...], preferred_element_type=F32).astype(BF16)
        for hd in range(A_HEADS):
            ql = (jnp.dot(qa[:, hd * A_QK_DIM:(hd + 1) * A_QK_DIM], wuk_ref[hd],
                          preferred_element_type=F32) * (A_QK_DIM ** -0.5 * LOG2_E)).astype(BF16)
            for j in range(blocks):
                qlat_ref[s * blocks + j, hd] = ql[j * Q_BLOCK:(j + 1) * Q_BLOCK]
        for hd in range(IDX_HEADS):
            for j in range(blocks):
                qidx_ref[s * blocks + j, hd] = qi[j * Q_BLOCK:(j + 1) * Q_BLOCK,
                                                  hd * IDX_DIM:(hd + 1) * IDX_DIM]
        ckv_ref[r, :] = _rms(proj(s, c1, A_KV_RANK), ckvg_ref[...]).astype(BF16)
        c1 += A_KV_RANK
        small = proj(s, c1, SMALL_W)
        small_ref[r, :] = small
        kraw = small[:, :IDX_DIM]
        mu = jnp.mean(kraw, axis=-1, keepdims=True)
        kc = kraw - mu
        kn = kc * lax.rsqrt(jnp.mean(kc * kc, axis=-1, keepdims=True) + EPS)
        kidx_ref[r, :] = (kn * lng_ref[...] + lnb_ref[...]).astype(BF16)
        g = -jnp.exp(alog_ref[...]) * jax.nn.softplus(small + dtb_ref[...])
        lane = lax.broadcasted_iota(jnp.int32, small.shape, 1)
        gates_ref[r, :] = jnp.where(lane >= OFF_A, _chunk_cumsum(g, CHUNK), _sigmoid(small))


def _in_proj(x2, g, w_all, cq_g, ckv_g, ln_g, ln_b, conv_w, alog_v, dtb_v, w_uq, w_uk, w_iq,
             *, tm, seq):
    T, D = x2.shape
    n_all = w_all.shape[1]
    assert seq % tm == 0 and tm % (IN_SUB * Q_BLOCK) == 0 and Q_BLOCK % CHUNK == 0
    row = lambda w: pl.BlockSpec((tm, w), lambda i: (i, 0))
    qblk = lambda heads, w: pl.BlockSpec((tm // Q_BLOCK, heads, Q_BLOCK, w),
                                         lambda i: (i, 0, 0, 0))
    wide = ((GDN_W, BF16), (GDN_W, BF16), (GDN_W, BF16), (GDN_W, F32),
            (D, F32), (D, F32))
    out_shapes = (
        jax.ShapeDtypeStruct((T // Q_BLOCK, A_HEADS, Q_BLOCK, A_KV_RANK), BF16),
        jax.ShapeDtypeStruct((T // Q_BLOCK, IDX_HEADS, Q_BLOCK, IDX_DIM), BF16),
        jax.ShapeDtypeStruct((T, A_KV_RANK), BF16),
        jax.ShapeDtypeStruct((T, IDX_DIM), BF16),
        jax.ShapeDtypeStruct((T, SMALL_W), F32),
        jax.ShapeDtypeStruct((T, SMALL_W), F32),
    ) + tuple(jax.ShapeDtypeStruct((T, w), dt) for w, dt in wide)
    out_specs = (qblk(A_HEADS, A_KV_RANK), qblk(IDX_HEADS, IDX_DIM), row(A_KV_RANK),
                 row(IDX_DIM), row(SMALL_W), row(SMALL_W)) + tuple(row(w) for w, _ in wide)
    return pl.pallas_call(
        functools.partial(_in_proj_kernel, d_model=D, tiles_per_seq=seq // tm),
        grid=(T // tm,),
        in_specs=[row(D), _resident((1, D)), _resident((D, n_all)),
                  _resident((1, A_Q_RANK)), _resident((1, A_KV_RANK)),
                  _resident((1, IDX_DIM)), _resident((1, IDX_DIM)),
                  _resident(conv_w.shape), _resident((1, SMALL_W)), _resident((1, SMALL_W)),
                  _resident(w_uq.shape), _resident(w_uk.shape), _resident(w_iq.shape)],
        out_specs=out_specs,
        out_shape=out_shapes,
        scratch_shapes=[pltpu.VMEM((SUBLANES, conv_w.shape[1]), F32)],
        compiler_params=pltpu.CompilerParams(
            dimension_semantics=("arbitrary",), vmem_limit_bytes=VMEM_LIMIT),
        name="in_proj",
    )(x2, g, w_all, cq_g, ckv_g, ln_g, ln_b, conv_w, alog_v, dtb_v, w_uq, w_uk, w_iq)


def _topk_bias(score_ref, bias_ref, row0, n_chunks, topk, tri_ones):
    _, nq, kc = score_ref.shape
    rg = nq // SEARCH_GROUPS
    groups = range(SEARCH_GROUPS)
    rsum = lambda x: jnp.sum(x, axis=1, keepdims=True)
    rmin = lambda x: jnp.min(x, axis=1, keepdims=True)

    def ld(g):
        return jnp.concatenate(
            [score_ref[c, g * rg:(g + 1) * rg, :] for c in range(n_chunks)], axis=1)

    pos = lambda g: row0 + g * rg + lax.broadcasted_iota(jnp.int32, (rg, 1), 0)
    k_eff = [jnp.minimum(pos(g) + 1, topk).astype(F32) for g in groups]
    lo = tuple(rmin(jnp.where(ld(g) == -jnp.inf, jnp.inf, ld(g))) for g in groups)
    hi = tuple(jnp.max(ld(g), axis=1, keepdims=True) for g in groups)

    def bisect(_, carry):
        lo, hi = carry
        mid = [0.5 * (a + b) for a, b in zip(lo, hi)]
        cnt = [rsum(jnp.where(ld(g) >= mid[g], 1.0, 0.0)) for g in groups]
        ge = [cnt[g] >= k_eff[g] for g in groups]
        return (tuple(jnp.where(ge[g], mid[g], lo[g]) for g in groups),
                tuple(jnp.where(ge[g], hi[g], mid[g]) for g in groups))

    lo, hi = lax.fori_loop(0, BISECT_STEPS, bisect, (lo, hi), unroll=BISECT_UNROLL)

    def above(g, v):
        s = ld(g)
        gt = s > v
        return rsum(jnp.where(gt, 1.0, 0.0)), rmin(jnp.where(gt, s, jnp.inf))

    v0 = tuple(rmin(jnp.where(ld(g) >= lo[g], ld(g), jnp.inf)) for g in groups)
    first = [above(g, v0[g]) for g in groups]

    def cond(c):
        _, n_gt, _ = c
        flags = [jnp.max(jnp.where(n_gt[g] >= k_eff[g], 1.0, 0.0)) for g in groups]
        return functools.reduce(jnp.maximum, flags) > 0.0

    def body(c):
        v, n_gt, nxt = c
        v = tuple(jnp.where(n_gt[g] >= k_eff[g], nxt[g], v[g]) for g in groups)
        nxt_state = [above(g, v[g]) for g in groups]
        return v, tuple(a for a, _ in nxt_state), tuple(b for _, b in nxt_state)

    tau, n_gt, _ = lax.while_loop(
        cond, body, (v0, tuple(a for a, _ in first), tuple(b for _, b in first)))
    k_all = jnp.concatenate(k_eff, axis=0)
    need = k_all - jnp.concatenate(n_gt, axis=0)
    tau = jnp.concatenate(tau, axis=0)

    n_ge = jnp.zeros((nq, LANES), F32)
    for c in range(n_chunks):
        for j in range(kc // LANES):
            ge = score_ref[c, :, j * LANES:(j + 1) * LANES] >= tau
            bias_ref[c, :, j * LANES:(j + 1) * LANES] = jnp.where(ge, 0.0, NEG_BIG)
            n_ge = n_ge + jnp.where(ge, 1.0, 0.0)
    surplus = jnp.max(jnp.where(rsum(n_ge) > k_all, 1.0, 0.0)) > 0.0

    @pl.when(surplus)
    def _():
        carry = jnp.zeros((nq, LANES), F32)
        for c in range(n_chunks):
            for j in range(kc // LANES):
                sc = score_ref[c, :, j * LANES:(j + 1) * LANES]
                eq = sc == tau
                r = jnp.dot(jnp.where(eq, 1.0, 0.0).astype(BF16), tri_ones,
                            preferred_element_type=F32)
                prefix = r[:, :LANES] + carry
                carry = carry + r[:, LANES:]
                sel = (sc > tau) | (eq & (prefix <= need))
                bias_ref[c, :, j * LANES:(j + 1) * LANES] = jnp.where(sel, 0.0, NEG_BIG)


def _dsa_kernel(qlat_ref, qidx_ref, small_ref, ckv_ref, kidx_ref, wuv_ref,
                o_ref, score_ref, bias_ref, s_ref, m_ref, l_ref, acc_ref, *, topk):
    i = pl.program_id(1)
    nq = Q_BLOCK
    max_chunks, _, kc = score_ref.shape
    n_kc = (i * nq) // kc + 1
    ng = A_HEADS // HEAD_GROUP
    rows = HEAD_GROUP * nq
    tiles = kc // LANES
    nt = (((1,), (1,)), ((), ()))
    w_idx = small_ref[0][:, OFF_WIDX:OFF_WIDX + IDX_HEADS] * (IDX_HEADS ** -0.5 * IDX_DIM ** -0.5)
    row_pos = i * nq + lax.broadcasted_iota(jnp.int32, (nq, 1), 0)

    r_i = lax.broadcasted_iota(jnp.int32, (LANES, 2 * LANES), 0)
    c_i = lax.broadcasted_iota(jnp.int32, (LANES, 2 * LANES), 1)
    tri_ones = jnp.where((c_i >= LANES) | (r_i <= c_i), 1.0, 0.0).astype(BF16)

    def key_rows(c):
        return slice(c * kc, (c + 1) * kc)

    def score_chunk(c):
        logits = lax.dot_general(qidx_ref[0].reshape(IDX_HEADS * nq, IDX_DIM),
                                 kidx_ref[0, key_rows(c), :], nt,
                                 preferred_element_type=F32)
        sc = jnp.zeros((nq, kc), F32)
        for h in range(IDX_HEADS):
            sc = sc + w_idx[:, h:h + 1] * jnp.maximum(logits[h * nq:(h + 1) * nq], 0.0)
        col = c * kc + lax.broadcasted_iota(jnp.int32, (nq, kc), 1)
        score_ref[c] = jnp.where(col <= row_pos, sc, -jnp.inf)

    def pass1(c):
        ckv = ckv_ref[0, key_rows(c), :]
        bias = bias_ref[c]
        ss = [lax.dot_general(
            qlat_ref[0, g * HEAD_GROUP:(g + 1) * HEAD_GROUP].reshape(rows, A_KV_RANK), ckv, nt,
            preferred_element_type=F32) for g in range(ng)]
        for g in range(ng):
            s = (ss[g].reshape(HEAD_GROUP, nq, kc) + bias[None]).reshape(rows, kc)
            s_ref[c, g] = s
            mt = s[:, :LANES]
            for j in range(1, tiles):
                mt = jnp.maximum(mt, s[:, j * LANES:(j + 1) * LANES])
            m_ref[g] = mt if c == 0 else jnp.maximum(m_ref[g], mt)

    def pass2(c):
        ckv = ckv_ref[0, key_rows(c), :]
        for g in range(ng):
            s = s_ref[c, g]
            m = m_ref[g]
            p = [jnp.exp2(s[:, j * LANES:(j + 1) * LANES] - m) for j in range(tiles)]
            lsum = functools.reduce(lambda a, b: a + b, p)
            pv = jnp.dot(jnp.concatenate(p, axis=1).astype(BF16), ckv,
                         preferred_element_type=F32)
            l_ref[g] = lsum if c == 0 else l_ref[g] + lsum
            acc_ref[g] = pv if c == 0 else acc_ref[g] + pv

    def block(nk, search):
        for c in range(nk):
            score_chunk(c)
        if search:
            _topk_bias(score_ref, bias_ref, i * nq, nk, topk, tri_ones)
        else:
            bias_ref[0] = jnp.where(score_ref[0] > -jnp.inf, 0.0, NEG_BIG)
        for c in range(nk):
            pass1(c)
        for g in range(ng):
            m_ref[g] = jnp.broadcast_to(jnp.max(m_ref[g], axis=1, keepdims=True), (rows, LANES))
        for c in range(nk):
            pass2(c)
        for g in range(ng):
            l = jnp.sum(l_ref[g], axis=1, keepdims=True)
            og = (acc_ref[g] * (1.0 / l)).astype(BF16)
            for hh in range(HEAD_GROUP):
                h = g * HEAD_GROUP + hh
                o_ref[0, :, h * A_V_DIM:(h + 1) * A_V_DIM] = jnp.dot(
                    og[hh * nq:(hh + 1) * nq], wuv_ref[h],
                    preferred_element_type=F32).astype(o_ref.dtype)

    keep_all = (i + 1) * nq <= topk
    pl.when(keep_all)(functools.partial(block, 1, False))
    for v in range(max_chunks):
        pl.when(jnp.logical_and(n_kc == v + 1, jnp.logical_not(keep_all)))(
            functools.partial(block, v + 1, True))


def _dsa(qlat, qidx, small, ckv, kidx, w_uv):
    B, L, _ = ckv.shape
    topk = min(TOPK_MAX, L // 4)
    kc = min(KEY_CHUNK, L)
    assert L % kc == 0 and kc % Q_BLOCK == 0 and topk <= kc
    n_chunks = L // kc
    nb = L // Q_BLOCK
    ng = A_HEADS // HEAD_GROUP
    rows = HEAD_GROUP * Q_BLOCK
    a_width = A_HEADS * A_V_DIM
    return pl.pallas_call(
        functools.partial(_dsa_kernel, topk=topk),
        grid=(B, nb),
        in_specs=[
            pl.BlockSpec((1, A_HEADS, Q_BLOCK, A_KV_RANK), lambda b, i: (b * nb + i, 0, 0, 0)),
            pl.BlockSpec((1, IDX_HEADS, Q_BLOCK, IDX_DIM), lambda b, i: (b * nb + i, 0, 0, 0)),
            pl.BlockSpec((1, Q_BLOCK, SMALL_W), lambda b, i: (b, i, 0)),
            pl.BlockSpec((1, L, A_KV_RANK), lambda b, i: (b, 0, 0)),
            pl.BlockSpec((1, L, IDX_DIM), lambda b, i: (b, 0, 0)),
            _resident(w_uv.shape),
        ],
        out_specs=pl.BlockSpec((1, Q_BLOCK, a_width), lambda b, i: (b, i, 0)),
        out_shape=jax.ShapeDtypeStruct((B, L, a_width), BF16),
        scratch_shapes=[
            pltpu.VMEM((n_chunks, Q_BLOCK, kc), F32),
            pltpu.VMEM((n_chunks, Q_BLOCK, kc), F32),
            pltpu.VMEM((n_chunks, ng, rows, kc), F32),
            pltpu.VMEM((ng, rows, LANES), F32),
            pltpu.VMEM((ng, rows, LANES), F32),
            pltpu.VMEM((ng, rows, A_KV_RANK), F32),
        ],
        compiler_params=pltpu.CompilerParams(
            dimension_semantics=("arbitrary", "arbitrary"), vmem_limit_bytes=VMEM_LIMIT),
        name="dsa",
    )(qlat, qidx, small, ckv, kidx, w_uv)


def _lane_bcast(x, c):
    lane = lax.broadcasted_iota(jnp.int32, x.shape, 1)
    col = jnp.sum(jnp.where(lane == c, x, 0.0), axis=1, keepdims=True)
    return jnp.broadcast_to(col, x.shape)


def _gdn_kernel(q_ref, k_ref, v_ref, z_ref, gates_ref, og_ref, o_ref,
                mneg_ref, r_ref, qeff_ref, o0_ref, at_ref, *, seq, hp):
    hg = pl.program_id(1)
    C = CHUNK
    n_chunks = seq // C
    dk = B_K_DIM

    R = PACK * C
    rr = lax.broadcasted_iota(jnp.int32, (R, R), 0)
    cc = lax.broadcasted_iota(jnp.int32, (R, R), 1)
    same = (rr // C) == (cc // C)
    tri = same & (rr >= cc)
    strict = same & (rr > cc)
    eye = jnp.where(rr == cc, 1.0, 0.0)
    kcol = lax.broadcasted_iota(jnp.int32, (dk, R), 1) // C
    nt = (((1,), (1,)), ((), ()))
    dot = functools.partial(jnp.dot, preferred_element_type=F32)

    def prep(t, _):
        ids = [(hh, t * PREP_GROUP + j) for j in range(PREP_GROUP) for hh in range(hp)]
        each = lambda f, *cols: [f(*a) for a in zip(*cols)]
        rows = [pl.ds(pl.multiple_of(n * R, R), R) for _, n in ids]
        load = lambda ref: [ref[0, r, hh * dk:(hh + 1) * dk] for (hh, _), r in zip(ids, rows)]
        qb, kb, vb = (load(r) for r in (q_ref, k_ref, v_ref))
        q, k, v = (each(lambda a: a.astype(F32), x) for x in (qb, kb, vb))
        gt = [gates_ref[0, r, :] for r in rows]
        beta = [_lane_bcast(a, OFF_BETA + hg * hp + hh) for a, (hh, _) in zip(gt, ids)]
        G = [_lane_bcast(a, OFF_A + hg * hp + hh) for a, (hh, _) in zip(gt, ids)]
        Gl = each(lambda g: jnp.concatenate(
            [jnp.broadcast_to(g[(p + 1) * C - 1:(p + 1) * C, :], (C, LANES)) for p in range(PACK)],
            axis=0), G)
        decay = each(lambda g: jnp.exp(jnp.where(tri, g[:, :R] - g.T[:R, :R], -jnp.inf)), G)
        eG = each(jnp.exp, G)
        qkk = each(lambda q_, k_: lax.dot_general(
            jnp.concatenate([q_, k_], axis=0), k_, nt, preferred_element_type=F32), qb, kb)
        N = each(lambda b, a, d: jnp.where(strict, b[:, :R] * a[R:] * d, 0.0), beta, qkk, decay)
        X = each(lambda a: eye - a, N)
        Nb = each(lambda a: a.astype(BF16), N)
        Pb = each(lambda a: dot(a, a).astype(BF16), Nb)
        steps = C.bit_length() - 2
        for it in range(steps):
            if it + 1 < steps:
                xp = each(lambda x, p: dot(jnp.concatenate([x.astype(BF16), p], axis=0), p), X, Pb)
                X = each(lambda x, a: x + a[:R], X, xp)
                Pb = each(lambda a: a[R:].astype(BF16), xp)
            else:
                X = each(lambda x, p: x + dot(x.astype(BF16), p), X, Pb)
        rhs = each(lambda v_, k_, b, e: jnp.concatenate(
            [v_ * b, k_ * (b * e)], axis=-1).astype(BF16), v, k, beta, eG)
        sol = each(lambda x, r: dot(x.astype(BF16), r).astype(BF16), X, rhs)
        qk = each(lambda a, d: (a[:R] * d).astype(BF16), qkk, decay)
        ktT = each(lambda k_, g, gl: (k_ * jnp.exp(gl - g)).T.astype(BF16), k, G, Gl)
        ktbd = each(lambda a: jnp.concatenate(
            [jnp.where(kcol == p, a, jnp.zeros_like(a)) for p in range(PACK)], axis=0), ktT)
        kts = each(dot, ktbd, sol)
        qks = each(dot, qk, sol)
        for i, (hh, n) in enumerate(ids):
            qe = q[i] * eG[i]
            for p in range(PACK):
                ch = n * PACK + p
                r_ref[hh, ch] = kts[i][p * dk:(p + 1) * dk, :B_V_DIM]
                mneg_ref[hh, ch] = kts[i][p * dk:(p + 1) * dk, B_V_DIM:].astype(BF16)
                o0_ref[hh, ch] = qks[i][p * C:(p + 1) * C, :B_V_DIM]
                qeff_ref[hh, ch] = (qe[p * C:(p + 1) * C]
                                    - qks[i][p * C:(p + 1) * C, B_V_DIM:]).astype(BF16)
                at_ref[hh, ch] = jnp.exp(G[i][(p + 1) * C - 1:(p + 1) * C, :])
        return 0

    lax.fori_loop(0, n_chunks // (PREP_GROUP * PACK), prep, 0)

    og = og_ref[...]

    def emit(n, o):
        rows = pl.ds(n * C if isinstance(n, int) else pl.multiple_of(n * C, C), C)
        for hh in range(hp):
            ls = slice(hh * dk, (hh + 1) * dk)
            o_ref[0, rows, ls] = (_rms(o[hh], og) * _silu(z_ref[0, rows, ls])).astype(o_ref.dtype)

    def scan(n, carry):
        S, o_prev = carry
        emit((n + n_chunks - 1) % n_chunks, o_prev)
        Sb = [s.astype(BF16) for s in S]
        ms = [dot(mneg_ref[hh, n], Sb[hh]) for hh in range(hp)]
        os_ = [dot(qeff_ref[hh, n], Sb[hh]) for hh in range(hp)]
        new = tuple(S[hh] * at_ref[hh, n] - ms[hh] + r_ref[hh, n] for hh in range(hp))
        return new, tuple(os_[hh] + o0_ref[hh, n] for hh in range(hp))

    zeros = lambda rows_: tuple(jnp.zeros((rows_, B_V_DIM), F32) for _ in range(hp))
    _, o_last = lax.fori_loop(0, n_chunks, scan, (zeros(dk), zeros(C)))
    emit(n_chunks - 1, o_last)


def _gdn(qn, kn, vn, zb, gates, onorm_g):
    B, L, W = qn.shape
    hp = GDN_HEADS_PER_STEP
    n_chunks = L // CHUNK
    heads = W // B_K_DIM
    assert heads % hp == 0 and n_chunks % (PREP_GROUP * PACK) == 0 and PACK * CHUNK <= LANES
    col = pl.BlockSpec((1, L, hp * B_K_DIM), lambda b, h: (b, 0, h))
    return pl.pallas_call(
        functools.partial(_gdn_kernel, seq=L, hp=hp),
        grid=(B, heads // hp),
        in_specs=[col, col, col, col,
                  pl.BlockSpec((1, L, SMALL_W), lambda b, h: (b, 0, 0)),
                  _resident((1, B_V_DIM))],
        out_specs=col,
        out_shape=jax.ShapeDtypeStruct((B, L, W), BF16),
        scratch_shapes=[
            pltpu.VMEM((hp, n_chunks, B_K_DIM, B_K_DIM), BF16),
            pltpu.VMEM((hp, n_chunks, B_K_DIM, B_V_DIM), F32),
            pltpu.VMEM((hp, n_chunks, CHUNK, B_K_DIM), BF16),
            pltpu.VMEM((hp, n_chunks, CHUNK, B_V_DIM), F32),
            pltpu.VMEM((hp, n_chunks, 1, LANES), F32),
        ],
        compiler_params=pltpu.CompilerParams(
            dimension_semantics=("arbitrary", "arbitrary"), vmem_limit_bytes=VMEM_LIMIT),
        name="gdn",
    )(qn, kn, vn, zb, gates, onorm_g)


def _merge_kernel(x_ref, oa_ref, ob_ref, ga_ref, gb_ref, wa_ref, wb_ref, wo_ref,
                  fg_ref, wg_ref, wu_ref, wd_ref, og_ref, o_ref, *, ff_chunk, final_norm):
    dot = functools.partial(jnp.dot, preferred_element_type=F32)
    n = x_ref.shape[0] // MERGE_SUB
    subs = [slice(s * n, (s + 1) * n) for s in range(MERGE_SUB)]
    ya = [dot(oa_ref[r, :], wa_ref[...]) for r in subs]
    yb = [dot(ob_ref[r, :], wb_ref[...]) for r in subs]
    merged = [(_sigmoid(ga_ref[r, :]) * a + _sigmoid(gb_ref[r, :]) * b).astype(BF16)
              for r, a, b in zip(subs, ya, yb)]
    acc = [x_ref[r, :] + dot(m, wo_ref[...]) for r, m in zip(subs, merged)]
    h = [_rms(a, fg_ref[...]).astype(BF16) for a in acc]
    d_ff = wg_ref.shape[1]
    for c0 in range(0, d_ff, ff_chunk):
        gate = [dot(v, wg_ref[:, c0:c0 + ff_chunk]) for v in h]
        up = [dot(v, wu_ref[:, c0:c0 + ff_chunk]) for v in h]
        act = [(_silu(g) * u).astype(BF16) for g, u in zip(gate, up)]
        acc = [a + dot(v, wd_ref[c0:c0 + ff_chunk, :]) for a, v in zip(acc, act)]
    for r, a in zip(subs, acc):
        o_ref[r, :] = _rms(a, og_ref[...]) if final_norm else a


def _merge(x2, oa, ob, ga, gb, wa, wb, wo, fg, wg, wu, wd, og, *, tm, final_norm):
    T, D = x2.shape
    d_ff = wg.shape[1]
    ff_chunk = d_ff // 2 if (d_ff // 2) % LANES == 0 else d_ff
    row = pl.BlockSpec((tm, D), lambda i: (i, 0))
    return pl.pallas_call(
        functools.partial(_merge_kernel, ff_chunk=ff_chunk, final_norm=final_norm),
        grid=(T // tm,),
        in_specs=[row, row, row, row, row,
                  _resident(wa.shape), _resident(wb.shape), _resident(wo.shape),
                  _resident(fg.shape), _resident(wg.shape), _resident(wu.shape),
                  _resident(wd.shape), _resident(og.shape)],
        out_specs=row,
        out_shape=jax.ShapeDtypeStruct((T, D), F32),
        compiler_params=pltpu.CompilerParams(
            dimension_semantics=("arbitrary",), vmem_limit_bytes=VMEM_LIMIT),
        name="merge_ffn",
    )(x2, oa, ob, ga, gb, wa, wb, wo, fg, wg, wu, wd, og)


def _reorder_w_in(w):
    sizes = (A_Q_RANK, A_KV_RANK, IDX_DIM, IDX_HEADS,
             B_HEADS * B_K_DIM, B_HEADS * B_K_DIM, B_HEADS * B_V_DIM, B_HEADS, B_HEADS,
             B_HEADS * B_V_DIM, w.shape[0], w.shape[0])
    parts, c0 = [], 0
    for s in sizes:
        parts.append(w[:, c0:c0 + s])
        c0 += s
    (c_q, c_kv, k_idx, w_idx, q_b, k_b, v_b, beta_b, a_b, z_b, gate_a, gate_b) = parts
    pad = jnp.zeros((w.shape[0], SMALL_W - IDX_DIM - IDX_HEADS - 2 * B_HEADS), w.dtype)
    return jnp.concatenate(
        [q_b, k_b, v_b, z_b, gate_a, gate_b, c_q, c_kv, k_idx, w_idx, beta_b, a_b, pad],
        axis=1).astype(BF16)


def kernel(x, mix_norm_g, w_in, cq_norm_g, ckv_norm_g, w_uq, w_uk, w_uv, w_iq, kidx_ln_g, kidx_ln_b, w_branch_a, conv_w, a_log, dt_bias, onorm_g, w_branch_b, w_out, ffn_norm_g, w_gate, w_up, w_down, final_norm_g):
    B, L, D = x.shape
    depth = w_in.shape[0]
    T = B * L
    tm = min(TOKEN_TILE, L)
    x2 = x.reshape(T, D)
    vec = lambda a: a.reshape(1, -1).astype(F32)
    at_a = lambda a: jnp.zeros((1, SMALL_W), F32).at[0, OFF_A:OFF_A + B_HEADS].set(a.astype(F32))
    for l in range(depth):
        (qlat, qidx, ckv, kidx, small, gates, qn, kn, vn, zb, ga, gb) = _in_proj(
            x2, vec(mix_norm_g[l]), _reorder_w_in(w_in[l]), vec(cq_norm_g[l]),
            vec(ckv_norm_g[l]), vec(kidx_ln_g[l]), vec(kidx_ln_b[l]), conv_w[l].astype(F32),
            at_a(a_log[l]), at_a(dt_bias[l]), w_uq[l].astype(BF16), w_uk[l].astype(BF16),
            w_iq[l].astype(BF16), tm=tm, seq=L)
        seq = lambda a: a.reshape(B, L, a.shape[-1])
        o_a = _dsa(qlat, qidx, seq(small), seq(ckv), seq(kidx), w_uv[l].astype(BF16))
        o_b = _gdn(seq(qn), seq(kn), seq(vn), seq(zb), seq(gates), vec(onorm_g[l]))
        x2 = _merge(x2, o_a.reshape(T, -1), o_b.reshape(T, -1), ga, gb,
                    w_branch_a[l].astype(BF16), w_branch_b[l].astype(BF16),
                    w_out[l].astype(BF16), vec(ffn_norm_g[l]), w_gate[l].astype(BF16),
                    w_up[l].astype(BF16), w_down[l].astype(BF16), vec(final_norm_g),
                    tm=min(MERGE_TILE, T), final_norm=(l == depth - 1))
    return x2.reshape(B, L, D)
```

```python
import functools

import jax
import jax.numpy as jnp
from jax import lax
from jax.experimental import pallas as pl
from jax.experimental.pallas import tpu as pltpu

F32 = jnp.float32
BF16 = jnp.bfloat16

EPS = 1e-6
A_HEADS = 16
A_QK_DIM = 64
A_V_DIM = 64
A_Q_RANK = 256
A_KV_RANK = 256
IDX_HEADS = 8
IDX_DIM = 64
TOPK_MAX = 256
Q_BLOCK = 128
B_HEADS = 8
B_K_DIM = 128
B_V_DIM = 128
CONV_WIDTH = 4
CHUNK = 64

GDN_W = B_HEADS * B_K_DIM
assert B_K_DIM == B_V_DIM
LANES = 128
SUBLANES = 8
SMALL_W = LANES
OFF_WIDX = IDX_DIM
OFF_BETA = IDX_DIM + IDX_HEADS
OFF_A = IDX_DIM + IDX_HEADS + B_HEADS
VMEM_LIMIT = 56 * 1024 * 1024
NEG_BIG = -1e30
LOG2_E = 1.4426950408889634
BISECT_STEPS = 16
BISECT_UNROLL = 16
TOKEN_TILE = 512
IN_SUB = 2
MERGE_TILE = 512
MERGE_SUB = 2
HEAD_GROUP = 4
KEY_CHUNK = 512
SEARCH_GROUPS = 4
GDN_HEADS_PER_STEP = 4
PACK = 2
PREP_GROUP = 4


def _resident(shape):
    nd = len(shape)
    return pl.BlockSpec(shape, lambda *_: (0,) * nd, pipeline_mode=pl.Buffered(1))


def _rms(x, g):
    return x * lax.rsqrt(jnp.mean(x * x, axis=-1, keepdims=True) + EPS) * g


def _l2n(x, scale=1.0):
    return x * (lax.rsqrt(jnp.sum(x * x, axis=-1, keepdims=True) + EPS) * scale)


def _sigmoid(x):
    return 0.5 * jnp.tanh(0.5 * x) + 0.5


def _silu(x):
    u = 0.5 * x
    return u + u * jnp.tanh(u)


def _chunk_cumsum(g, chunk):
    pos = lax.broadcasted_iota(jnp.int32, g.shape, 0) % chunk
    d = 1
    while d < chunk:
        g = g + jnp.where(pos >= d, pltpu.roll(g, d, 0), 0.0)
        d *= 2
    return g


def _conv_silu_tile(x, tail_ref, w_half):
    full = jnp.concatenate([tail_ref[...], x], axis=0)
    u = x * w_half[CONV_WIDTH - 1:CONV_WIDTH, :]
    for d in range(1, CONV_WIDTH):
        u = u + pltpu.roll(full, d, 0)[SUBLANES:] * w_half[CONV_WIDTH - 1 - d:CONV_WIDTH - d, :]
    tail_ref[...] = x[x.shape[0] - SUBLANES:]
    return u + u * jnp.tanh(u)


def _in_proj_kernel(x_ref, g_ref, w_ref, cqg_ref, ckvg_ref, lng_ref, lnb_ref, cw_ref,
                    alog_ref, dtb_ref, wuq_ref, wuk_ref, wiq_ref,
                    qlat_ref, qidx_ref, ckv_ref, kidx_ref, small_ref, gates_ref, q_ref, k_ref, v_ref,
                    z_ref, ga_ref, gb_ref, tail_ref, *, d_model, tiles_per_seq):
    @pl.when(pl.program_id(0) % tiles_per_seq == 0)
    def _():
        tail_ref[...] = jnp.zeros_like(tail_ref)

    n = x_ref.shape[0] // IN_SUB
    subs = [slice(s * n, (s + 1) * n) for s in range(IN_SUB)]
    hs = [_rms(x_ref[r, :], g_ref[...]).astype(BF16) for r in subs]

    def proj(s, c0, width):
        return jnp.dot(hs[s], w_ref[:, c0:c0 + width], preferred_element_type=F32)

    pair = 2 * B_K_DIM
    plain, c0 = [], 3 * GDN_W
    for ref, width in ((z_ref, GDN_W), (ga_ref, d_model), (gb_ref, d_model)):
        plain += [(ref, c, c0 + c) for c in range(0, width, pair)]
        c0 += width
    steps = [(part, c) for part in range(3) for c in range(0, GDN_W, pair)]
    for idx, (part, c) in enumerate(steps):
        ref = (q_ref, k_ref, v_ref)[part]
        cc = part * GDN_W + c
        lo, hi = idx * len(plain) // len(steps), (idx + 1) * len(plain) // len(steps)
        for s, r in enumerate(subs):
            y = _conv_silu_tile(proj(s, cc, pair), tail_ref.at[:, cc:cc + pair],
                                cw_ref[:, cc:cc + pair] * 0.5)
            if part < 2:
                scale = B_K_DIM ** -0.5 if part == 0 else 1.0
                y = jnp.concatenate(
                    [_l2n(y[:, :B_K_DIM], scale), _l2n(y[:, B_K_DIM:], scale)], axis=1)
            ref[r, c:c + pair] = y.astype(ref.dtype)
            for pref, pc, wc in plain[lo:hi]:
                pref[r, pc:pc + pair] = proj(s, wc, pair)
    blocks = n // Q_BLOCK
    for s, r in enumerate(subs):
        c1 = c0
        cq = _rms(proj(s, c1, A_Q_RANK), cqg_ref[...]).astype(BF16)
        c1 += A_Q_RANK
        qa = jnp.dot(cq, wuq_ref[...], preferred_element_type=F32).astype(BF16)
        qi = jnp.dot(cq, wiq_ref[...], preferred_element_type=F32).astype(BF16)
        for hd in range(A_HEADS):
            ql = (jnp.dot(qa[:, hd * A_QK_DIM:(hd + 1) * A_QK_DIM], wuk_ref[hd],
                          preferred_element_type=F32) * (A_QK_DIM ** -0.5 * LOG2_E)).astype(BF16)
            for j in range(blocks):
                qlat_ref[s * blocks + j, hd] = ql[j * Q_BLOCK:(j + 1) * Q_BLOCK]
        for hd in range(IDX_HEADS):
            for j in range(blocks):
                qidx_ref[s * blocks + j, hd] = qi[j * Q_BLOCK:(j + 1) * Q_BLOCK,
                                                  hd * IDX_DIM:(hd + 1) * IDX_DIM]
        ckv_ref[r, :] = _rms(proj(s, c1, A_KV_RANK), ckvg_ref[...]).astype(BF16)
        c1 += A_KV_RANK
        small = proj(s, c1, SMALL_W)
        small_ref[r, :] = small
        kraw = small[:, :IDX_DIM]
        mu = jnp.mean(kraw, axis=-1, keepdims=True)
        kc = kraw - mu
        kn = kc * lax.rsqrt(jnp.mean(kc * kc, axis=-1, keepdims=True) + EPS)
        kidx_ref[r, :] = (kn * lng_ref[...] + lnb_ref[...]).astype(BF16)
        g = -jnp.exp(alog_ref[...]) * jax.nn.softplus(small + dtb_ref[...])
        lane = lax.broadcasted_iota(jnp.int32, small.shape, 1)
        gates_ref[r, :] = jnp.where(lane >= OFF_A, _chunk_cumsum(g, CHUNK), _sigmoid(small))


def _in_proj(x2, g, w_all, cq_g, ckv_g, ln_g, ln_b, conv_w, alog_v, dtb_v, w_uq, w_uk, w_iq,
             *, tm, seq):
    T, D = x2.shape
    n_all = w_all.shape[1]
    assert seq % tm == 0 and tm % (IN_SUB * Q_BLOCK) == 0 and Q_BLOCK % CHUNK == 0
    row = lambda w: pl.BlockSpec((tm, w), lambda i: (i, 0))
    qblk = lambda heads, w: pl.BlockSpec((tm // Q_BLOCK, heads, Q_BLOCK, w),
                                         lambda i: (i, 0, 0, 0))
    wide = ((GDN_W, BF16), (GDN_W, BF16), (GDN_W, BF16), (GDN_W, F32),
            (D, F32), (D, F32))
    out_shapes = (
        jax.ShapeDtypeStruct((T // Q_BLOCK, A_HEADS, Q_BLOCK, A_KV_RANK), BF16),
        jax.ShapeDtypeStruct((T // Q_BLOCK, IDX_HEADS, Q_BLOCK, IDX_DIM), BF16),
        jax.ShapeDtypeStruct((T, A_KV_RANK), BF16),
        jax.ShapeDtypeStruct((T, IDX_DIM), BF16),
        jax.ShapeDtypeStruct((T, SMALL_W), F32),
        jax.ShapeDtypeStruct((T, SMALL_W), F32),
    ) + tuple(jax.ShapeDtypeStruct((T, w), dt) for w, dt in wide)
    out_specs = (qblk(A_HEADS, A_KV_RANK), qblk(IDX_HEADS, IDX_DIM), row(A_KV_RANK),
                 row(IDX_DIM), row(SMALL_W), row(SMALL_W)) + tuple(row(w) for w, _ in wide)
    return pl.pallas_call(
        functools.partial(_in_proj_kernel, d_model=D, tiles_per_seq=seq // tm),
        grid=(T // tm,),
        in_specs=[row(D), _resident((1, D)), _resident((D, n_all)),
                  _resident((1, A_Q_RANK)), _resident((1, A_KV_RANK)),
                  _resident((1, IDX_DIM)), _resident((1, IDX_DIM)),
                  _resident(conv_w.shape), _resident((1, SMALL_W)), _resident((1, SMALL_W)),
                  _resident(w_uq.shape), _resident(w_uk.shape), _resident(w_iq.shape)],
        out_specs=out_specs,
        out_shape=out_shapes,
        scratch_shapes=[pltpu.VMEM((SUBLANES, conv_w.shape[1]), F32)],
        compiler_params=pltpu.CompilerParams(
            dimension_semantics=("arbitrary",), vmem_limit_bytes=VMEM_LIMIT),
        name="in_proj",
    )(x2, g, w_all, cq_g, ckv_g, ln_g, ln_b, conv_w, alog_v, dtb_v, w_uq, w_uk, w_iq)


def _topk_bias(score_ref, bias_ref, row0, n_chunks, topk, tri_ones):
    _, nq, kc = score_ref.shape
    rg = nq // SEARCH_GROUPS
    groups = range(SEARCH_GROUPS)
    rsum = lambda x: jnp.sum(x, axis=1, keepdims=True)
    rmin = lambda x: jnp.min(x, axis=1, keepdims=True)

    def ld(g):
        return jnp.concatenate(
            [score_ref[c, g * rg:(g + 1) * rg, :] for c in range(n_chunks)], axis=1)

    pos = lambda g: row0 + g * rg + lax.broadcasted_iota(jnp.int32, (rg, 1), 0)
    k_eff = [jnp.minimum(pos(g) + 1, topk).astype(F32) for g in groups]
    lo = tuple(rmin(jnp.where(ld(g) == -jnp.inf, jnp.inf, ld(g))) for g in groups)
    hi = tuple(jnp.max(ld(g), axis=1, keepdims=True) for g in groups)

    def bisect(_, carry):
        lo, hi = carry
        mid = [0.5 * (a + b) for a, b in zip(lo, hi)]
        cnt = [rsum(jnp.where(ld(g) >= mid[g], 1.0, 0.0)) for g in groups]
        ge = [cnt[g] >= k_eff[g] for g in groups]
        return (tuple(jnp.where(ge[g], mid[g], lo[g]) for g in groups),
                tuple(jnp.where(ge[g], hi[g], mid[g]) for g in groups))

    lo, hi = lax.fori_loop(0, BISECT_STEPS, bisect, (lo, hi), unroll=BISECT_UNROLL)

    def above(g, v):
        s = ld(g)
        gt = s > v
        return rsum(jnp.where(gt, 1.0, 0.0)), rmin(jnp.where(gt, s, jnp.inf))

    v0 = tuple(rmin(jnp.where(ld(g) >= lo[g], ld(g), jnp.inf)) for g in groups)
    first = [above(g, v0[g]) for g in groups]

    def cond(c):
        _, n_gt, _ = c
        flags = [jnp.max(jnp.where(n_gt[g] >= k_eff[g], 1.0, 0.0)) for g in groups]
        return functools.reduce(jnp.maximum, flags) > 0.0

    def body(c):
        v, n_gt, nxt = c
        v = tuple(jnp.where(n_gt[g] >= k_eff[g], nxt[g], v[g]) for g in groups)
        nxt_state = [above(g, v[g]) for g in groups]
        return v, tuple(a for a, _ in nxt_state), tuple(b for _, b in nxt_state)

    tau, n_gt, _ = lax.while_loop(
        cond, body, (v0, tuple(a for a, _ in first), tuple(b for _, b in first)))
    k_all = jnp.concatenate(k_eff, axis=0)
    need = k_all - jnp.concatenate(n_gt, axis=0)
    tau = jnp.concatenate(tau, axis=0)

    n_ge = jnp.zeros((nq, LANES), F32)
    for c in range(n_chunks):
        for j in range(kc // LANES):
            ge = score_ref[c, :, j * LANES:(j + 1) * LANES] >= tau
            bias_ref[c, :, j * LANES:(j + 1) * LANES] = jnp.where(ge, 0.0, NEG_BIG)
            n_ge = n_ge + jnp.where(ge, 1.0, 0.0)
    surplus = jnp.max(jnp.where(rsum(n_ge) > k_all, 1.0, 0.0)) > 0.0

    @pl.when(surplus)
    def _():
        carry = jnp.zeros((nq, LANES), F32)
        for c in range(n_chunks):
            for j in range(kc // LANES):
                sc = score_ref[c, :, j * LANES:(j + 1) * LANES]
                eq = sc == tau
                r = jnp.dot(jnp.where(eq, 1.0, 0.0).astype(BF16), tri_ones,
                            preferred_element_type=F32)
                prefix = r[:, :LANES] + carry
                carry = carry + r[:, LANES:]
                sel = (sc > tau) | (eq & (prefix <= need))
                bias_ref[c, :, j * LANES:(j + 1) * LANES] = jnp.where(sel, 0.0, NEG_BIG)


def _dsa_kernel(qlat_ref, qidx_ref, small_ref, ckv_ref, kidx_ref, wuv_ref,
                o_ref, score_ref, bias_ref, s_ref, m_ref, l_ref, acc_ref, *, topk):
    i = pl.program_id(1)
    nq = Q_BLOCK
    max_chunks, _, kc = score_ref.shape
    n_kc = (i * nq) // kc + 1
    ng = A_HEADS // HEAD_GROUP
    rows = HEAD_GROUP * nq
    tiles = kc // LANES
    nt = (((1,), (1,)), ((), ()))
    w_idx = small_ref[0][:, OFF_WIDX:OFF_WIDX + IDX_HEADS] * (IDX_HEADS ** -0.5 * IDX_DIM ** -0.5)
    row_pos = i * nq + lax.broadcasted_iota(jnp.int32, (nq, 1), 0)

    r_i = lax.broadcasted_iota(jnp.int32, (LANES, 2 * LANES), 0)
    c_i = lax.broadcasted_iota(jnp.int32, (LANES, 2 * LANES), 1)
    tri_ones = jnp.where((c_i >= LANES) | (r_i <= c_i), 1.0, 0.0).astype(BF16)

    def key_rows(c):
        return slice(c * kc, (c + 1) * kc)

    def score_chunk(c):
        logits = lax.dot_general(qidx_ref[0].reshape(IDX_HEADS * nq, IDX_DIM),
                                 kidx_ref[0, key_rows(c), :], nt,
                                 preferred_element_type=F32)
        sc = jnp.zeros((nq, kc), F32)
        for h in range(IDX_HEADS):
            sc = sc + w_idx[:, h:h + 1] * jnp.maximum(logits[h * nq:(h + 1) * nq], 0.0)
        col = c * kc + lax.broadcasted_iota(jnp.int32, (nq, kc), 1)
        score_ref[c] = jnp.where(col <= row_pos, sc, -jnp.inf)

    def pass1(c):
        ckv = ckv_ref[0, key_rows(c), :]
        bias = bias_ref[c]
        ss = [lax.dot_general(
            qlat_ref[0, g * HEAD_GROUP:(g + 1) * HEAD_GROUP].reshape(rows, A_KV_RANK), ckv, nt,
            preferred_element_type=F32) for g in range(ng)]
        for g in range(ng):
            s = (ss[g].reshape(HEAD_GROUP, nq, kc) + bias[None]).reshape(rows, kc)
            s_ref[c, g] = s
            mt = s[:, :LANES]
            for j in range(1, tiles):
                mt = jnp.maximum(mt, s[:, j * LANES:(j + 1) * LANES])
            m_ref[g] = mt if c == 0 else jnp.maximum(m_ref[g], mt)

    def pass2(c):
        ckv = ckv_ref[0, key_rows(c), :]
        for g in range(ng):
            s = s_ref[c, g]
            m = m_ref[g]
            p = [jnp.exp2(s[:, j * LANES:(j + 1) * LANES] - m) for j in range(tiles)]
            lsum = functools.reduce(lambda a, b: a + b, p)
            pv = jnp.dot(jnp.concatenate(p, axis=1).astype(BF16), ckv,
                         preferred_element_type=F32)
            l_ref[g] = lsum if c == 0 else l_ref[g] + lsum
            acc_ref[g] = pv if c == 0 else acc_ref[g] + pv

    def block(nk, search):
        for c in range(nk):
            score_chunk(c)
        if search:
            _topk_bias(score_ref, bias_ref, i * nq, nk, topk, tri_ones)
        else:
            bias_ref[0] = jnp.where(score_ref[0] > -jnp.inf, 0.0, NEG_BIG)
        for c in range(nk):
            pass1(c)
        for g in range(ng):
            m_ref[g] = jnp.broadcast_to(jnp.max(m_ref[g], axis=1, keepdims=True), (rows, LANES))
        for c in range(nk):
            pass2(c)
        for g in range(ng):
            l = jnp.sum(l_ref[g], axis=1, keepdims=True)
            og = (acc_ref[g] * (1.0 / l)).astype(BF16)
            for hh in range(HEAD_GROUP):
                h = g * HEAD_GROUP + hh
                o_ref[0, :, h * A_V_DIM:(h + 1) * A_V_DIM] = jnp.dot(
                    og[hh * nq:(hh + 1) * nq], wuv_ref[h],
                    preferred_element_type=F32).astype(o_ref.dtype)

    keep_all = (i + 1) * nq <= topk
    pl.when(keep_all)(functools.partial(block, 1, False))
    for v in range(max_chunks):
        pl.when(jnp.logical_and(n_kc == v + 1, jnp.logical_not(keep_all)))(
            functools.partial(block, v + 1, True))


def _dsa(qlat, qidx, small, ckv, kidx, w_uv):
    B, L, _ = ckv.shape
    topk = min(TOPK_MAX, L // 4)
    kc = min(KEY_CHUNK, L)
    assert L % kc == 0 and kc % Q_BLOCK == 0 and topk <= kc
    n_chunks = L // kc
    nb = L // Q_BLOCK
    ng = A_HEADS // HEAD_GROUP
    rows = HEAD_GROUP * Q_BLOCK
    a_width = A_HEADS * A_V_DIM
    return pl.pallas_call(
        functools.partial(_dsa_kernel, topk=topk),
        grid=(B, nb),
        in_specs=[
            pl.BlockSpec((1, A_HEADS, Q_BLOCK, A_KV_RANK), lambda b, i: (b * nb + i, 0, 0, 0)),
            pl.BlockSpec((1, IDX_HEADS, Q_BLOCK, IDX_DIM), lambda b, i: (b * nb + i, 0, 0, 0)),
            pl.BlockSpec((1, Q_BLOCK, SMALL_W), lambda b, i: (b, i, 0)),
            pl.BlockSpec((1, L, A_KV_RANK), lambda b, i: (b, 0, 0)),
            pl.BlockSpec((1, L, IDX_DIM), lambda b, i: (b, 0, 0)),
            _resident(w_uv.shape),
        ],
        out_specs=pl.BlockSpec((1, Q_BLOCK, a_width), lambda b, i: (b, i, 0)),
        out_shape=jax.ShapeDtypeStruct((B, L, a_width), BF16),
        scratch_shapes=[
            pltpu.VMEM((n_chunks, Q_BLOCK, kc), F32),
            pltpu.VMEM((n_chunks, Q_BLOCK, kc), F32),
            pltpu.VMEM((n_chunks, ng, rows, kc), F32),
            pltpu.VMEM((ng, rows, LANES), F32),
            pltpu.VMEM((ng, rows, LANES), F32),
            pltpu.VMEM((ng, rows, A_KV_RANK), F32),
        ],
        compiler_params=pltpu.CompilerParams(
            dimension_semantics=("arbitrary", "arbitrary"), vmem_limit_bytes=VMEM_LIMIT),
        name="dsa",
    )(qlat, qidx, small, ckv, kidx, w_uv)


def _lane_bcast(x, c):
    lane = lax.broadcasted_iota(jnp.int32, x.shape, 1)
    col = jnp.sum(jnp.where(lane == c, x, 0.0), axis=1, keepdims=True)
    return jnp.broadcast_to(col, x.shape)


def _gdn_kernel(q_ref, k_ref, v_ref, z_ref, gates_ref, og_ref, o_ref,
                mneg_ref, r_ref, qeff_ref, o0_ref, at_ref, *, seq, hp):
    hg = pl.program_id(1)
    C = CHUNK
    n_chunks = seq // C
    dk = B_K_DIM

    R = PACK * C
    rr = lax.broadcasted_iota(jnp.int32, (R, R), 0)
    cc = lax.broadcasted_iota(jnp.int32, (R, R), 1)
    same = (rr // C) == (cc // C)
    tri = same & (rr >= cc)
    strict = same & (rr > cc)
    eye = jnp.where(rr == cc, 1.0, 0.0)
    kcol = lax.broadcasted_iota(jnp.int32, (dk, R), 1) // C
    nt = (((1,), (1,)), ((), ()))
    dot = functools.partial(jnp.dot, preferred_element_type=F32)

    def prep(t, _):
        ids = [(hh, t * PREP_GROUP + j) for j in range(PREP_GROUP) for hh in range(hp)]
        each = lambda f, *cols: [f(*a) for a in zip(*cols)]
        rows = [pl.ds(pl.multiple_of(n * R, R), R) for _, n in ids]
        load = lambda ref: [ref[0, r, hh * dk:(hh + 1) * dk] for (hh, _), r in zip(ids, rows)]
        qb, kb, vb = (load(r) for r in (q_ref, k_ref, v_ref))
        q, k, v = (each(lambda a: a.astype(F32), x) for x in (qb, kb, vb))
        gt = [gates_ref[0, r, :] for r in rows]
        beta = [_lane_bcast(a, OFF_BETA + hg * hp + hh) for a, (hh, _) in zip(gt, ids)]
        G = [_lane_bcast(a, OFF_A + hg * hp + hh) for a, (hh, _) in zip(gt, ids)]
        Gl = each(lambda g: jnp.concatenate(
            [jnp.broadcast_to(g[(p + 1) * C - 1:(p + 1) * C, :], (C, LANES)) for p in range(PACK)],
            axis=0), G)
        decay = each(lambda g: jnp.exp(jnp.where(tri, g[:, :R] - g.T[:R, :R], -jnp.inf)), G)
        eG = each(jnp.exp, G)
        qkk = each(lambda q_, k_: lax.dot_general(
            jnp.concatenate([q_, k_], axis=0), k_, nt, preferred_element_type=F32), qb, kb)
        N = each(lambda b, a, d: jnp.where(strict, b[:, :R] * a[R:] * d, 0.0), beta, qkk, decay)
        X = each(lambda a: eye - a, N)
        Nb = each(lambda a: a.astype(BF16), N)
        Pb = each(lambda a: dot(a, a).astype(BF16), Nb)
        steps = C.bit_length() - 2
        for it in range(steps):
            if it + 1 < steps:
                xp = each(lambda x, p: dot(jnp.concatenate([x.astype(BF16), p], axis=0), p), X, Pb)
                X = each(lambda x, a: x + a[:R], X, xp)
                Pb = each(lambda a: a[R:].astype(BF16), xp)
            else:
                X = each(lambda x, p: x + dot(x.astype(BF16), p), X, Pb)
        rhs = each(lambda v_, k_, b, e: jnp.concatenate(
            [v_ * b, k_ * (b * e)], axis=-1).astype(BF16), v, k, beta, eG)
        sol = each(lambda x, r: dot(x.astype(BF16), r).astype(BF16), X, rhs)
        qk = each(lambda a, d: (a[:R] * d).astype(BF16), qkk, decay)
        ktT = each(lambda k_, g, gl: (k_ * jnp.exp(gl - g)).T.astype(BF16), k, G, Gl)
        ktbd = each(lambda a: jnp.concatenate(
            [jnp.where(kcol == p, a, jnp.zeros_like(a)) for p in range(PACK)], axis=0), ktT)
        kts = each(dot, ktbd, sol)
        qks = each(dot, qk, sol)
        for i, (hh, n) in enumerate(ids):
            qe = q[i] * eG[i]
            for p in range(PACK):
                ch = n * PACK + p
                r_ref[hh, ch] = kts[i][p * dk:(p + 1) * dk, :B_V_DIM]
                mneg_ref[hh, ch] = kts[i][p * dk:(p + 1) * dk, B_V_DIM:].astype(BF16)
                o0_ref[hh, ch] = qks[i][p * C:(p + 1) * C, :B_V_DIM]
                qeff_ref[hh, ch] = (qe[p * C:(p + 1) * C]
                                    - qks[i][p * C:(p + 1) * C, B_V_DIM:]).astype(BF16)
                at_ref[hh, ch] = jnp.exp(G[i][(p + 1) * C - 1:(p + 1) * C, :])
        return 0

    lax.fori_loop(0, n_chunks // (PREP_GROUP * PACK), prep, 0)

    og = og_ref[...]

    def emit(n, o):
        rows = pl.ds(n * C if isinstance(n, int) else pl.multiple_of(n * C, C), C)
        for hh in range(hp):
            ls = slice(hh * dk, (hh + 1) * dk)
            o_ref[0, rows, ls] = (_rms(o[hh], og) * _silu(z_ref[0, rows, ls])).astype(o_ref.dtype)

    def scan(n, carry):
        S, o_prev = carry
        emit((n + n_chunks - 1) % n_chunks, o_prev)
        Sb = [s.astype(BF16) for s in S]
        ms = [dot(mneg_ref[hh, n], Sb[hh]) for hh in range(hp)]
        os_ = [dot(qeff_ref[hh, n], Sb[hh]) for hh in range(hp)]
        new = tuple(S[hh] * at_ref[hh, n] - ms[hh] + r_ref[hh, n] for hh in range(hp))
        return new, tuple(os_[hh] + o0_ref[hh, n] for hh in range(hp))

    zeros = lambda rows_: tuple(jnp.zeros((rows_, B_V_DIM), F32) for _ in range(hp))
    _, o_last = lax.fori_loop(0, n_chunks, scan, (zeros(dk), zeros(C)))
    emit(n_chunks - 1, o_last)


def _gdn(qn, kn, vn, zb, gates, onorm_g):
    B, L, W = qn.shape
    hp = GDN_HEADS_PER_STEP
    n_chunks = L // CHUNK
    heads = W // B_K_DIM
    assert heads % hp == 0 and n_chunks % (PREP_GROUP * PACK) == 0 and PACK * CHUNK <= LANES
    col = pl.BlockSpec((1, L, hp * B_K_DIM), lambda b, h: (b, 0, h))
    return pl.pallas_call(
        functools.partial(_gdn_kernel, seq=L, hp=hp),
        grid=(B, heads // hp),
        in_specs=[col, col, col, col,
                  pl.BlockSpec((1, L, SMALL_W), lambda b, h: (b, 0, 0)),
                  _resident((1, B_V_DIM))],
        out_specs=col,
        out_shape=jax.ShapeDtypeStruct((B, L, W), BF16),
        scratch_shapes=[
            pltpu.VMEM((hp, n_chunks, B_K_DIM, B_K_DIM), BF16),
            pltpu.VMEM((hp, n_chunks, B_K_DIM, B_V_DIM), F32),
            pltpu.VMEM((hp, n_chunks, CHUNK, B_K_DIM), BF16),
            pltpu.VMEM((hp, n_chunks, CHUNK, B_V_DIM), F32),
            pltpu.VMEM((hp, n_chunks, 1, LANES), F32),
        ],
        compiler_params=pltpu.CompilerParams(
            dimension_semantics=("arbitrary", "arbitrary"), vmem_limit_bytes=VMEM_LIMIT),
        name="gdn",
    )(qn, kn, vn, zb, gates, onorm_g)


def _merge_kernel(x_ref, oa_ref, ob_ref, ga_ref, gb_ref, wa_ref, wb_ref, wo_ref,
                  fg_ref, wg_ref, wu_ref, wd_ref, og_ref, o_ref, *, ff_chunk, final_norm):
    dot = functools.partial(jnp.dot, preferred_element_type=F32)
    n = x_ref.shape[0] // MERGE_SUB
    subs = [slice(s * n, (s + 1) * n) for s in range(MERGE_SUB)]
    ya = [dot(oa_ref[r, :], wa_ref[...]) for r in subs]
    yb = [dot(ob_ref[r, :], wb_ref[...]) for r in subs]
    merged = [(_sigmoid(ga_ref[r, :]) * a + _sigmoid(gb_ref[r, :]) * b).astype(BF16)
              for r, a, b in zip(subs, ya, yb)]
    acc = [x_ref[r, :] + dot(m, wo_ref[...]) for r, m in zip(subs, merged)]
    h = [_rms(a, fg_ref[...]).astype(BF16) for a in acc]
    d_ff = wg_ref.shape[1]
    for c0 in range(0, d_ff, ff_chunk):
        gate = [dot(v, wg_ref[:, c0:c0 + ff_chunk]) for v in h]
        up = [dot(v, wu_ref[:, c0:c0 + ff_chunk]) for v in h]
        act = [(_silu(g) * u).astype(BF16) for g, u in zip(gate, up)]
        acc = [a + dot(v, wd_ref[c0:c0 + ff_chunk, :]) for a, v in zip(acc, act)]
    for r, a in zip(subs, acc):
        o_ref[r, :] = _rms(a, og_ref[...]) if final_norm else a


def _merge(x2, oa, ob, ga, gb, wa, wb, wo, fg, wg, wu, wd, og, *, tm, final_norm):
    T, D = x2.shape
    d_ff = wg.shape[1]
    ff_chunk = d_ff // 2 if (d_ff // 2) % LANES == 0 else d_ff
    row = pl.BlockSpec((tm, D), lambda i: (i, 0))
    return pl.pallas_call(
        functools.partial(_merge_kernel, ff_chunk=ff_chunk, final_norm=final_norm),
        grid=(T // tm,),
        in_specs=[row, row, row, row, row,
                  _resident(wa.shape), _resident(wb.shape), _resident(wo.shape),
                  _resident(fg.shape), _resident(wg.shape), _resident(wu.shape),
                  _resident(wd.shape), _resident(og.shape)],
        out_specs=row,
        out_shape=jax.ShapeDtypeStruct((T, D), F32),
        compiler_params=pltpu.CompilerParams(
            dimension_semantics=("arbitrary",), vmem_limit_bytes=VMEM_LIMIT),
        name="merge_ffn",
    )(x2, oa, ob, ga, gb, wa, wb, wo, fg, wg, wu, wd, og)


def _reorder_w_in(w):
    sizes = (A_Q_RANK, A_KV_RANK, IDX_DIM, IDX_HEADS,
             B_HEADS * B_K_DIM, B_HEADS * B_K_DIM, B_HEADS * B_V_DIM, B_HEADS, B_HEADS,
             B_HEADS * B_V_DIM, w.shape[0], w.shape[0])
    parts, c0 = [], 0
    for s in sizes:
        parts.append(w[:, c0:c0 + s])
        c0 += s
    (c_q, c_kv, k_idx, w_idx, q_b, k_b, v_b, beta_b, a_b, z_b, gate_a, gate_b) = parts
    pad = jnp.zeros((w.shape[0], SMALL_W - IDX_DIM - IDX_HEADS - 2 * B_HEADS), w.dtype)
    return jnp.concatenate(
        [q_b, k_b, v_b, z_b, gate_a, gate_b, c_q, c_kv, k_idx, w_idx, beta_b, a_b, pad],
        axis=1).astype(BF16)


def kernel(x, mix_norm_g, w_in, cq_norm_g, ckv_norm_g, w_uq, w_uk, w_uv, w_iq, kidx_ln_g, kidx_ln_b, w_branch_a, conv_w, a_log, dt_bias, onorm_g, w_branch_b, w_out, ffn_norm_g, w_gate, w_up, w_down, final_norm_g):
    B, L, D = x.shape
    depth = w_in.shape[0]
    T = B * L
    tm = min(TOKEN_TILE, L)
    x2 = x.reshape(T, D)
    vec = lambda a: a.reshape(1, -1).astype(F32)
    at_a = lambda a: jnp.zeros((1, SMALL_W), F32).at[0, OFF_A:OFF_A + B_HEADS].set(a.astype(F32))
    for l in range(depth):
        (qlat, qidx, ckv, kidx, small, gates, qn, kn, vn, zb, ga, gb) = _in_proj(
            x2, vec(mix_norm_g[l]), _reorder_w_in(w_in[l]), vec(cq_norm_g[l]),
            vec(ckv_norm_g[l]), vec(kidx_ln_g[l]), vec(kidx_ln_b[l]), conv_w[l].astype(F32),
            at_a(a_log[l]), at_a(dt_bias[l]), w_uq[l].astype(BF16), w_uk[l].astype(BF16),
            w_iq[l].astype(BF16), tm=tm, seq=L)
        seq = lambda a: a.reshape(B, L, a.shape[-1])
        o_a = _dsa(qlat, qidx, seq(small), seq(ckv), seq(kidx), w_uv[l].astype(BF16))
        o_b = _gdn(seq(qn), seq(kn), seq(vn), seq(zb), seq(gates), vec(onorm_g[l]))
        x2 = _merge(x2, o_a.reshape(T, -1), o_b.reshape(T, -1), ga, gb,
                    w_branch_a[l].astype(BF16), w_branch_b[l].astype(BF16),
                    w_out[l].astype(BF16), vec(ffn_norm_g[l]), w_gate[l].astype(BF16),
                    w_up[l].astype(BF16), w_down[l].astype(BF16), vec(final_norm_g),
                    tm=min(MERGE_TILE, T), final_norm=(l == depth - 1))
    return x2.reshape(B, L, D)
```

```python
import functools

import jax
import jax.numpy as jnp
from jax import lax
from jax.experimental import pallas as pl
from jax.experimental.pallas import tpu as pltpu

F32 = jnp.float32
BF16 = jnp.bfloat16

EPS = 1e-6
A_HEADS = 16
A_QK_DIM = 64
A_V_DIM = 64
A_Q_RANK = 256
A_KV_RANK = 256
IDX_HEADS = 8
IDX_DIM = 64
TOPK_MAX = 256
Q_BLOCK = 128
B_HEADS = 8
B_K_DIM = 128
B_V_DIM = 128
CONV_WIDTH = 4
CHUNK = 64

GDN_W = B_HEADS * B_K_DIM
assert B_K_DIM == B_V_DIM
LANES = 128
SUBLANES = 8
SMALL_W = LANES
OFF_WIDX = IDX_DIM
OFF_BETA = IDX_DIM + IDX_HEADS
OFF_A = IDX_DIM + IDX_HEADS + B_HEADS
VMEM_LIMIT = 56 * 1024 * 1024
NEG_BIG = -1e30
LOG2_E = 1.4426950408889634
BISECT_STEPS = 16
BISECT_UNROLL = 16
TOKEN_TILE = 512
IN_SUB = 2
MERGE_TILE = 512
MERGE_SUB = 2
HEAD_GROUP = 4
KEY_CHUNK = 512
SEARCH_GROUPS = 4
GDN_HEADS_PER_STEP = 4
PACK = 2
PREP_GROUP = 4


def _resident(shape):
    nd = len(shape)
    return pl.BlockSpec(shape, lambda *_: (0,) * nd, pipeline_mode=pl.Buffered(1))


def _rms(x, g):
    return x * lax.rsqrt(jnp.mean(x * x, axis=-1, keepdims=True) + EPS) * g


def _l2n(x, scale=1.0):
    return x * (lax.rsqrt(jnp.sum(x * x, axis=-1, keepdims=True) + EPS) * scale)


def _sigmoid(x):
    return 0.5 * jnp.tanh(0.5 * x) + 0.5


def _silu(x):
    u = 0.5 * x
    return u + u * jnp.tanh(u)


def _chunk_cumsum(g, chunk):
    pos = lax.broadcasted_iota(jnp.int32, g.shape, 0) % chunk
    d = 1
    while d < chunk:
        g = g + jnp.where(pos >= d, pltpu.roll(g, d, 0), 0.0)
        d *= 2
    return g


def _conv_silu_tile(x, tail_ref, w_half):
    full = jnp.concatenate([tail_ref[...], x], axis=0)
    u = x * w_half[CONV_WIDTH - 1:CONV_WIDTH, :]
    for d in range(1, CONV_WIDTH):
        u = u + pltpu.roll(full, d, 0)[SUBLANES:] * w_half[CONV_WIDTH - 1 - d:CONV_WIDTH - d, :]
    tail_ref[...] = x[x.shape[0] - SUBLANES:]
    return u + u * jnp.tanh(u)


def _in_proj_kernel(x_ref, g_ref, w_ref, cqg_ref, ckvg_ref, lng_ref, lnb_ref, cw_ref,
                    alog_ref, dtb_ref, wuq_ref, wuk_ref, wiq_ref,
                    qlat_ref, qidx_ref, ckv_ref, kidx_ref, small_ref, gates_ref, q_ref, k_ref, v_ref,
                    z_ref, ga_ref, gb_ref, tail_ref, *, d_model, tiles_per_seq):
    @pl.when(pl.program_id(0) % tiles_per_seq == 0)
    def _():
        tail_ref[...] = jnp.zeros_like(tail_ref)

    n = x_ref.shape[0] // IN_SUB
    subs = [slice(s * n, (s + 1) * n) for s in range(IN_SUB)]
    hs = [_rms(x_ref[r, :], g_ref[...]).astype(BF16) for r in subs]

    def proj(s, c0, width):
        return jnp.dot(hs[s], w_ref[:, c0:c0 + width], preferred_element_type=F32)

    pair = 2 * B_K_DIM
    plain, c0 = [], 3 * GDN_W
    for ref, width in ((z_ref, GDN_W), (ga_ref, d_model), (gb_ref, d_model)):
        plain += [(ref, c, c0 + c) for c in range(0, width, pair)]
        c0 += width
    steps = [(part, c) for part in range(3) for c in range(0, GDN_W, pair)]
    for idx, (part, c) in enumerate(steps):
        ref = (q_ref, k_ref, v_ref)[part]
        cc = part * GDN_W + c
        lo, hi = idx * len(plain) // len(steps), (idx + 1) * len(plain) // len(steps)
        for s, r in enumerate(subs):
            y = _conv_silu_tile(proj(s, cc, pair), tail_ref.at[:, cc:cc + pair],
                                cw_ref[:, cc:cc + pair] * 0.5)
            if part < 2:
                scale = B_K_DIM ** -0.5 if part == 0 else 1.0
                y = jnp.concatenate(
                    [_l2n(y[:, :B_K_DIM], scale), _l2n(y[:, B_K_DIM:], scale)], axis=1)
            ref[r, c:c + pair] = y.astype(ref.dtype)
            for pref, pc, wc in plain[lo:hi]:
                pref[r, pc:pc + pair] = proj(s, wc, pair)
    blocks = n // Q_BLOCK
    for s, r in enumerate(subs):
        c1 = c0
        cq = _rms(proj(s, c1, A_Q_RANK), cqg_ref[...]).astype(BF16)
        c1 += A_Q_RANK
        qa = jnp.dot(cq, wuq_ref[...], preferred_element_type=F32).astype(BF16)
        qi = jnp.dot(cq, wiq_ref[...], preferred_element_type=F32).astype(BF16)
        for hd in range(A_HEADS):
            ql = (jnp.dot(qa[:, hd * A_QK_DIM:(hd + 1) * A_QK_DIM], wuk_ref[hd],
                          preferred_element_type=F32) * (A_QK_DIM ** -0.5 * LOG2_E)).astype(BF16)
            for j in range(blocks):
                qlat_ref[s * blocks + j, hd] = ql[j * Q_BLOCK:(j + 1) * Q_BLOCK]
        for hd in range(IDX_HEADS):
            for j in range(blocks):
                qidx_ref[s * blocks + j, hd] = qi[j * Q_BLOCK:(j + 1) * Q_BLOCK,
                                                  hd * IDX_DIM:(hd + 1) * IDX_DIM]
        ckv_ref[r, :] = _rms(proj(s, c1, A_KV_RANK), ckvg_ref[...]).astype(BF16)
        c1 += A_KV_RANK
        small = proj(s, c1, SMALL_W)
        small_ref[r, :] = small
        kraw = small[:, :IDX_DIM]
        mu = jnp.mean(kraw, axis=-1, keepdims=True)
        kc = kraw - mu
        kn = kc * lax.rsqrt(jnp.mean(kc * kc, axis=-1, keepdims=True) + EPS)
        kidx_ref[r, :] = (kn * lng_ref[...] + lnb_ref[...]).astype(BF16)
        g = -jnp.exp(alog_ref[...]) * jax.nn.softplus(small + dtb_ref[...])
        lane = lax.broadcasted_iota(jnp.int32, small.shape, 1)
        gates_ref[r, :] = jnp.where(lane >= OFF_A, _chunk_cumsum(g, CHUNK), _sigmoid(small))


def _in_proj(x2, g, w_all, cq_g, ckv_g, ln_g, ln_b, conv_w, alog_v, dtb_v, w_uq, w_uk, w_iq,
             *, tm, seq):
    T, D = x2.shape
    n_all = w_all.shape[1]
    assert seq % tm == 0 and tm % (IN_SUB * Q_BLOCK) == 0 and Q_BLOCK % CHUNK == 0
    row = lambda w: pl.BlockSpec((tm, w), lambda i: (i, 0))
    qblk = lambda heads, w: pl.BlockSpec((tm // Q_BLOCK, heads, Q_BLOCK, w),
                                         lambda i: (i, 0, 0, 0))
    wide = ((GDN_W, BF16), (GDN_W, BF16), (GDN_W, BF16), (GDN_W, F32),
            (D, F32), (D, F32))
    out_shapes = (
        jax.ShapeDtypeStruct((T // Q_BLOCK, A_HEADS, Q_BLOCK, A_KV_RANK), BF16),
        jax.ShapeDtypeStruct((T // Q_BLOCK, IDX_HEADS, Q_BLOCK, IDX_DIM), BF16),
        jax.ShapeDtypeStruct((T, A_KV_RANK), BF16),
        jax.ShapeDtypeStruct((T, IDX_DIM), BF16),
        jax.ShapeDtypeStruct((T, SMALL_W), F32),
        jax.ShapeDtypeStruct((T, SMALL_W), F32),
    ) + tuple(jax.ShapeDtypeStruct((T, w), dt) for w, dt in wide)
    out_specs = (qblk(A_HEADS, A_KV_RANK), qblk(IDX_HEADS, IDX_DIM), row(A_KV_RANK),
                 row(IDX_DIM), row(SMALL_W), row(SMALL_W)) + tuple(row(w) for w, _ in wide)
    return pl.pallas_call(
        functools.partial(_in_proj_kernel, d_model=D, tiles_per_seq=seq // tm),
        grid=(T // tm,),
        in_specs=[row(D), _resident((1, D)), _resident((D, n_all)),
                  _resident((1, A_Q_RANK)), _resident((1, A_KV_RANK)),
                  _resident((1, IDX_DIM)), _resident((1, IDX_DIM)),
                  _resident(conv_w.shape), _resident((1, SMALL_W)), _resident((1, SMALL_W)),
                  _resident(w_uq.shape), _resident(w_uk.shape), _resident(w_iq.shape)],
        out_specs=out_specs,
        out_shape=out_shapes,
        scratch_shapes=[pltpu.VMEM((SUBLANES, conv_w.shape[1]), F32)],
        compiler_params=pltpu.CompilerParams(
            dimension_semantics=("arbitrary",), vmem_limit_bytes=VMEM_LIMIT),
        name="in_proj",
    )(x2, g, w_all, cq_g, ckv_g, ln_g, ln_b, conv_w, alog_v, dtb_v, w_uq, w_uk, w_iq)


def _topk_bias(score_ref, bias_ref, row0, n_chunks, topk, tri_ones):
    _, nq, kc = score_ref.shape
    rg = nq // SEARCH_GROUPS
    groups = range(SEARCH_GROUPS)
    rsum = lambda x: jnp.sum(x, axis=1, keepdims=True)
    rmin = lambda x: jnp.min(x, axis=1, keepdims=True)

    def ld(g):
        return jnp.concatenate(
            [score_ref[c, g * rg:(g + 1) * rg, :] for c in range(n_chunks)], axis=1)

    pos = lambda g: row0 + g * rg + lax.broadcasted_iota(jnp.int32, (rg, 1), 0)
    k_eff = [jnp.minimum(pos(g) + 1, topk).astype(F32) for g in groups]
    lo = tuple(rmin(jnp.where(ld(g) == -jnp.inf, jnp.inf, ld(g))) for g in groups)
    hi = tuple(jnp.max(ld(g), axis=1, keepdims=True) for g in groups)

    def bisect(_, carry):
        lo, hi = carry
        mid = [0.5 * (a + b) for a, b in zip(lo, hi)]
        cnt = [rsum(jnp.where(ld(g) >= mid[g], 1.0, 0.0)) for g in groups]
        ge = [cnt[g] >= k_eff[g] for g in groups]
        return (tuple(jnp.where(ge[g], mid[g], lo[g]) for g in groups),
                tuple(jnp.where(ge[g], hi[g], mid[g]) for g in groups))

    lo, hi = lax.fori_loop(0, BISECT_STEPS, bisect, (lo, hi), unroll=BISECT_UNROLL)

    def above(g, v):
        s = ld(g)
        gt = s > v
        return rsum(jnp.where(gt, 1.0, 0.0)), rmin(jnp.where(gt, s, jnp.inf))

    v0 = tuple(rmin(jnp.where(ld(g) >= lo[g], ld(g), jnp.inf)) for g in groups)
    first = [above(g, v0[g]) for g in groups]

    def cond(c):
        _, n_gt, _ = c
        flags = [jnp.max(jnp.where(n_gt[g] >= k_eff[g], 1.0, 0.0)) for g in groups]
        return functools.reduce(jnp.maximum, flags) > 0.0

    def body(c):
        v, n_gt, nxt = c
        v = tuple(jnp.where(n_gt[g] >= k_eff[g], nxt[g], v[g]) for g in groups)
        nxt_state = [above(g, v[g]) for g in groups]
        return v, tuple(a for a, _ in nxt_state), tuple(b for _, b in nxt_state)

    tau, n_gt, _ = lax.while_loop(
        cond, body, (v0, tuple(a for a, _ in first), tuple(b for _, b in first)))
    k_all = jnp.concatenate(k_eff, axis=0)
    need = k_all - jnp.concatenate(n_gt, axis=0)
    tau = jnp.concatenate(tau, axis=0)

    n_ge = jnp.zeros((nq, LANES), F32)
    for c in range(n_chunks):
        for j in range(kc // LANES):
            ge = score_ref[c, :, j * LANES:(j + 1) * LANES] >= tau
            bias_ref[c, :, j * LANES:(j + 1) * LANES] = jnp.where(ge, 0.0, NEG_BIG)
            n_ge = n_ge + jnp.where(ge, 1.0, 0.0)
    surplus = jnp.max(jnp.where(rsum(n_ge) > k_all, 1.0, 0.0)) > 0.0

    @pl.when(surplus)
    def _():
        carry = jnp.zeros((nq, LANES), F32)
        for c in range(n_chunks):
            for j in range(kc // LANES):
                sc = score_ref[c, :, j * LANES:(j + 1) * LANES]
                eq = sc == tau
                r = jnp.dot(jnp.where(eq, 1.0, 0.0).astype(BF16), tri_ones,
                            preferred_element_type=F32)
                prefix = r[:, :LANES] + carry
                carry = carry + r[:, LANES:]
                sel = (sc > tau) | (eq & (prefix <= need))
                bias_ref[c, :, j * LANES:(j + 1) * LANES] = jnp.where(sel, 0.0, NEG_BIG)


def _dsa_kernel(qlat_ref, qidx_ref, small_ref, ckv_ref, kidx_ref, wuv_ref,
                o_ref, score_ref, bias_ref, s_ref, m_ref, l_ref, acc_ref, *, topk):
    i = pl.program_id(1)
    nq = Q_BLOCK
    max_chunks, _, kc = score_ref.shape
    n_kc = (i * nq) // kc + 1
    ng = A_HEADS // HEAD_GROUP
    rows = HEAD_GROUP * nq
    tiles = kc // LANES
    nt = (((1,), (1,)), ((), ()))
    w_idx = small_ref[0][:, OFF_WIDX:OFF_WIDX + IDX_HEADS] * (IDX_HEADS ** -0.5 * IDX_DIM ** -0.5)
    row_pos = i * nq + lax.broadcasted_iota(jnp.int32, (nq, 1), 0)

    r_i = lax.broadcasted_iota(jnp.int32, (LANES, 2 * LANES), 0)
    c_i = lax.broadcasted_iota(jnp.int32, (LANES, 2 * LANES), 1)
    tri_ones = jnp.where((c_i >= LANES) | (r_i <= c_i), 1.0, 0.0).astype(BF16)

    def key_rows(c):
        return slice(c * kc, (c + 1) * kc)

    def score_chunk(c):
        logits = lax.dot_general(qidx_ref[0].reshape(IDX_HEADS * nq, IDX_DIM),
                                 kidx_ref[0, key_rows(c), :], nt,
                                 preferred_element_type=F32)
        sc = jnp.zeros((nq, kc), F32)
        for h in range(IDX_HEADS):
            sc = sc + w_idx[:, h:h + 1] * jnp.maximum(logits[h * nq:(h + 1) * nq], 0.0)
        col = c * kc + lax.broadcasted_iota(jnp.int32, (nq, kc), 1)
        score_ref[c] = jnp.where(col <= row_pos, sc, -jnp.inf)

    def pass1(c):
        ckv = ckv_ref[0, key_rows(c), :]
        bias = bias_ref[c]
        ss = [lax.dot_general(
            qlat_ref[0, g * HEAD_GROUP:(g + 1) * HEAD_GROUP].reshape(rows, A_KV_RANK), ckv, nt,
            preferred_element_type=F32) for g in range(ng)]
        for g in range(ng):
            s = (ss[g].reshape(HEAD_GROUP, nq, kc) + bias[None]).reshape(rows, kc)
            s_ref[c, g] = s
            mt = s[:, :LANES]
            for j in range(1, tiles):
                mt = jnp.maximum(mt, s[:, j * LANES:(j + 1) * LANES])
            m_ref[g] = mt if c == 0 else jnp.maximum(m_ref[g], mt)

    def pass2(c):
        ckv = ckv_ref[0, key_rows(c), :]
        for g in range(ng):
            s = s_ref[c, g]
            m = m_ref[g]
            p = [jnp.exp2(s[:, j * LANES:(j + 1) * LANES] - m) for j in range(tiles)]
            lsum = functools.reduce(lambda a, b: a + b, p)
            pv = jnp.dot(jnp.concatenate(p, axis=1).astype(BF16), ckv,
                         preferred_element_type=F32)
            l_ref[g] = lsum if c == 0 else l_ref[g] + lsum
            acc_ref[g] = pv if c == 0 else acc_ref[g] + pv

    def block(nk, search):
        for c in range(nk):
            score_chunk(c)
        if search:
            _topk_bias(score_ref, bias_ref, i * nq, nk, topk, tri_ones)
        else:
            bias_ref[0] = jnp.where(score_ref[0] > -jnp.inf, 0.0, NEG_BIG)
        for c in range(nk):
            pass1(c)
        for g in range(ng):
            m_ref[g] = jnp.broadcast_to(jnp.max(m_ref[g], axis=1, keepdims=True), (rows, LANES))
        for c in range(nk):
            pass2(c)
        for g in range(ng):
            l = jnp.sum(l_ref[g], axis=1, keepdims=True)
            og = (acc_ref[g] * (1.0 / l)).astype(BF16)
            for hh in range(HEAD_GROUP):
                h = g * HEAD_GROUP + hh
                o_ref[0, :, h * A_V_DIM:(h + 1) * A_V_DIM] = jnp.dot(
                    og[hh * nq:(hh + 1) * nq], wuv_ref[h],
                    preferred_element_type=F32).astype(o_ref.dtype)

    keep_all = (i + 1) * nq <= topk
    pl.when(keep_all)(functools.partial(block, 1, False))
    for v in range(max_chunks):
        pl.when(jnp.logical_and(n_kc == v + 1, jnp.logical_not(keep_all)))(
            functools.partial(block, v + 1, True))


def _dsa(qlat, qidx, small, ckv, kidx, w_uv):
    B, L, _ = ckv.shape
    topk = min(TOPK_MAX, L // 4)
    kc = min(KEY_CHUNK, L)
    assert L % kc == 0 and kc % Q_BLOCK == 0 and topk <= kc
    n_chunks = L // kc
    nb = L // Q_BLOCK
    ng = A_HEADS // HEAD_GROUP
    rows = HEAD_GROUP * Q_BLOCK
    a_width = A_HEADS * A_V_DIM
    return pl.pallas_call(
        functools.partial(_dsa_kernel, topk=topk),
        grid=(B, nb),
        in_specs=[
            pl.BlockSpec((1, A_HEADS, Q_BLOCK, A_KV_RANK), lambda b, i: (b * nb + i, 0, 0, 0)),
            pl.BlockSpec((1, IDX_HEADS, Q_BLOCK, IDX_DIM), lambda b, i: (b * nb + i, 0, 0, 0)),
            pl.BlockSpec((1, Q_BLOCK, SMALL_W), lambda b, i: (b, i, 0)),
            pl.BlockSpec((1, L, A_KV_RANK), lambda b, i: (b, 0, 0)),
            pl.BlockSpec((1, L, IDX_DIM), lambda b, i: (b, 0, 0)),
            _resident(w_uv.shape),
        ],
        out_specs=pl.BlockSpec((1, Q_BLOCK, a_width), lambda b, i: (b, i, 0)),
        out_shape=jax.ShapeDtypeStruct((B, L, a_width), BF16),
        scratch_shapes=[
            pltpu.VMEM((n_chunks, Q_BLOCK, kc), F32),
            pltpu.VMEM((n_chunks, Q_BLOCK, kc), F32),
            pltpu.VMEM((n_chunks, ng, rows, kc), F32),
            pltpu.VMEM((ng, rows, LANES), F32),
            pltpu.VMEM((ng, rows, LANES), F32),
            pltpu.VMEM((ng, rows, A_KV_RANK), F32),
        ],
        compiler_params=pltpu.CompilerParams(
            dimension_semantics=("arbitrary", "arbitrary"), vmem_limit_bytes=VMEM_LIMIT),
        name="dsa",
    )(qlat, qidx, small, ckv, kidx, w_uv)


def _lane_bcast(x, c):
    lane = lax.broadcasted_iota(jnp.int32, x.shape, 1)
    col = jnp.sum(jnp.where(lane == c, x, 0.0), axis=1, keepdims=True)
    return jnp.broadcast_to(col, x.shape)


def _gdn_kernel(q_ref, k_ref, v_ref, z_ref, gates_ref, og_ref, o_ref,
                mneg_ref, r_ref, qeff_ref, o0_ref, at_ref, *, seq, hp):
    hg = pl.program_id(1)
    C = CHUNK
    n_chunks = seq // C
    dk = B_K_DIM

    R = PACK * C
    rr = lax.broadcasted_iota(jnp.int32, (R, R), 0)
    cc = lax.broadcasted_iota(jnp.int32, (R, R), 1)
    same = (rr // C) == (cc // C)
    tri = same & (rr >= cc)
    strict = same & (rr > cc)
    eye = jnp.where(rr == cc, 1.0, 0.0)
    kcol = lax.broadcasted_iota(jnp.int32, (dk, R), 1) // C
    nt = (((1,), (1,)), ((), ()))
    dot = functools.partial(jnp.dot, preferred_element_type=F32)

    def prep(t, _):
        ids = [(hh, t * PREP_GROUP + j) for j in range(PREP_GROUP) for hh in range(hp)]
        each = lambda f, *cols: [f(*a) for a in zip(*cols)]
        rows = [pl.ds(pl.multiple_of(n * R, R), R) for _, n in ids]
        load = lambda ref: [ref[0, r, hh * dk:(hh + 1) * dk] for (hh, _), r in zip(ids, rows)]
        qb, kb, vb = (load(r) for r in (q_ref, k_ref, v_ref))
        q, k, v = (each(lambda a: a.astype(F32), x) for x in (qb, kb, vb))
        gt = [gates_ref[0, r, :] for r in rows]
        beta = [_lane_bcast(a, OFF_BETA + hg * hp + hh) for a, (hh, _) in zip(gt, ids)]
        G = [_lane_bcast(a, OFF_A + hg * hp + hh) for a, (hh, _) in zip(gt, ids)]
        Gl = each(lambda g: jnp.concatenate(
            [jnp.broadcast_to(g[(p + 1) * C - 1:(p + 1) * C, :], (C, LANES)) for p in range(PACK)],
            axis=0), G)
        decay = each(lambda g: jnp.exp(jnp.where(tri, g[:, :R] - g.T[:R, :R], -jnp.inf)), G)
        eG = each(jnp.exp, G)
        qkk = each(lambda q_, k_: lax.dot_general(
            jnp.concatenate([q_, k_], axis=0), k_, nt, preferred_element_type=F32), qb, kb)
        N = each(lambda b, a, d: jnp.where(strict, b[:, :R] * a[R:] * d, 0.0), beta, qkk, decay)
        X = each(lambda a: eye - a, N)
        Nb = each(lambda a: a.astype(BF16), N)
        Pb = each(lambda a: dot(a, a).astype(BF16), Nb)
        steps = C.bit_length() - 2
        for it in range(steps):
            if it + 1 < steps:
                xp = each(lambda x, p: dot(jnp.concatenate([x.astype(BF16), p], axis=0), p), X, Pb)
                X = each(lambda x, a: x + a[:R], X, xp)
                Pb = each(lambda a: a[R:].astype(BF16), xp)
            else:
                X = each(lambda x, p: x + dot(x.astype(BF16), p), X, Pb)
        rhs = each(lambda v_, k_, b, e: jnp.concatenate(
            [v_ * b, k_ * (b * e)], axis=-1).astype(BF16), v, k, beta, eG)
        sol = each(lambda x, r: dot(x.astype(BF16), r).astype(BF16), X, rhs)
        qk = each(lambda a, d: (a[:R] * d).astype(BF16), qkk, decay)
        ktT = each(lambda k_, g, gl: (k_ * jnp.exp(gl - g)).T.astype(BF16), k, G, Gl)
        ktbd = each(lambda a: jnp.concatenate(
            [jnp.where(kcol == p, a, jnp.zeros_like(a)) for p in range(PACK)], axis=0), ktT)
        kts = each(dot, ktbd, sol)
        qks = each(dot, qk, sol)
        for i, (hh, n) in enumerate(ids):
            qe = q[i] * eG[i]
            for p in range(PACK):
                ch = n * PACK + p
                r_ref[hh, ch] = kts[i][p * dk:(p + 1) * dk, :B_V_DIM]
                mneg_ref[hh, ch] = kts[i][p * dk:(p + 1) * dk, B_V_DIM:].astype(BF16)
                o0_ref[hh, ch] = qks[i][p * C:(p + 1) * C, :B_V_DIM]
                qeff_ref[hh, ch] = (qe[p * C:(p + 1) * C]
                                    - qks[i][p * C:(p + 1) * C, B_V_DIM:]).astype(BF16)
                at_ref[hh, ch] = jnp.exp(G[i][(p + 1) * C - 1:(p + 1) * C, :])
        return 0

    lax.fori_loop(0, n_chunks // (PREP_GROUP * PACK), prep, 0, unroll=2)

    og = og_ref[...]

    def emit(n, o):
        rows = pl.ds(n * C if isinstance(n, int) else pl.multiple_of(n * C, C), C)
        for hh in range(hp):
            ls = slice(hh * dk, (hh + 1) * dk)
            o_ref[0, rows, ls] = (_rms(o[hh], og) * _silu(z_ref[0, rows, ls])).astype(o_ref.dtype)

    def scan(n, carry):
        S, o_prev = carry
        emit((n + n_chunks - 1) % n_chunks, o_prev)
        Sb = [s.astype(BF16) for s in S]
        ms = [dot(mneg_ref[hh, n], Sb[hh]) for hh in range(hp)]
        os_ = [dot(qeff_ref[hh, n], Sb[hh]) for hh in range(hp)]
        new = tuple(S[hh] * at_ref[hh, n] - ms[hh] + r_ref[hh, n] for hh in range(hp))
        return new, tuple(os_[hh] + o0_ref[hh, n] for hh in range(hp))

    zeros = lambda rows_: tuple(jnp.zeros((rows_, B_V_DIM), F32) for _ in range(hp))
    _, o_last = lax.fori_loop(0, n_chunks, scan, (zeros(dk), zeros(C)), unroll=2)
    emit(n_chunks - 1, o_last)


def _gdn(qn, kn, vn, zb, gates, onorm_g):
    B, L, W = qn.shape
    hp = GDN_HEADS_PER_STEP
    n_chunks = L // CHUNK
    heads = W // B_K_DIM
    assert heads % hp == 0 and n_chunks % (PREP_GROUP * PACK) == 0 and PACK * CHUNK <= LANES
    col = pl.BlockSpec((1, L, hp * B_K_DIM), lambda b, h: (b, 0, h))
    return pl.pallas_call(
        functools.partial(_gdn_kernel, seq=L, hp=hp),
        grid=(B, heads // hp),
        in_specs=[col, col, col, col,
                  pl.BlockSpec((1, L, SMALL_W), lambda b, h: (b, 0, 0)),
                  _resident((1, B_V_DIM))],
        out_specs=col,
        out_shape=jax.ShapeDtypeStruct((B, L, W), BF16),
        scratch_shapes=[
            pltpu.VMEM((hp, n_chunks, B_K_DIM, B_K_DIM), BF16),
            pltpu.VMEM((hp, n_chunks, B_K_DIM, B_V_DIM), F32),
            pltpu.VMEM((hp, n_chunks, CHUNK, B_K_DIM), BF16),
            pltpu.VMEM((hp, n_chunks, CHUNK, B_V_DIM), F32),
            pltpu.VMEM((hp, n_chunks, 1, LANES), F32),
        ],
        compiler_params=pltpu.CompilerParams(
            dimension_semantics=("arbitrary", "arbitrary"), vmem_limit_bytes=VMEM_LIMIT),
        name="gdn",
    )(qn, kn, vn, zb, gates, onorm_g)


def _merge_kernel(x_ref, oa_ref, ob_ref, ga_ref, gb_ref, wa_ref, wb_ref, wo_ref,
                  fg_ref, wg_ref, wu_ref, wd_ref, og_ref, o_ref, *, ff_chunk, final_norm):
    dot = functools.partial(jnp.dot, preferred_element_type=F32)
    n = x_ref.shape[0] // MERGE_SUB
    subs = [slice(s * n, (s + 1) * n) for s in range(MERGE_SUB)]
    ya = [dot(oa_ref[r, :], wa_ref[...]) for r in subs]
    yb = [dot(ob_ref[r, :], wb_ref[...]) for r in subs]
    merged = [(_sigmoid(ga_ref[r, :]) * a + _sigmoid(gb_ref[r, :]) * b).astype(BF16)
              for r, a, b in zip(subs, ya, yb)]
    acc = [x_ref[r, :] + dot(m, wo_ref[...]) for r, m in zip(subs, merged)]
    h = [_rms(a, fg_ref[...]).astype(BF16) for a in acc]
    d_ff = wg_ref.shape[1]
    for c0 in range(0, d_ff, ff_chunk):
        gate = [dot(v, wg_ref[:, c0:c0 + ff_chunk]) for v in h]
        up = [dot(v, wu_ref[:, c0:c0 + ff_chunk]) for v in h]
        act = [(_silu(g) * u).astype(BF16) for g, u in zip(gate, up)]
        acc = [a + dot(v, wd_ref[c0:c0 + ff_chunk, :]) for a, v in zip(acc, act)]
    for r, a in zip(subs, acc):
        o_ref[r, :] = _rms(a, og_ref[...]) if final_norm else a


def _merge(x2, oa, ob, ga, gb, wa, wb, wo, fg, wg, wu, wd, og, *, tm, final_norm):
    T, D = x2.shape
    d_ff = wg.shape[1]
    ff_chunk = d_ff // 2 if (d_ff // 2) % LANES == 0 else d_ff
    row = pl.BlockSpec((tm, D), lambda i: (i, 0))
    return pl.pallas_call(
        functools.partial(_merge_kernel, ff_chunk=ff_chunk, final_norm=final_norm),
        grid=(T // tm,),
        in_specs=[row, row, row, row, row,
                  _resident(wa.shape), _resident(wb.shape), _resident(wo.shape),
                  _resident(fg.shape), _resident(wg.shape), _resident(wu.shape),
                  _resident(wd.shape), _resident(og.shape)],
        out_specs=row,
        out_shape=jax.ShapeDtypeStruct((T, D), F32),
        compiler_params=pltpu.CompilerParams(
            dimension_semantics=("arbitrary",), vmem_limit_bytes=VMEM_LIMIT),
        name="merge_ffn",
    )(x2, oa, ob, ga, gb, wa, wb, wo, fg, wg, wu, wd, og)


def _reorder_w_in(w):
    sizes = (A_Q_RANK, A_KV_RANK, IDX_DIM, IDX_HEADS,
             B_HEADS * B_K_DIM, B_HEADS * B_K_DIM, B_HEADS * B_V_DIM, B_HEADS, B_HEADS,
             B_HEADS * B_V_DIM, w.shape[0], w.shape[0])
    parts, c0 = [], 0
    for s in sizes:
        parts.append(w[:, c0:c0 + s])
        c0 += s
    (c_q, c_kv, k_idx, w_idx, q_b, k_b, v_b, beta_b, a_b, z_b, gate_a, gate_b) = parts
    pad = jnp.zeros((w.shape[0], SMALL_W - IDX_DIM - IDX_HEADS - 2 * B_HEADS), w.dtype)
    return jnp.concatenate(
        [q_b, k_b, v_b, z_b, gate_a, gate_b, c_q, c_kv, k_idx, w_idx, beta_b, a_b, pad],
        axis=1).astype(BF16)


def kernel(x, mix_norm_g, w_in, cq_norm_g, ckv_norm_g, w_uq, w_uk, w_uv, w_iq, kidx_ln_g, kidx_ln_b, w_branch_a, conv_w, a_log, dt_bias, onorm_g, w_branch_b, w_out, ffn_norm_g, w_gate, w_up, w_down, final_norm_g):
    B, L, D = x.shape
    depth = w_in.shape[0]
    T = B * L
    tm = min(TOKEN_TILE, L)
    x2 = x.reshape(T, D)
    vec = lambda a: a.reshape(1, -1).astype(F32)
    at_a = lambda a: jnp.zeros((1, SMALL_W), F32).at[0, OFF_A:OFF_A + B_HEADS].set(a.astype(F32))
    for l in range(depth):
        (qlat, qidx, ckv, kidx, small, gates, qn, kn, vn, zb, ga, gb) = _in_proj(
            x2, vec(mix_norm_g[l]), _reorder_w_in(w_in[l]), vec(cq_norm_g[l]),
            vec(ckv_norm_g[l]), vec(kidx_ln_g[l]), vec(kidx_ln_b[l]), conv_w[l].astype(F32),
            at_a(a_log[l]), at_a(dt_bias[l]), w_uq[l].astype(BF16), w_uk[l].astype(BF16),
            w_iq[l].astype(BF16), tm=tm, seq=L)
        seq = lambda a: a.reshape(B, L, a.shape[-1])
        o_a = _dsa(qlat, qidx, seq(small), seq(ckv), seq(kidx), w_uv[l].astype(BF16))
        o_b = _gdn(seq(qn), seq(kn), seq(vn), seq(zb), seq(gates), vec(onorm_g[l]))
        x2 = _merge(x2, o_a.reshape(T, -1), o_b.reshape(T, -1), ga, gb,
                    w_branch_a[l].astype(BF16), w_branch_b[l].astype(BF16),
                    w_out[l].astype(BF16), vec(ffn_norm_g[l]), w_gate[l].astype(BF16),
                    w_up[l].astype(BF16), w_down[l].astype(BF16), vec(final_norm_g),
                    tm=min(MERGE_TILE, T), final_norm=(l == depth - 1))
    return x2.reshape(B, L, D)
```
